```python
import math
import jax, jax.numpy as jnp
from jax import lax
import numpy as np

D_MODEL = 2048
BATCH = 2
SEQ = 4096
DEPTH = 1
DEC_BATCH = 8
DEC_SEQ = 1
PAST_LEN = 16384
PAGE_SIZE = 128

N_META = 16
D_CONV = D_MODEL // 2
CONV_K = 3
N_HEADS = 8
HEAD_DIM = 64
V_DIM = 2 * HEAD_DIM
D_QK = N_HEADS * 2 * HEAD_DIM
D_ATTN = N_HEADS * V_DIM
D_IN = 3 * D_CONV + 2 * D_QK + D_ATTN + 2 * D_MODEL
N_BUCKETS = 32
MAX_DISTANCE = 128
N_EXPERTS = 32
TOP_K = 4
D_FF = D_MODEL
SWIGLU_LIMIT = 7.0
SWIGLU_ALPHA = 1.702
MOE_BLOCK = 128
Q_BLOCK = 128
EPS = 1e-6
NEG_INF = -1e30

kernel_name = 'hybrid_shortconv_diffattn_moe_step'


def rms_norm(x, g):
    xf = x.astype(jnp.float32)
    y = xf * lax.rsqrt(jnp.mean(xf * xf, axis=-1, keepdims=True) + EPS)
    return (y * g.astype(jnp.float32)).astype(x.dtype)


def t5_bucket(rel):
    n = jnp.maximum(rel, 0)
    max_exact = N_BUCKETS // 2
    nf = jnp.maximum(n, 1).astype(jnp.float32)
    large = max_exact + (jnp.log(nf / max_exact) / math.log(MAX_DISTANCE / max_exact)
                         * (N_BUCKETS - max_exact)).astype(jnp.int32)
    large = jnp.minimum(large, N_BUCKETS - 1)
    return jnp.where(n < max_exact, n, large)


def diff_lambda(lq1, lk1, lq2, lk2, lam_init):
    f32 = jnp.float32
    return (jnp.exp(jnp.sum(lq1.astype(f32) * lk1.astype(f32)))
            - jnp.exp(jnp.sum(lq2.astype(f32) * lk2.astype(f32))) + lam_init)


def diff_attention_core(q, k, v, q_pos, k_pos, rel_bias, lam):
    s = jnp.einsum('bqhcd,bkhcd->bchqk', q, k).astype(jnp.float32) * (HEAD_DIM ** -0.5)
    rel = q_pos[:, None] - k_pos[None, :]
    bias = jnp.moveaxis(rel_bias[t5_bucket(rel)], -1, 0).astype(jnp.float32)
    s = jnp.where(rel >= 0, s + bias, NEG_INF)
    p = jax.nn.softmax(s, axis=-1)
    a = p[:, 0] - lam * p[:, 1]
    return jnp.einsum('bhqk,bkhd->bqhd', a.astype(v.dtype), v)


def prompt_diff_attention(q, k, v, rel_bias, lam):
    b, s = q.shape[:2]
    s_pad = -(-s // Q_BLOCK) * Q_BLOCK
    pad = [(0, 0), (0, s_pad - s), (0, 0), (0, 0), (0, 0)]
    qp = jnp.pad(q, pad)
    kp = jnp.pad(k, pad)
    vp = jnp.pad(v, pad[:4])
    nb = s_pad // Q_BLOCK
    q_blocks = jnp.moveaxis(qp.reshape(b, nb, Q_BLOCK, N_HEADS, 2, HEAD_DIM), 1, 0)
    k_pos = jnp.arange(s_pad)

    def one_block(args):
        qb, bi = args
        q_pos = bi * Q_BLOCK + jnp.arange(Q_BLOCK)
        return diff_attention_core(qb, kp, vp, q_pos, k_pos, rel_bias, lam)

    o = lax.map(one_block, (q_blocks, jnp.arange(nb)))
    return jnp.moveaxis(o, 0, 1).reshape(b, s_pad, N_HEADS, V_DIM)[:, :s]


def short_conv(u, prev, w):
    t = u.shape[1]
    full = jnp.concatenate([prev.astype(u.dtype), u], axis=1)
    y = sum(full[:, j:j + t] * w[j] for j in range(CONV_K))
    return y, full[:, t:]


def _moe_block_size(n_assign):
    per_expert = max(1, n_assign // N_EXPERTS)
    return min(MOE_BLOCK, max(8, 1 << (per_expert.bit_length() - 1)))


def moe(x, w_router, b_router, w_mlp1, b_mlp1, w_mlp2, b_mlp2):
    lead = x.shape[:-1]
    xt = x.reshape(-1, D_MODEL)
    t = xt.shape[0]
    logits = (xt @ w_router + b_router).astype(jnp.float32)
    top_v, top_i = lax.top_k(logits, TOP_K)
    gates = jax.nn.softmax(top_v, axis=-1)
    n_assign = t * TOP_K
    blk = _moe_block_size(n_assign)
    n_blocks = -(-n_assign // blk) + N_EXPERTS
    flat_e = top_i.reshape(-1)
    flat_tok = jnp.arange(n_assign) // TOP_K
    flat_gate = gates.reshape(-1)
    order = jnp.argsort(flat_e)
    se = flat_e[order]
    counts = jnp.bincount(flat_e, length=N_EXPERTS)
    padded = (counts + blk - 1) // blk * blk
    pad_end = jnp.cumsum(padded)
    pad_start = pad_end - padded
    start = jnp.cumsum(counts) - counts
    dest = pad_start[se] + jnp.arange(n_assign) - start[se]
    n_slots = n_blocks * blk
    slot_tok = jnp.zeros((n_slots,), jnp.int32).at[dest].set(flat_tok[order].astype(jnp.int32))
    slot_gate = jnp.zeros((n_slots,), jnp.float32).at[dest].set(flat_gate[order])
    block_e = jnp.minimum(jnp.searchsorted(pad_end, jnp.arange(n_blocks) * blk, side='right'),
                          N_EXPERTS - 1)

    def run_block(args):
        tok, e = args
        xb = xt[tok]
        hcat = xb @ w_mlp1[e] + b_mlp1[e]
        g = jnp.minimum(hcat[:, :D_FF], SWIGLU_LIMIT)
        up = jnp.clip(hcat[:, D_FF:], -SWIGLU_LIMIT, SWIGLU_LIMIT)
        act = g * jax.nn.sigmoid(SWIGLU_ALPHA * g) * (up + 1.0)
        return act @ w_mlp2[e] + b_mlp2[e]

    y = lax.map(run_block, (slot_tok.reshape(n_blocks, blk), block_e))
    y = y.reshape(n_slots, D_MODEL) * slot_gate[:, None].astype(y.dtype)
    out = jnp.zeros_like(xt).at[slot_tok].add(y.astype(xt.dtype))
    return out.reshape(*lead, D_MODEL)


def hybrid_layer(h, conv_prev, attend, lam_init, norm1_g, w_in, conv_w, q_norm_g, k_norm_g, subln_g,
                 w_branch_a, w_branch_b, w_out, norm2_g, w_router, b_router,
                 w_mlp1, b_mlp1, w_mlp2, b_mlp2):
    bn, t, _ = h.shape
    xn = rms_norm(h, norm1_g)
    z = xn @ w_in
    cuts = [D_CONV, 2 * D_CONV, 3 * D_CONV, 3 * D_CONV + D_QK, 3 * D_CONV + 2 * D_QK,
            3 * D_CONV + 2 * D_QK + D_ATTN, 3 * D_CONV + 2 * D_QK + D_ATTN + D_MODEL]
    b_gate, c_gate, x_tilde, q, k, v, g_a, g_b = jnp.split(z, cuts, axis=-1)
    conv_y, conv_state = short_conv(c_gate * x_tilde, conv_prev, conv_w)
    y_a = (b_gate * conv_y) @ w_branch_a
    q = rms_norm(q.reshape(bn, t, N_HEADS, 2, HEAD_DIM), q_norm_g)
    k = rms_norm(k.reshape(bn, t, N_HEADS, 2, HEAD_DIM), k_norm_g)
    v = v.reshape(bn, t, N_HEADS, V_DIM)
    o = attend(q, k, v)
    o = rms_norm(o, subln_g) * (1.0 - lam_init)
    y_b = o.reshape(bn, t, D_ATTN) @ w_branch_b
    h = h + (jax.nn.sigmoid(g_a) * y_a + jax.nn.sigmoid(g_b) * y_b) @ w_out
    h = h + moe(rms_norm(h, norm2_g), w_router, b_router, w_mlp1, b_mlp1, w_mlp2, b_mlp2)
    return h, k.reshape(bn, t, N_HEADS, 2 * HEAD_DIM), v, conv_state


def setup_inputs(seed: int = 0) -> dict:
    key = jax.random.key(seed)
    ks = jax.random.split(key, 32)
    f32 = jnp.float32
    n_pages = PAST_LEN // PAGE_SIZE
    n_pool = (DEC_BATCH * n_pages * 5 + 3) // 4

    def nrm(k, shape, scale):
        return jax.random.normal(k, shape, f32) * scale

    page_table = jax.random.permutation(ks[5], n_pool)[:DEC_BATCH * n_pages]
    page_table = page_table.reshape(DEC_BATCH, n_pages).astype(jnp.int32)
    return {
        'x_prompt': nrm(ks[0], (BATCH, SEQ, D_MODEL), 1.0),
        'x_sample': nrm(ks[1], (DEC_BATCH, DEC_SEQ, D_MODEL), 1.0),
        'cache_k': nrm(ks[2], (DEPTH, n_pool, PAGE_SIZE, N_HEADS, 2 * HEAD_DIM), 1.0),
        'cache_v': nrm(ks[3], (DEPTH, n_pool, PAGE_SIZE, N_HEADS, V_DIM), 1.0),
        'state_conv': nrm(ks[4], (DEPTH, DEC_BATCH, CONV_K - 1, D_CONV), 1.0),
        'page_table': page_table,
        'meta_tokens': nrm(ks[6], (N_META, D_MODEL), 1.0),
        'rel_bias': nrm(ks[7], (N_BUCKETS, N_HEADS), 0.5),
        'norm1_g': 1.0 + nrm(ks[8], (DEPTH, D_MODEL), 0.02),
        'w_in': nrm(ks[9], (DEPTH, D_MODEL, D_IN), D_MODEL ** -0.5),
        'conv_w': nrm(ks[10], (DEPTH, CONV_K, D_CONV), CONV_K ** -0.5),
        'q_norm_g': 1.0 + nrm(ks[11], (DEPTH, HEAD_DIM), 0.02),
        'k_norm_g': 1.0 + nrm(ks[12], (DEPTH, HEAD_DIM), 0.02),
        'lambda_q1': nrm(ks[13], (DEPTH, HEAD_DIM), 0.1),
        'lambda_k1': nrm(ks[14], (DEPTH, HEAD_DIM), 0.1),
        'lambda_q2': nrm(ks[15], (DEPTH, HEAD_DIM), 0.1),
        'lambda_k2': nrm(ks[16], (DEPTH, HEAD_DIM), 0.1),
        'subln_g': 1.0 + nrm(ks[17], (DEPTH, V_DIM), 0.02),
        'w_branch_a': nrm(ks[18], (DEPTH, D_CONV, D_MODEL), D_CONV ** -0.5),
        'w_branch_b': nrm(ks[19], (DEPTH, D_ATTN, D_MODEL), D_ATTN ** -0.5),
        'w_out': nrm(ks[20], (DEPTH, D_MODEL, D_MODEL), D_MODEL ** -0.5),
        'norm2_g': 1.0 + nrm(ks[21], (DEPTH, D_MODEL), 0.02),
        'w_router': nrm(ks[22], (DEPTH, D_MODEL, N_EXPERTS), D_MODEL ** -0.5),
        'b_router': nrm(ks[23], (DEPTH, N_EXPERTS), 0.01),
        'w_mlp1': nrm(ks[24], (DEPTH, N_EXPERTS, D_MODEL, 2 * D_FF), D_MODEL ** -0.5),
        'b_mlp1': nrm(ks[25], (DEPTH, N_EXPERTS, 2 * D_FF), 0.01),
        'w_mlp2': nrm(ks[26], (DEPTH, N_EXPERTS, D_FF, D_MODEL), D_FF ** -0.5),
        'b_mlp2': nrm(ks[27], (DEPTH, N_EXPERTS, D_MODEL), 0.01),
    }


def reference(x_prompt, x_sample, cache_k, cache_v, state_conv, page_table, meta_tokens, rel_bias,
              norm1_g, w_in, conv_w, q_norm_g, k_norm_g, lambda_q1, lambda_k1, lambda_q2, lambda_k2,
              subln_g, w_branch_a, w_branch_b, w_out, norm2_g, w_router, b_router,
              w_mlp1, b_mlp1, w_mlp2, b_mlp2):
    n_prompt = x_prompt.shape[0]
    n_dec, t_dec = x_sample.shape[0], x_sample.shape[1]
    meta = jnp.broadcast_to(meta_tokens.astype(x_prompt.dtype)[None], (n_prompt, N_META, D_MODEL))
    h_p = jnp.concatenate([meta, x_prompt], axis=1)
    h_s = x_sample
    q_pos_s = PAST_LEN + jnp.arange(t_dec)
    k_pos_s = jnp.arange(PAST_LEN + t_dec)
    k_p_rows, v_p_rows, c_p_rows = [], [], []
    k_s_rows, v_s_rows, c_s_rows = [], [], []
    for l in range(DEPTH):
        lam_init = 0.8 - 0.6 * math.exp(-0.3 * l)
        lam = diff_lambda(lambda_q1[l], lambda_k1[l], lambda_q2[l], lambda_k2[l], lam_init)

        def attend_prompt(q, k, v, lam=lam):
            return prompt_diff_attention(q, k, v, rel_bias, lam)

        conv0 = jnp.zeros((n_prompt, CONV_K - 1, D_CONV), h_p.dtype)
        h_p, kp, vp, cp = hybrid_layer(
            h_p, conv0, attend_prompt, lam_init, norm1_g[l], w_in[l], conv_w[l], q_norm_g[l],
            k_norm_g[l], subln_g[l], w_branch_a[l], w_branch_b[l], w_out[l], norm2_g[l],
            w_router[l], b_router[l], w_mlp1[l], b_mlp1[l], w_mlp2[l], b_mlp2[l])

        k_past = cache_k[l, page_table].reshape(n_dec, -1, N_HEADS, 2, HEAD_DIM)
        v_past = cache_v[l, page_table].reshape(n_dec, -1, N_HEADS, V_DIM)

        def attend_sample(q, k, v, k_past=k_past, v_past=v_past, lam=lam):
            k_all = jnp.concatenate([k_past.astype(k.dtype), k], axis=1)
            v_all = jnp.concatenate([v_past.astype(v.dtype), v], axis=1)
            return diff_attention_core(q, k_all, v_all, q_pos_s, k_pos_s, rel_bias, lam)

        h_s, ks_, vs_, cs_ = hybrid_layer(
            h_s, state_conv[l], attend_sample, lam_init, norm1_g[l], w_in[l], conv_w[l], q_norm_g[l],
            k_norm_g[l], subln_g[l], w_branch_a[l], w_branch_b[l], w_out[l], norm2_g[l],
            w_router[l], b_router[l], w_mlp1[l], b_mlp1[l], w_mlp2[l], b_mlp2[l])

        k_p_rows.append(kp)
        v_p_rows.append(vp)
        c_p_rows.append(cp)
        k_s_rows.append(ks_)
        v_s_rows.append(vs_)
        c_s_rows.append(cs_)

    y_prompt = h_p[:, N_META:]
    y_sample = h_s
    return (y_prompt, y_sample, jnp.stack(k_p_rows), jnp.stack(v_p_rows), jnp.stack(c_p_rows),
            jnp.stack(k_s_rows), jnp.stack(v_s_rows), jnp.stack(c_s_rows))
```

```python
import functools
import math

import jax
import jax.numpy as jnp
import numpy as np
from jax import lax
from jax.experimental import pallas as pl
from jax.experimental.pallas import tpu as pltpu

D_MODEL = 2048
N_META = 16
D_CONV = 1024
CONV_K = 3
N_HEADS = 8
HEAD_DIM = 64
V_DIM = 128
D_QK = 1024
D_ATTN = 1024
D_IN = 10240
N_BUCKETS = 32
MAX_DISTANCE = 128
N_EXPERTS = 32
TOP_K = 4
D_FF = 2048
SWIGLU_LIMIT = 7.0
SWIGLU_ALPHA = 1.702
EPS = 1e-6
NEG_INF = -1e30
PAGE_SIZE = 128

F32 = jnp.float32
BF16 = jnp.bfloat16
I32 = jnp.int32
U32 = jnp.uint32

LANES = 128
SUB_ROWS = 128
VMEM_LIMIT = 56 * 1024 * 1024


def _cparams(sem, vmem=VMEM_LIMIT):
    return pltpu.CompilerParams(dimension_semantics=sem, vmem_limit_bytes=vmem)


def _in_proj_kernel(x_ref, g_ref, w_ref, z_ref, xn_ref):
    @pl.when(pl.program_id(1) == 0)
    def _():
        x = x_ref[...]
        ms = jnp.mean(x * x, axis=-1, keepdims=True)
        xn_ref[...] = (x * lax.rsqrt(ms + EPS) * g_ref[...]).astype(BF16)

    z_ref[...] = jnp.dot(xn_ref[...], w_ref[...].astype(BF16), preferred_element_type=F32)


def _in_proj(h2d, g, w, tm, tn):
    rows = h2d.shape[0]
    return pl.pallas_call(
        _in_proj_kernel,
        grid=(rows // tm, D_IN // tn),
        in_specs=[
            pl.BlockSpec((tm, D_MODEL), lambda i, j: (i, 0)),
            pl.BlockSpec((1, D_MODEL), lambda i, j: (0, 0)),
            pl.BlockSpec((D_MODEL, tn), lambda i, j: (0, j)),
        ],
        out_specs=pl.BlockSpec((tm, tn), lambda i, j: (i, j)),
        out_shape=jax.ShapeDtypeStruct((rows, D_IN), F32),
        scratch_shapes=[pltpu.VMEM((tm, D_MODEL), BF16)],
        compiler_params=_cparams(("parallel", "arbitrary")),
        name="in_proj",
    )(h2d, g.reshape(1, D_MODEL), w)


def _half_norm(x, g2, lo):
    t = x * x
    s_lo = jnp.sum(jnp.where(lo, t, 0.0), axis=-1, keepdims=True)
    s_hi = jnp.sum(jnp.where(lo, 0.0, t), axis=-1, keepdims=True)
    inv = jnp.where(lo, lax.rsqrt(s_lo * (1.0 / HEAD_DIM) + EPS), lax.rsqrt(s_hi * (1.0 / HEAD_DIM) + EPS))
    return x * inv * g2


def _prep_kernel(zb, zc, zx, zq, zk, zv, cw, qg, kg,
                 ya_ref, kout, vout, qs, kb, vb, ust, carry, *, tm, state_tile, state_row):
    i = pl.program_id(1)

    @pl.when(i == 0)
    def _():
        carry[...] = jnp.zeros_like(carry)

    u = zc[0] * zx[0]
    prev = carry[...]
    row = lax.broadcasted_iota(I32, (tm, 1), 0)
    u1 = jnp.where(row == 0, prev[7:8], pltpu.roll(u, 1, 0))
    u2 = jnp.where(row == 0, prev[6:7], jnp.where(row == 1, prev[7:8], pltpu.roll(u, 2, 0)))
    y = cw[0:1] * u2 + cw[1:2] * u1 + cw[2:3] * u
    ya_ref[0] = (zb[0] * y).astype(BF16)
    carry[...] = u[tm - 8:tm]

    @pl.when(i == state_tile)
    def _():
        ust[0] = u[state_row:state_row + 8]

    lo = lax.broadcasted_iota(I32, (1, LANES), 1) < HEAD_DIM
    for h in range(N_HEADS):
        sl = slice(h * LANES, (h + 1) * LANES)
        qn = _half_norm(zq[0, :, sl], qg[...], lo) * (HEAD_DIM ** -0.5)
        qs[0, 0, h] = jnp.where(lo, qn, 0.0).astype(BF16)
        qs[0, 1, h] = jnp.where(lo, 0.0, qn).astype(BF16)
        kn = _half_norm(zk[0, :, sl], kg[...], lo)
        kout[0, :, sl] = kn
        kb[0, h] = kn.astype(BF16)
        vb[0, h] = zv[0, :, sl].astype(BF16)
    vout[0] = zv[0]


def _prep(z3, conv_w, qg2, kg2, n_batch, t_pad, t_real, tm):
    nt = t_pad // tm
    state_tile = (t_real - 2) // tm
    state_row = ((t_real - 2) % tm) // 8 * 8
    sec = lambda s: pl.BlockSpec((1, tm, 1024), lambda b, i, s=s: (b, i, s))
    small = lambda shape: pl.BlockSpec(shape, lambda b, i: (0,) * len(shape))
    return pl.pallas_call(
        functools.partial(_prep_kernel, tm=tm, state_tile=state_tile, state_row=state_row),
        grid=(n_batch, nt),
        in_specs=[sec(0), sec(1), sec(2), sec(3), sec(4), sec(5),
                  small((CONV_K, D_CONV)), small((1, LANES)), small((1, LANES))],
        out_specs=[
            pl.BlockSpec((1, tm, D_CONV), lambda b, i: (b, i, 0)),
            pl.BlockSpec((1, tm, D_QK), lambda b, i: (b, i, 0)),
            pl.BlockSpec((1, tm, D_ATTN), lambda b, i: (b, i, 0)),
            pl.BlockSpec((1, 2, N_HEADS, tm, LANES), lambda b, i: (b, 0, 0, i, 0)),
            pl.BlockSpec((1, N_HEADS, tm, LANES), lambda b, i: (b, 0, i, 0)),
            pl.BlockSpec((1, N_HEADS, tm, LANES), lambda b, i: (b, 0, i, 0)),
            pl.BlockSpec((1, 8, D_CONV), lambda b, i: (b, 0, 0)),
        ],
        out_shape=[
            jax.ShapeDtypeStruct((n_batch, t_pad, D_CONV), BF16),
            jax.ShapeDtypeStruct((n_batch, t_real, D_QK), F32),
            jax.ShapeDtypeStruct((n_batch, t_real, D_ATTN), F32),
            jax.ShapeDtypeStruct((n_batch, 2, N_HEADS, t_pad, LANES), BF16),
            jax.ShapeDtypeStruct((n_batch, N_HEADS, t_pad, LANES), BF16),
            jax.ShapeDtypeStruct((n_batch, N_HEADS, t_pad, LANES), BF16),
            jax.ShapeDtypeStruct((n_batch, 8, D_CONV), F32),
        ],
        scratch_shapes=[pltpu.VMEM((8, D_CONV), F32)],
        compiler_params=_cparams(("parallel", "arbitrary")),
        name="prep",
    )(z3, z3, z3, z3, z3, z3, conv_w, qg2, kg2)


def _lambda_value(lq1, lk1, lq2, lk2, lam_init):
    a = jnp.sum(lq1[...] * lk1[...], axis=-1, keepdims=True)
    b = jnp.sum(lq2[...] * lk2[...], axis=-1, keepdims=True)
    return jnp.exp(a) - jnp.exp(b) + lam_init


def _attn_kernel(qi_tab, ki_tab, q_ref, k_ref, v_ref, bias_ref, lq1, lk1, lq2, lk2, sg,
                 o_ref, m_ref, l_ref, acc_ref, *, tq, tk, lam_init):
    step = pl.program_id(1)
    qi = qi_tab[step]
    ki = ki_tab[step]

    @pl.when(ki == 0)
    def _():
        m_ref[...] = jnp.full_like(m_ref, NEG_INF)
        l_ref[...] = jnp.zeros_like(l_ref)
        acc_ref[...] = jnp.zeros_like(acc_ref)

    def head_step(h, near):
        q = q_ref[0, :, h].reshape(2 * tq, LANES)
        s = lax.dot_general(q, k_ref[0, h], (((1,), (1,)), ((), ())), preferred_element_type=F32)
        if near:
            s = (s.reshape(2, tq, tk) + bias_ref[0, h][None]).reshape(2 * tq, tk)
        m_prev = m_ref[h]
        m_new = jnp.maximum(m_prev, jnp.max(s, axis=-1, keepdims=True))
        alpha = jnp.exp(m_prev - m_new)
        p = jnp.exp(s - m_new[:, 0:1])
        l_ref[h] = alpha * l_ref[h] + jnp.sum(p, axis=-1, keepdims=True)
        acc_ref[h] = alpha * acc_ref[h] + jnp.dot(p.astype(BF16), v_ref[0, h], preferred_element_type=F32)
        m_ref[h] = m_new

    @pl.when(ki >= qi - 1)
    def _():
        lax.fori_loop(0, N_HEADS, lambda h, c: (head_step(h, True), c)[1], 0)

    @pl.when(ki < qi - 1)
    def _():
        lax.fori_loop(0, N_HEADS, lambda h, c: (head_step(h, False), c)[1], 0)

    @pl.when(ki == qi)
    def _():
        lam = _lambda_value(lq1, lk1, lq2, lk2, lam_init)

        def fin(h, c):
            acc = acc_ref[h]
            l = l_ref[h]
            o = acc[:tq] / l[:tq] - lam * (acc[tq:] / l[tq:])
            on = o * lax.rsqrt(jnp.mean(o * o, axis=-1, keepdims=True) + EPS) * sg[...]
            o_ref[0, h] = (on * (1.0 - lam_init)).astype(BF16)
            return c

        lax.fori_loop(0, N_HEADS, fin, 0)


def _prompt_attention(qs, kb, vb, bias_tiles, lams, sg, lam_init, t_pad, tq):
    n_batch = qs.shape[0]
    nq = t_pad // tq
    pairs = [(qi, ki) for qi in range(nq) for ki in range(qi + 1)]
    qi_tab = jnp.asarray(np.array([p[0] for p in pairs], np.int32))
    ki_tab = jnp.asarray(np.array([p[1] for p in pairs], np.int32))
    vec = lambda n: pl.BlockSpec((1, n), lambda b, s, qt, kt: (0, 0))
    grid_spec = pltpu.PrefetchScalarGridSpec(
        num_scalar_prefetch=2,
        grid=(n_batch, len(pairs)),
        in_specs=[
            pl.BlockSpec((1, 2, N_HEADS, tq, LANES), lambda b, s, qt, kt: (b, 0, 0, qt[s], 0)),
            pl.BlockSpec((1, N_HEADS, tq, LANES), lambda b, s, qt, kt: (b, 0, kt[s], 0)),
            pl.BlockSpec((1, N_HEADS, tq, LANES), lambda b, s, qt, kt: (b, 0, kt[s], 0)),
            pl.BlockSpec((1, N_HEADS, tq, tq),
                         lambda b, s, qt, kt: (jnp.minimum(qt[s] - kt[s], 1), 0, 0, 0)),
            vec(HEAD_DIM), vec(HEAD_DIM), vec(HEAD_DIM), vec(HEAD_DIM), vec(V_DIM),
        ],
        out_specs=pl.BlockSpec((1, N_HEADS, tq, LANES), lambda b, s, qt, kt: (b, 0, qt[s], 0)),
        scratch_shapes=[
            pltpu.VMEM((N_HEADS, 2 * tq, LANES), F32),
            pltpu.VMEM((N_HEADS, 2 * tq, LANES), F32),
            pltpu.VMEM((N_HEADS, 2 * tq, LANES), F32),
        ],
    )
    return pl.pallas_call(
        functools.partial(_attn_kernel, tq=tq, tk=tq, lam_init=lam_init),
        grid_spec=grid_spec,
        out_shape=jax.ShapeDtypeStruct((n_batch, N_HEADS, t_pad, LANES), BF16),
        compiler_params=_cparams(("parallel", "arbitrary")),
        name="prompt_attn",
    )(qi_tab, ki_tab, qs, kb, vb, bias_tiles, *lams, sg)


def _sample_conv_kernel(zb, zc, zx, s0, s1, cw, ya_ref, u_ref):
    u = zc[...] * zx[...]
    y = cw[0:1] * s0[...] + cw[1:2] * s1[...] + cw[2:3] * u
    ya_ref[...] = zb[...] * y
    u_ref[...] = u


def _sample_conv(z2d, s0, s1, conv_w, row_block, n):
    sec = lambda s: pl.BlockSpec((n, D_CONV), lambda i, s=s: (row_block, s))
    full = lambda r: pl.BlockSpec((r, D_CONV), lambda i: (0, 0))
    return pl.pallas_call(
        _sample_conv_kernel,
        grid=(1,),
        in_specs=[sec(0), sec(1), sec(2), full(n), full(n), full(CONV_K)],
        out_specs=[full(n), full(n)],
        out_shape=[jax.ShapeDtypeStruct((n, D_CONV), F32)] * 2,
        name="sample_conv",
    )(z2d, z2d, z2d, s0, s1, conv_w)


def _decode_kernel(pt_ref, zq, zk, zv, *rest, pages_per_step, n_steps, lam_init):
    kpages = rest[:pages_per_step]
    vpages = rest[pages_per_step:2 * pages_per_step]
    (dbias, qg, kg, lq1, lk1, lq2, lk2, sg,
     o_ref, kout, vout, qrows, kself, vself, m_ref, l_ref, acc_ref) = rest[2 * pages_per_step:]
    r = pl.program_id(0)
    p = pl.program_id(1)
    rows = 2 * N_HEADS
    lo = lax.broadcasted_iota(I32, (1, LANES), 1) < HEAD_DIM

    @pl.when(p == 0)
    def _():
        qrow = zq[pl.ds(r, 1), :]
        krow = zk[pl.ds(r, 1), :]
        qn = jnp.concatenate(
            [_half_norm(qrow[:, h * LANES:(h + 1) * LANES], qg[...], lo) for h in range(N_HEADS)], axis=1)
        kn = jnp.concatenate(
            [_half_norm(krow[:, h * LANES:(h + 1) * LANES], kg[...], lo) for h in range(N_HEADS)], axis=1)
        rid = lax.broadcasted_iota(I32, (rows, D_QK), 0)
        col = lax.broadcasted_iota(I32, (rows, D_QK), 1)
        comp = jnp.where(rid >= N_HEADS, 1, 0)
        head = rid - comp * N_HEADS
        hit = (col >= (2 * head + comp) * HEAD_DIM) & (col < (2 * head + comp + 1) * HEAD_DIM)
        qrows[...] = jnp.where(hit, qn * (HEAD_DIM ** -0.5), 0.0).astype(BF16)
        kself[...] = kn
        vself[...] = zv[pl.ds(r, 1), :]
        kout[0] = kn
        vout[0] = zv[pl.ds(r, 1), :]
        m_ref[...] = jnp.full_like(m_ref, NEG_INF)
        l_ref[...] = jnp.zeros_like(l_ref)
        acc_ref[...] = jnp.zeros_like(acc_ref)

    def update(s, vals_bf16):
        m_prev = m_ref[...]
        m_new = jnp.maximum(m_prev, jnp.max(s, axis=-1, keepdims=True))
        alpha = jnp.exp(m_prev - m_new)
        pr = jnp.exp(s - m_new)
        l_ref[...] = alpha * l_ref[...] + jnp.sum(pr, axis=-1, keepdims=True)
        m_ref[...] = m_new
        return alpha, pr

    for i in range(pages_per_step):
        kp = kpages[i][0].astype(BF16)
        s = lax.dot_general(qrows[...], kp, (((1,), (1,)), ((), ())), preferred_element_type=F32)
        if i == pages_per_step - 1:
            s = s + jnp.where(p == n_steps - 1, dbias[:, 0:PAGE_SIZE], 0.0)
        alpha, pr = update(s, None)
        acc_ref[...] = alpha * acc_ref[...] + jnp.dot(
            pr.astype(BF16), vpages[i][0].astype(BF16), preferred_element_type=F32)

    @pl.when(p == n_steps - 1)
    def _():
        kb16 = kself[...].astype(BF16).astype(F32)
        s_self = jnp.sum(qrows[...].astype(F32) * kb16, axis=-1, keepdims=True) + dbias[:, PAGE_SIZE:PAGE_SIZE + 1]
        alpha, pr = update(s_self, None)
        acc = alpha * acc_ref[...] + pr.astype(BF16).astype(F32) * vself[...].astype(BF16).astype(F32)
        l = l_ref[...]
        lam = _lambda_value(lq1, lk1, lq2, lk2, lam_init)
        od = acc[:N_HEADS] / l[:N_HEADS] - lam * (acc[N_HEADS:] / l[N_HEADS:])
        rid = lax.broadcasted_iota(I32, (N_HEADS, D_ATTN), 0)
        col = lax.broadcasted_iota(I32, (N_HEADS, D_ATTN), 1)
        diag = (col >= rid * V_DIM) & (col < (rid + 1) * V_DIM)
        o = jnp.sum(jnp.where(diag, od, 0.0), axis=0, keepdims=True)
        parts = []
        for h in range(N_HEADS):
            oh = o[:, h * V_DIM:(h + 1) * V_DIM]
            parts.append(oh * lax.rsqrt(jnp.mean(oh * oh, axis=-1, keepdims=True) + EPS) * sg[...])
        o_ref[0] = jnp.concatenate(parts, axis=1) * (1.0 - lam_init)


def _sample_attention(z2d, row_block, n_dec, ck, cv, page_table, dbias, qg2, kg2, lams, sg, lam_init,
                      pages_per_step):
    n_pages = page_table.shape[1]
    n_steps = n_pages // pages_per_step
    pt = page_table.reshape(-1)
    rows = 2 * N_HEADS
    zsec = lambda s: pl.BlockSpec((n_dec, 1024), lambda r, p, pt, s=s: (row_block, s))
    page = lambda i: pl.BlockSpec(
        (1, PAGE_SIZE, 1024), lambda r, p, pt, i=i: (pt[r * n_pages + p * pages_per_step + i], 0, 0))
    vec = lambda shape: pl.BlockSpec(shape, lambda r, p, pt: (0,) * len(shape))
    out_row = pl.BlockSpec((1, 1, 1024), lambda r, p, pt: (r, 0, 0))
    grid_spec = pltpu.PrefetchScalarGridSpec(
        num_scalar_prefetch=1,
        grid=(n_dec, n_steps),
        in_specs=[zsec(3), zsec(4), zsec(5)]
        + [page(i) for i in range(pages_per_step)] + [page(i) for i in range(pages_per_step)]
        + [vec((rows, 2 * PAGE_SIZE)), vec((1, LANES)), vec((1, LANES)),
           vec((1, HEAD_DIM)), vec((1, HEAD_DIM)), vec((1, HEAD_DIM)), vec((1, HEAD_DIM)), vec((1, V_DIM))],
        out_specs=[out_row, out_row, out_row],
        scratch_shapes=[
            pltpu.VMEM((rows, D_QK), BF16),
            pltpu.VMEM((1, D_QK), F32),
            pltpu.VMEM((1, D_ATTN), F32),
            pltpu.VMEM((rows, 1), F32),
            pltpu.VMEM((rows, 1), F32),
            pltpu.VMEM((rows, D_ATTN), F32),
        ],
    )
    return pl.pallas_call(
        functools.partial(_decode_kernel, pages_per_step=pages_per_step, n_steps=n_steps, lam_init=lam_init),
        grid_spec=grid_spec,
        out_shape=[jax.ShapeDtypeStruct((n_dec, 1, 1024), F32)] * 3,
        compiler_params=_cparams(("parallel", "arbitrary")),
        name="sample_attn",
    )(pt, z2d, z2d, z2d, *([ck] * pages_per_step), *([cv] * pages_per_step), dbias, qg2, kg2, *lams, sg)


def _merge_kernel(ya, on, ga, gb, h, wa, wb, wo, n2g, wr_hi, wr_lo, br,
                  h1_ref, xp_ref, topi_ref, gate_ref, rank_ref, cnt_ref, carry,
                  *, tm, tiles_per_batch, t_real, n_dec):
    i = pl.program_id(0)

    @pl.when(i == 0)
    def _():
        carry[...] = jnp.zeros_like(carry)

    y_a = jnp.dot(ya[...], wa[...], preferred_element_type=F32)
    o_cat = jnp.concatenate([on[0, hd] for hd in range(N_HEADS)], axis=1)
    y_b = jnp.dot(o_cat, wb[...], preferred_element_type=F32)
    mix = jax.nn.sigmoid(ga[...]) * y_a + jax.nn.sigmoid(gb[...]) * y_b
    h1 = h[...] + jnp.dot(mix.astype(BF16), wo[...], preferred_element_type=F32)
    h1_ref[...] = h1

    xn = h1 * lax.rsqrt(jnp.mean(h1 * h1, axis=-1, keepdims=True) + EPS) * n2g[...]
    xb = xn.astype(BF16)
    xf = xb.astype(F32)
    bits = pltpu.bitcast(xf, U32)
    half = D_MODEL // 2
    xp_ref[...] = bits[:, :half] | (bits[:, half:] >> 16)

    x_lo = (xn - xf).astype(BF16)
    logits = (jnp.dot(xb, wr_hi[...], preferred_element_type=F32)
              + jnp.dot(xb, wr_lo[...], preferred_element_type=F32)
              + jnp.dot(x_lo, wr_hi[...], preferred_element_type=F32)) + br[...]

    lane = lax.broadcasted_iota(I32, (tm, LANES), 1)
    work = logits
    vals, idxs, sels = [], [], []
    for _ in range(TOP_K):
        mx = jnp.max(work, axis=-1, keepdims=True)
        idx = jnp.min(jnp.where(work == mx, lane, LANES), axis=-1, keepdims=True)
        sel = lane == idx
        vals.append(mx)
        idxs.append(idx)
        sels.append(sel)
        work = jnp.where(sel, -jnp.inf, work)
    exps = [jnp.exp(v - vals[0]) for v in vals]
    denom = exps[0] + exps[1] + exps[2] + exps[3]

    pos_in_batch = (i % tiles_per_batch) * tm + lax.broadcasted_iota(I32, (tm, 1), 0)
    limit = jnp.where(i // tiles_per_batch == 0, t_real + n_dec, t_real)
    valid = pos_in_batch < limit

    onehot = jnp.zeros((tm, LANES), F32)
    for sel in sels:
        onehot = onehot + jnp.where(sel, 1.0, 0.0)
    onehot = jnp.where(valid, onehot, 0.0)
    rr = lax.broadcasted_iota(I32, (tm, tm), 0)
    cc = lax.broadcasted_iota(I32, (tm, tm), 1)
    lower = jnp.where(rr > cc, 1.0, 0.0).astype(BF16)
    before = jnp.dot(lower, onehot.astype(BF16), preferred_element_type=F32) + carry[...]

    topi = jnp.zeros((tm, LANES), I32)
    gate = jnp.zeros((tm, LANES), F32)
    rank = jnp.zeros((tm, LANES), I32)
    for j in range(TOP_K):
        rj = jnp.sum(jnp.where(sels[j], before, 0.0), axis=-1, keepdims=True).astype(I32)
        topi = jnp.where(lane == j, idxs[j], topi)
        gate = jnp.where(lane == j, exps[j] / denom, gate)
        rank = jnp.where(lane == j, rj, rank)
    topi_ref[...] = topi
    gate_ref[...] = gate
    rank_ref[...] = rank
    carry[...] = carry[...] + jnp.sum(onehot, axis=0, keepdims=True)
    cnt_ref[...] = jnp.broadcast_to(carry[...], cnt_ref.shape)


def _merge(ya2d, on, z2d, h2d, wa, wb, wo, n2g, wr_hi, wr_lo, br, tm, t_pad, t_real, n_dec):
    rows = h2d.shape[0]
    tpb = t_pad // tm
    const = lambda shape: pl.BlockSpec(shape, lambda i: (0,) * len(shape), pipeline_mode=pl.Buffered(1))
    row = lambda w, dt=None: pl.BlockSpec((tm, w), lambda i: (i, 0))
    return pl.pallas_call(
        functools.partial(_merge_kernel, tm=tm, tiles_per_batch=tpb, t_real=t_real, n_dec=n_dec),
        grid=(rows // tm,),
        in_specs=[
            row(D_CONV),
            pl.BlockSpec((1, N_HEADS, tm, LANES), lambda i: (i // tpb, 0, i % tpb, 0)),
            pl.BlockSpec((tm, D_MODEL), lambda i: (i, 3)),
            pl.BlockSpec((tm, D_MODEL), lambda i: (i, 4)),
            row(D_MODEL),
            const((D_CONV, D_MODEL)), const((D_ATTN, D_MODEL)), const((D_MODEL, D_MODEL)),
            const((1, D_MODEL)), const((D_MODEL, LANES)), const((D_MODEL, LANES)), const((1, LANES)),
        ],
        out_specs=[row(D_MODEL), row(D_MODEL // 2), row(LANES), row(LANES), row(LANES),
                   pl.BlockSpec((8, LANES), lambda i: (0, 0))],
        out_shape=[
            jax.ShapeDtypeStruct((rows, D_MODEL), F32),
            jax.ShapeDtypeStruct((rows, D_MODEL // 2), U32),
            jax.ShapeDtypeStruct((rows, LANES), I32),
            jax.ShapeDtypeStruct((rows, LANES), F32),
            jax.ShapeDtypeStruct((rows, LANES), I32),
            jax.ShapeDtypeStruct((8, LANES), F32),
        ],
        scratch_shapes=[pltpu.VMEM((1, LANES), F32)],
        compiler_params=_cparams(("arbitrary",)),
        name="merge_route",
    )(ya2d, on, z2d, z2d, h2d, wa, wb, wo, n2g, wr_hi, wr_lo, br)


def _dispatch_kernel(pos_ref, zpos_ref, x_ref, xs_hbm, zbuf, sem, zsem, *, tm):
    i = pl.program_id(0)

    @pl.when(i == 0)
    def _():
        zbuf[...] = jnp.zeros_like(zbuf)

        def zstart(e, c):
            pltpu.make_async_copy(zbuf, xs_hbm.at[pl.ds(pl.multiple_of(zpos_ref[e], SUB_ROWS), SUB_ROWS)],
                                  zsem).start()
            return c

        def zwait(e, c):
            pltpu.make_async_copy(zbuf, xs_hbm.at[pl.ds(0, SUB_ROWS)], zsem).wait()
            return c

        lax.fori_loop(0, N_EXPERTS, zstart, 0)
        lax.fori_loop(0, N_EXPERTS, zwait, 0)

    base = i * tm

    def start(t, c):
        for j in range(TOP_K):
            dst = pos_ref[(base + t) * TOP_K + j]
            pltpu.make_async_copy(x_ref.at[pl.ds(t, 1)], xs_hbm.at[pl.ds(dst, 1)], sem).start()
        return c

    def wait(t, c):
        for j in range(TOP_K):
            pltpu.make_async_copy(x_ref.at[pl.ds(0, 1)], xs_hbm.at[pl.ds(0, 1)], sem).wait()
        return c

    lax.fori_loop(0, tm, start, 0)
    lax.fori_loop(0, tm, wait, 0)


def _dispatch(pos_flat, zpos, xp, n_slots, tm):
    rows, width = xp.shape
    grid_spec = pltpu.PrefetchScalarGridSpec(
        num_scalar_prefetch=2,
        grid=(rows // tm,),
        in_specs=[pl.BlockSpec((tm, width), lambda i, p, z: (i, 0))],
        out_specs=pl.BlockSpec(memory_space=pl.ANY),
        scratch_shapes=[pltpu.VMEM((SUB_ROWS, width), U32),
                        pltpu.SemaphoreType.DMA(()), pltpu.SemaphoreType.DMA(())],
    )
    return pl.pallas_call(
        functools.partial(_dispatch_kernel, tm=tm),
        grid_spec=grid_spec,
        out_shape=jax.ShapeDtypeStruct((n_slots, width), U32),
        compiler_params=_cparams(("arbitrary",)),
        name="moe_dispatch",
    )(pos_flat, zpos, xp)


def _moe_kernel(item_e, item_start, item_nsub, n_items,
                xs_hbm, w1g, w1u, w2, b1g, b1u, b2,
                ys_hbm, xraw, xa, xb, yacc, sem_in, sem_out, *, nf):
    it = pl.program_id(0)
    f = pl.program_id(1)
    half = D_MODEL // 2

    @pl.when(it < n_items[0])
    def _():
        start = pl.multiple_of(item_start[it], SUB_ROWS)
        nsub = item_nsub[it]

        def in_copy(s):
            r0 = pl.multiple_of(s * SUB_ROWS, SUB_ROWS)
            return pltpu.make_async_copy(xs_hbm.at[pl.ds(start + r0, SUB_ROWS)],
                                         xraw.at[pl.ds(r0, SUB_ROWS)], sem_in.at[s])

        def out_copy(s):
            r0 = pl.multiple_of(s * SUB_ROWS, SUB_ROWS)
            return pltpu.make_async_copy(yacc.at[pl.ds(r0, SUB_ROWS)],
                                         ys_hbm.at[pl.ds(start + r0, SUB_ROWS)], sem_out)

        @pl.when(f == 0)
        def _():
            lax.fori_loop(0, nsub, lambda s, c: (in_copy(s).start(), c)[1], 0)

            def unpack(s, c):
                in_copy(s).wait()
                r0 = pl.multiple_of(s * SUB_ROWS, SUB_ROWS)
                w = xraw[pl.ds(r0, SUB_ROWS), :]
                xa[pl.ds(r0, SUB_ROWS), :] = pltpu.bitcast(w & jnp.uint32(0xFFFF0000), F32).astype(BF16)
                xb[pl.ds(r0, SUB_ROWS), :] = pltpu.bitcast(w << 16, F32).astype(BF16)
                yacc[pl.ds(r0, SUB_ROWS), :] = jnp.broadcast_to(b2[0], (SUB_ROWS, D_MODEL))
                return c

            lax.fori_loop(0, nsub, unpack, 0)

        wg = w1g[0].astype(BF16)
        wu = w1u[0].astype(BF16)
        wd = w2[0].astype(BF16)

        def sub(s, c):
            r0 = pl.multiple_of(s * SUB_ROWS, SUB_ROWS)
            a = xa[pl.ds(r0, SUB_ROWS), :]
            b = xb[pl.ds(r0, SUB_ROWS), :]
            hg = (jnp.dot(a, wg[:half], preferred_element_type=F32)
                  + jnp.dot(b, wg[half:], preferred_element_type=F32) + b1g[0])
            hu = (jnp.dot(a, wu[:half], preferred_element_type=F32)
                  + jnp.dot(b, wu[half:], preferred_element_type=F32) + b1u[0])
            g = jnp.minimum(hg, SWIGLU_LIMIT)
            up = jnp.clip(hu, -SWIGLU_LIMIT, SWIGLU_LIMIT)
            act = g * jax.nn.sigmoid(SWIGLU_ALPHA * g) * (up + 1.0)
            yacc[pl.ds(r0, SUB_ROWS), :] += jnp.dot(act.astype(BF16), wd, preferred_element_type=F32)
            return c

        lax.fori_loop(0, nsub, sub, 0)

        @pl.when(f == nf - 1)
        def _():
            lax.fori_loop(0, nsub, lambda s, c: (out_copy(s).start(), c)[1], 0)
            lax.fori_loop(0, nsub, lambda s, c: (out_copy(s).wait(), c)[1], 0)


def _moe_experts(items, xs, w1, b1, w2, b2, n_slots, rmax, tf):
    item_e, item_start, item_nsub, n_items = items
    ni = item_e.shape[0]
    nf = D_FF // tf
    half = D_MODEL // 2

    def fidx(it, f, n):
        return jnp.where(it < n[0], f, nf - 1)

    grid_spec = pltpu.PrefetchScalarGridSpec(
        num_scalar_prefetch=4,
        grid=(ni, nf),
        in_specs=[
            pl.BlockSpec(memory_space=pl.ANY),
            pl.BlockSpec((1, D_MODEL, tf), lambda it, f, e, s, ns, n: (e[it], 0, fidx(it, f, n))),
            pl.BlockSpec((1, D_MODEL, tf), lambda it, f, e, s, ns, n: (e[it], 0, nf + fidx(it, f, n))),
            pl.BlockSpec((1, tf, D_MODEL), lambda it, f, e, s, ns, n: (e[it], fidx(it, f, n), 0)),
            pl.BlockSpec((1, 1, tf), lambda it, f, e, s, ns, n: (e[it], 0, fidx(it, f, n))),
            pl.BlockSpec((1, 1, tf), lambda it, f, e, s, ns, n: (e[it], 0, nf + fidx(it, f, n))),
            pl.BlockSpec((1, 1, D_MODEL), lambda it, f, e, s, ns, n: (e[it], 0, 0)),
        ],
        out_specs=pl.BlockSpec(memory_space=pl.ANY),
        scratch_shapes=[
            pltpu.VMEM((rmax, half), U32),
            pltpu.VMEM((rmax, half), BF16),
            pltpu.VMEM((rmax, half), BF16),
            pltpu.VMEM((rmax, D_MODEL), F32),
            pltpu.SemaphoreType.DMA((rmax // SUB_ROWS,)), pltpu.SemaphoreType.DMA(()),
        ],
    )
    return pl.pallas_call(
        functools.partial(_moe_kernel, nf=nf),
        grid_spec=grid_spec,
        out_shape=jax.ShapeDtypeStruct((n_slots, D_MODEL), F32),
        compiler_params=_cparams(("arbitrary", "arbitrary")),
        name="moe_experts",
    )(item_e, item_start, item_nsub, n_items, xs, w1, w1, w2,
      b1.reshape(N_EXPERTS, 1, 2 * D_FF), b1.reshape(N_EXPERTS, 1, 2 * D_FF), b2.reshape(N_EXPERTS, 1, D_MODEL))


def _combine_kernel(pos_ref, ys_hbm, h1_hbm, gate_hbm, y_ref, ysamp_ref, ybuf, hbuf, gbuf, sem, hsem,
                    *, tm, tiles_per_batch, t_pad, n_dec, samp_row0):
    i = pl.program_id(0)
    n_tiles = pl.num_programs(0)

    def gather_and_mix(row0, n):
        hcp = pltpu.make_async_copy(h1_hbm.at[pl.ds(row0, n)], hbuf.at[pl.ds(0, n)], hsem)
        gcp = pltpu.make_async_copy(gate_hbm.at[pl.ds(row0, n)], gbuf.at[pl.ds(0, n)], hsem)
        hcp.start()
        gcp.start()

        def start(t, c):
            for j in range(TOP_K):
                src = pos_ref[(row0 + t) * TOP_K + j]
                pltpu.make_async_copy(ys_hbm.at[pl.ds(src, 1)], ybuf.at[j, pl.ds(t, 1)], sem).start()
            return c

        def wait(t, c):
            for j in range(TOP_K):
                pltpu.make_async_copy(ys_hbm.at[pl.ds(0, 1)], ybuf.at[0, pl.ds(0, 1)], sem).wait()
            return c

        lax.fori_loop(0, n, start, 0)
        hcp.wait()
        gcp.wait()
        lax.fori_loop(0, n, wait, 0)
        g = gbuf[0:n, :]
        out = hbuf[0:n, :]
        for j in range(TOP_K):
            out = out + g[:, j:j + 1] * ybuf[j, 0:n, :]
        return out

    row0 = (i // tiles_per_batch) * t_pad + N_META + (i % tiles_per_batch) * tm
    y_ref[0] = gather_and_mix(pl.multiple_of(row0, 8), tm)

    @pl.when(i == n_tiles - 1)
    def _():
        ysamp_ref[...] = gather_and_mix(samp_row0, n_dec)


def _combine(pos_flat, ys, h1, gate, n_batch, seq, t_pad, n_dec, samp_row0, tm):
    tpb = seq // tm
    grid_spec = pltpu.PrefetchScalarGridSpec(
        num_scalar_prefetch=1,
        grid=(n_batch * tpb,),
        in_specs=[pl.BlockSpec(memory_space=pl.ANY)] * 3,
        out_specs=[pl.BlockSpec((1, tm, D_MODEL), lambda i, p: (i // tpb, i % tpb, 0)),
                   pl.BlockSpec((n_dec, D_MODEL), lambda i, p: (0, 0))],
        scratch_shapes=[
            pltpu.VMEM((TOP_K, tm, D_MODEL), F32),
            pltpu.VMEM((tm, D_MODEL), F32),
            pltpu.VMEM((tm, LANES), F32),
            pltpu.SemaphoreType.DMA(()), pltpu.SemaphoreType.DMA(()),
        ],
    )
    return pl.pallas_call(
        functools.partial(_combine_kernel, tm=tm, tiles_per_batch=tpb, t_pad=t_pad, n_dec=n_dec,
                          samp_row0=samp_row0),
        grid_spec=grid_spec,
        out_shape=[jax.ShapeDtypeStruct((n_batch, seq, D_MODEL), F32),
                   jax.ShapeDtypeStruct((n_dec, D_MODEL), F32)],
        compiler_params=_cparams(("arbitrary",)),
        name="moe_combine",
    )(pos_flat, ys, h1, gate)


def _t5_bucket(rel):
    n = jnp.maximum(rel, 0)
    max_exact = N_BUCKETS // 2
    nf = jnp.maximum(n, 1).astype(F32)
    large = max_exact + (jnp.log(nf / max_exact) / math.log(MAX_DISTANCE / max_exact)
                         * (N_BUCKETS - max_exact)).astype(I32)
    large = jnp.minimum(large, N_BUCKETS - 1)
    return jnp.where(n < max_exact, n, large)


def _bias_tables(rel_bias, tq, past_len):
    span = 2 * MAX_DISTANCE
    tab = rel_bias[_t5_bucket(jnp.arange(span))] - rel_bias[N_BUCKETS - 1][None]
    ii = jnp.arange(tq)[:, None]
    jj = jnp.arange(tq)[None, :]
    rel0 = ii - jj
    diag = jnp.where((rel0 >= 0)[None], jnp.moveaxis(tab[jnp.clip(rel0, 0, span - 1)], -1, 0), NEG_INF)
    sub = jnp.moveaxis(tab[jnp.minimum(tq + rel0, span - 1)], -1, 0)
    tiles = jnp.stack([diag, sub]).astype(F32)
    rel_last = past_len - (past_len - PAGE_SIZE + jnp.arange(PAGE_SIZE))
    last = tab[jnp.minimum(rel_last, span - 1)].T
    own = jnp.broadcast_to(tab[0][:, None], (N_HEADS, PAGE_SIZE))
    dbias = jnp.concatenate([last, own], axis=1)
    dbias = jnp.concatenate([dbias, dbias], axis=0).astype(F32)
    return tiles, dbias


def _routing_tables(counts, topi, rank, valid, n_trash_rows, rmax, n_items_max, real_slots):
    padded = (counts + SUB_ROWS - 1) // SUB_ROWS * SUB_ROWS
    ends = jnp.cumsum(padded)
    off = ends - padded
    pos = off[topi] + rank
    trash_row = jnp.cumsum(jnp.logical_not(valid).astype(I32)) - 1
    trash = real_slots + trash_row[:, None] * TOP_K + jnp.arange(TOP_K, dtype=I32)[None]
    pos_scatter = jnp.where(valid[:, None], pos, trash).reshape(-1)
    pos_gather = jnp.where(valid[:, None], pos, 0).reshape(-1)
    zero_trash = real_slots + -(-(n_trash_rows * TOP_K) // SUB_ROWS) * SUB_ROWS
    zpos = jnp.where(counts > 0, ends - SUB_ROWS, zero_trash).astype(I32)
    per_e = (padded + rmax - 1) // rmax
    item_end = jnp.cumsum(per_e)
    n_items = item_end[-1]
    t = jnp.arange(n_items_max, dtype=I32)
    tt = jnp.minimum(t, n_items - 1)
    e_of = jnp.minimum(jnp.searchsorted(item_end, tt, side="right"), N_EXPERTS - 1).astype(I32)
    k = tt - (item_end - per_e)[e_of]
    start = off[e_of] + k * rmax
    nrows = jnp.minimum(rmax, padded[e_of] - k * rmax)
    nsub = jnp.where(t < n_items, nrows // SUB_ROWS, 0).astype(I32)
    items = (e_of, start.astype(I32), nsub, n_items.reshape(1).astype(I32))
    return pos_scatter.astype(I32), pos_gather.astype(I32), zpos, items, zero_trash + SUB_ROWS


def kernel(x_prompt, x_sample, cache_k, cache_v, state_conv, page_table, meta_tokens, rel_bias, norm1_g, w_in,
           conv_w, q_norm_g, k_norm_g, lambda_q1, lambda_k1, lambda_q2, lambda_k2, subln_g, w_branch_a,
           w_branch_b, w_out, norm2_g, w_router, b_router, w_mlp1, b_mlp1, w_mlp2, b_mlp2):
    n_batch, seq, _ = x_prompt.shape
    n_dec, t_dec, _ = x_sample.shape
    depth, n_pool = cache_k.shape[0], cache_k.shape[1]
    n_pages = page_table.shape[1]
    past_len = n_pages * PAGE_SIZE
    assert depth == 1 and t_dec == 1 and n_dec == 8
    t_real = seq + N_META
    tq = 384
    t_pad = -(-(t_real + n_dec) // tq) * tq
    rows = n_batch * t_pad
    samp_row0 = t_real
    assert samp_row0 % 8 == 0 and t_pad % 1056 == 0
    lam_init = 0.8 - 0.6 * math.exp(-0.3 * 0)

    meta = jnp.broadcast_to(meta_tokens[None], (n_batch, N_META, D_MODEL))
    hp = jnp.concatenate([meta, x_prompt, jnp.zeros((n_batch, t_pad - t_real, D_MODEL), F32)], axis=1)
    hp = hp.at[0, samp_row0:samp_row0 + n_dec].set(x_sample[:, 0])
    h2d = hp.reshape(rows, D_MODEL)

    z2d = _in_proj(h2d, norm1_g[0], w_in[0], tm=1056, tn=512)
    z3 = z2d.reshape(n_batch, t_pad, D_IN)

    qg2 = jnp.tile(q_norm_g[0], 2).reshape(1, LANES)
    kg2 = jnp.tile(k_norm_g[0], 2).reshape(1, LANES)
    sg = subln_g[0].reshape(1, V_DIM)
    lams = [v[0].reshape(1, HEAD_DIM) for v in (lambda_q1, lambda_k1, lambda_q2, lambda_k2)]
    bias_tiles, dbias = _bias_tables(rel_bias, tq, past_len)

    ya, k_p, v_p, qs, kb, vb, ust = _prep(z3, conv_w[0], qg2, kg2, n_batch, t_pad, t_real, tm=tq)
    on = _prompt_attention(qs, kb, vb, bias_tiles, lams, sg, lam_init, t_pad, tq)

    row_block = samp_row0 // n_dec
    ya_s, u_s = _sample_conv(z2d, state_conv[0, :, 0], state_conv[0, :, 1], conv_w[0], row_block, n_dec)
    ck = cache_k[0].reshape(n_pool, PAGE_SIZE, N_HEADS * 2 * HEAD_DIM)
    cv = cache_v[0].reshape(n_pool, PAGE_SIZE, N_HEADS * V_DIM)
    o_s, k_s, v_s = _sample_attention(z2d, row_block, n_dec, ck, cv, page_table, dbias, qg2, kg2, lams, sg,
                                      lam_init, pages_per_step=4)

    ya = ya.at[0, samp_row0:samp_row0 + n_dec].set(ya_s.astype(BF16))
    on = on.at[0, :, samp_row0:samp_row0 + n_dec].set(
        o_s.reshape(n_dec, N_HEADS, V_DIM).transpose(1, 0, 2).astype(BF16))

    wr = jnp.pad(w_router[0], ((0, 0), (0, LANES - N_EXPERTS)))
    wr_hi = wr.astype(BF16)
    wr_lo = (wr - wr_hi.astype(F32)).astype(BF16)
    br = jnp.concatenate([b_router[0], jnp.full((LANES - N_EXPERTS,), NEG_INF, F32)]).reshape(1, LANES)
    h1, xp, topi, gate, rank, cnt = _merge(
        ya.reshape(rows, D_CONV), on, z2d, h2d,
        w_branch_a[0].astype(BF16), w_branch_b[0].astype(BF16), w_out[0].astype(BF16),
        norm2_g[0].reshape(1, D_MODEL), wr_hi, wr_lo, br, tm=tq // 2, t_pad=t_pad, t_real=t_real, n_dec=n_dec)

    rmax = 1536
    n_assign = (n_batch * t_real + n_dec) * TOP_K
    real_slots = -(-(n_assign + N_EXPERTS * (SUB_ROWS - 1)) // SUB_ROWS) * SUB_ROWS
    n_items_max = N_EXPERTS + real_slots // rmax
    pos_in_batch = jnp.arange(rows, dtype=I32) % t_pad
    valid = (pos_in_batch < t_real) | ((jnp.arange(rows) < t_pad) & (pos_in_batch < t_real + n_dec))
    pos_scatter, pos_gather, zpos, items, n_slots = _routing_tables(
        cnt[0, :N_EXPERTS].astype(I32), topi[:, :TOP_K], rank[:, :TOP_K], valid,
        rows - n_assign // TOP_K, rmax, n_items_max, real_slots)

    xs = _dispatch(pos_scatter, zpos, xp, n_slots, tm=tq)
    ys = _moe_experts(items, xs, w_mlp1[0], b_mlp1[0], w_mlp2[0], b_mlp2[0], n_slots, rmax, tf=256)
    y_prompt, y_samp = _combine(pos_gather, ys, h1, gate, n_batch, seq, t_pad, n_dec, samp_row0, tm=512)

    return (y_prompt,
            y_samp.reshape(n_dec, 1, D_MODEL),
            k_p.reshape(1, n_batch, t_real, N_HEADS, 2 * HEAD_DIM),
            v_p.reshape(1, n_batch, t_real, N_HEADS, V_DIM),
            ust[:, 6:8].reshape(1, n_batch, CONV_K - 1, D_CONV),
            k_s.reshape(1, n_dec, 1, N_HEADS, 2 * HEAD_DIM),
            v_s.reshape(1, n_dec, 1, N_HEADS, V_DIM),
            jnp.stack([state_conv[0, :, 1], u_s], axis=1).reshape(1, n_dec, CONV_K - 1, D_CONV))
```

```python
import functools
import math

import jax
import jax.numpy as jnp
import numpy as np
from jax import lax
from jax.experimental import pallas as pl
from jax.experimental.pallas import tpu as pltpu

D_MODEL = 2048
N_META = 16
D_CONV = 1024
CONV_K = 3
N_HEADS = 8
HEAD_DIM = 64
V_DIM = 128
D_QK = 1024
D_ATTN = 1024
D_IN = 10240
N_BUCKETS = 32
MAX_DISTANCE = 128
N_EXPERTS = 32
TOP_K = 4
D_FF = 2048
SWIGLU_LIMIT = 7.0
SWIGLU_ALPHA = 1.702
EPS = 1e-6
NEG_INF = -1e30
PAGE_SIZE = 128

F32 = jnp.float32
BF16 = jnp.bfloat16
I32 = jnp.int32
U32 = jnp.uint32

LANES = 128
SUB_ROWS = 128
BIG_ROWS = 512
ATTN_ROW_CHUNK = 64
LOG2E = 1.4426950408889634
DECODE_KEY_CHUNK = 16
D_QK_OFF = 3 * D_CONV
VMEM_LIMIT = 56 * 1024 * 1024


def _cparams(sem, vmem=VMEM_LIMIT):
    return pltpu.CompilerParams(dimension_semantics=sem, vmem_limit_bytes=vmem)


def _in_proj_kernel(x_ref, g_ref, w_ref, z_ref, xn_ref):
    @pl.when(pl.program_id(1) == 0)
    def _():
        x = x_ref[...]
        ms = jnp.mean(x * x, axis=-1, keepdims=True)
        xn_ref[...] = (x * lax.rsqrt(ms + EPS) * g_ref[...]).astype(BF16)

    z_ref[...] = jnp.dot(xn_ref[...], w_ref[...].astype(BF16), preferred_element_type=F32)


def _in_proj(h2d, g, w, tm, tn):
    rows = h2d.shape[0]
    return pl.pallas_call(
        _in_proj_kernel,
        grid=(rows // tm, D_IN // tn),
        in_specs=[
            pl.BlockSpec((tm, D_MODEL), lambda i, j: (i, 0)),
            pl.BlockSpec((1, D_MODEL), lambda i, j: (0, 0)),
            pl.BlockSpec((D_MODEL, tn), lambda i, j: (0, j)),
        ],
        out_specs=pl.BlockSpec((tm, tn), lambda i, j: (i, j)),
        out_shape=jax.ShapeDtypeStruct((rows, D_IN), F32),
        scratch_shapes=[pltpu.VMEM((tm, D_MODEL), BF16)],
        compiler_params=_cparams(("parallel", "arbitrary")),
        name="in_proj",
    )(h2d, g.reshape(1, D_MODEL), w)


def _half_norm(x, g2, lo):
    t = x * x
    s_lo = jnp.sum(jnp.where(lo, t, 0.0), axis=-1, keepdims=True)
    s_hi = jnp.sum(jnp.where(lo, 0.0, t), axis=-1, keepdims=True)
    inv = jnp.where(lo, lax.rsqrt(s_lo * (1.0 / HEAD_DIM) + EPS), lax.rsqrt(s_hi * (1.0 / HEAD_DIM) + EPS))
    return x * inv * g2


def _prep_kernel(zb, zc, zx, zq, zk, zv, cw, qg, kg,
                 ya_ref, kout, vout, qs, kb, vb, ust, carry, *, tm, state_tile, state_row):
    i = pl.program_id(1)

    @pl.when(i == 0)
    def _():
        carry[...] = jnp.zeros_like(carry)

    u = zc[0] * zx[0]
    prev = carry[...]
    row = lax.broadcasted_iota(I32, (tm, 1), 0)
    u1 = jnp.where(row == 0, prev[7:8], pltpu.roll(u, 1, 0))
    u2 = jnp.where(row == 0, prev[6:7], jnp.where(row == 1, prev[7:8], pltpu.roll(u, 2, 0)))
    y = cw[0:1] * u2 + cw[1:2] * u1 + cw[2:3] * u
    ya_ref[0] = (zb[0] * y).astype(BF16)
    carry[...] = u[tm - 8:tm]

    @pl.when(i == state_tile)
    def _():
        ust[0] = u[state_row:state_row + 8]

    lo = lax.broadcasted_iota(I32, (1, LANES), 1) < HEAD_DIM
    for h in range(N_HEADS):
        sl = slice(h * LANES, (h + 1) * LANES)
        qn = _half_norm(zq[0, :, sl], qg[...], lo) * (LOG2E * HEAD_DIM ** -0.5)
        qs[0, 0, h] = jnp.where(lo, qn, 0.0).astype(BF16)
        qs[0, 1, h] = jnp.where(lo, 0.0, qn).astype(BF16)
        kn = _half_norm(zk[0, :, sl], kg[...], lo)
        kout[0, :, sl] = kn
        kb[0, h] = kn.astype(BF16)
        vb[0, h] = zv[0, :, sl].astype(BF16)
    vout[0] = zv[0]


def _prep(z3, conv_w, qg2, kg2, n_batch, t_pad, t_real, tm):
    nt = t_pad // tm
    state_tile = (t_real - 2) // tm
    state_row = ((t_real - 2) % tm) // 8 * 8
    sec = lambda s: pl.BlockSpec((1, tm, 1024), lambda b, i, s=s: (b, i, s))
    small = lambda shape: pl.BlockSpec(shape, lambda b, i: (0,) * len(shape))
    return pl.pallas_call(
        functools.partial(_prep_kernel, tm=tm, state_tile=state_tile, state_row=state_row),
        grid=(n_batch, nt),
        in_specs=[sec(0), sec(1), sec(2), sec(3), sec(4), sec(5),
                  small((CONV_K, D_CONV)), small((1, LANES)), small((1, LANES))],
        out_specs=[
            pl.BlockSpec((1, tm, D_CONV), lambda b, i: (b, i, 0)),
            pl.BlockSpec((1, tm, D_QK), lambda b, i: (b, i, 0)),
            pl.BlockSpec((1, tm, D_ATTN), lambda b, i: (b, i, 0)),
            pl.BlockSpec((1, 2, N_HEADS, tm, LANES), lambda b, i: (b, 0, 0, i, 0)),
            pl.BlockSpec((1, N_HEADS, tm, LANES), lambda b, i: (b, 0, i, 0)),
            pl.BlockSpec((1, N_HEADS, tm, LANES), lambda b, i: (b, 0, i, 0)),
            pl.BlockSpec((1, 8, D_CONV), lambda b, i: (b, 0, 0)),
        ],
        out_shape=[
            jax.ShapeDtypeStruct((n_batch, t_pad, D_CONV), BF16),
            jax.ShapeDtypeStruct((n_batch, t_real, D_QK), F32),
            jax.ShapeDtypeStruct((n_batch, t_real, D_ATTN), F32),
            jax.ShapeDtypeStruct((n_batch, 2, N_HEADS, t_pad, LANES), BF16),
            jax.ShapeDtypeStruct((n_batch, N_HEADS, t_pad, LANES), BF16),
            jax.ShapeDtypeStruct((n_batch, N_HEADS, t_pad, LANES), BF16),
            jax.ShapeDtypeStruct((n_batch, 8, D_CONV), F32),
        ],
        scratch_shapes=[pltpu.VMEM((8, D_CONV), F32)],
        compiler_params=_cparams(("parallel", "arbitrary")),
        name="prep",
    )(z3, z3, z3, z3, z3, z3, conv_w, qg2, kg2)


def _lambda_value(lq1, lk1, lq2, lk2, lam_init):
    a = jnp.sum(lq1[...] * lk1[...], axis=-1, keepdims=True)
    b = jnp.sum(lq2[...] * lk2[...], axis=-1, keepdims=True)
    return jnp.exp(a) - jnp.exp(b) + lam_init


def _attn_kernel(qi_tab, ki_tab, q_ref, k_ref, v_ref, bias_ref, lq1, lk1, lq2, lk2, sg,
                 o_ref, m_ref, l_ref, acc_ref, s_ref, p_ref, a_ref, *, tq, tk, lam_init):
    step = pl.program_id(1)
    qi = qi_tab[step]
    ki = ki_tab[step]
    rc = ATTN_ROW_CHUNK

    @pl.when(ki == 0)
    def _():
        m_ref[...] = jnp.full_like(m_ref, NEG_INF)
        l_ref[...] = jnp.zeros_like(l_ref)
        acc_ref[...] = jnp.zeros_like(acc_ref)

    def head_step(h, near):
        q = q_ref[0, :, h].reshape(2 * tq, LANES)
        s_ref[...] = lax.dot_general(q, k_ref[0, h], (((1,), (1,)), ((), ())), preferred_element_type=F32)

        def chunk(c, carry):
            r0 = pl.multiple_of(c * rc, rc)
            rows = pl.ds(r0, rc)
            s = s_ref[rows, :]
            if near:
                b0 = pl.multiple_of((c % (tq // rc)) * rc, rc)
                s = s + bias_ref[0, h, pl.ds(b0, rc), :]
            m_prev = m_ref[h, rows, :]
            m_new = jnp.maximum(m_prev, jnp.max(s, axis=-1, keepdims=True))
            alpha = jnp.exp2(m_prev - m_new)
            p = jnp.exp2(s - jnp.concatenate([m_new] * (tk // LANES), axis=1))
            l_ref[h, rows, :] = alpha * l_ref[h, rows, :] + jnp.sum(p, axis=-1, keepdims=True)
            m_ref[h, rows, :] = m_new
            a_ref[rows, :] = alpha
            p_ref[rows, :] = p.astype(BF16)
            return carry

        lax.fori_loop(0, 2 * tq // rc, chunk, 0, unroll=True)
        acc_ref[h] = a_ref[...] * acc_ref[h] + jnp.dot(p_ref[...], v_ref[0, h], preferred_element_type=F32)

    @pl.when(ki >= qi - 1)
    def _():
        lax.fori_loop(0, N_HEADS, lambda h, c: (head_step(h, True), c)[1], 0)

    @pl.when(ki < qi - 1)
    def _():
        lax.fori_loop(0, N_HEADS, lambda h, c: (head_step(h, False), c)[1], 0)

    @pl.when(ki == qi)
    def _():
        lam = _lambda_value(lq1, lk1, lq2, lk2, lam_init)

        def fin(h, c):
            acc = acc_ref[h]
            l = l_ref[h]
            o = acc[:tq] / l[:tq] - lam * (acc[tq:] / l[tq:])
            on = o * lax.rsqrt(jnp.mean(o * o, axis=-1, keepdims=True) + EPS) * sg[...]
            o_ref[0, h] = (on * (1.0 - lam_init)).astype(BF16)
            return c

        lax.fori_loop(0, N_HEADS, fin, 0)


def _prompt_attention(qs, kb, vb, bias_tiles, lams, sg, lam_init, t_pad, tq):
    n_batch = qs.shape[0]
    nq = t_pad // tq
    pairs = [(qi, ki) for qi in range(nq) for ki in range(qi + 1)]
    qi_tab = jnp.asarray(np.array([p[0] for p in pairs], np.int32))
    ki_tab = jnp.asarray(np.array([p[1] for p in pairs], np.int32))
    vec = lambda n: pl.BlockSpec((1, n), lambda b, s, qt, kt: (0, 0))
    grid_spec = pltpu.PrefetchScalarGridSpec(
        num_scalar_prefetch=2,
        grid=(n_batch, len(pairs)),
        in_specs=[
            pl.BlockSpec((1, 2, N_HEADS, tq, LANES), lambda b, s, qt, kt: (b, 0, 0, qt[s], 0)),
            pl.BlockSpec((1, N_HEADS, tq, LANES), lambda b, s, qt, kt: (b, 0, kt[s], 0)),
            pl.BlockSpec((1, N_HEADS, tq, LANES), lambda b, s, qt, kt: (b, 0, kt[s], 0)),
            pl.BlockSpec((1, N_HEADS, tq, tq),
                         lambda b, s, qt, kt: (jnp.minimum(qt[s] - kt[s], 1), 0, 0, 0)),
            vec(HEAD_DIM), vec(HEAD_DIM), vec(HEAD_DIM), vec(HEAD_DIM), vec(V_DIM),
        ],
        out_specs=pl.BlockSpec((1, N_HEADS, tq, LANES), lambda b, s, qt, kt: (b, 0, qt[s], 0)),
        scratch_shapes=[
            pltpu.VMEM((N_HEADS, 2 * tq, LANES), F32),
            pltpu.VMEM((N_HEADS, 2 * tq, LANES), F32),
            pltpu.VMEM((N_HEADS, 2 * tq, LANES), F32),
            pltpu.VMEM((2 * tq, tq), F32),
            pltpu.VMEM((2 * tq, tq), BF16),
            pltpu.VMEM((2 * tq, LANES), F32),
        ],
    )
    return pl.pallas_call(
        functools.partial(_attn_kernel, tq=tq, tk=tq, lam_init=lam_init),
        grid_spec=grid_spec,
        out_shape=jax.ShapeDtypeStruct((n_batch, N_HEADS, t_pad, LANES), BF16),
        compiler_params=_cparams(("parallel", "arbitrary")),
        name="prompt_attn",
    )(qi_tab, ki_tab, qs, kb, vb, bias_tiles, *lams, sg)


def _sample_conv_kernel(zb, zc, zx, s0, s1, cw, ya_ref, u_ref):
    u = zc[...] * zx[...]
    y = cw[0:1] * s0[...] + cw[1:2] * s1[...] + cw[2:3] * u
    ya_ref[...] = zb[...] * y
    u_ref[...] = u


def _sample_conv(z2d, s0, s1, conv_w, row_block, n):
    sec = lambda s: pl.BlockSpec((n, D_CONV), lambda i, s=s: (row_block, s))
    full = lambda r: pl.BlockSpec((r, D_CONV), lambda i: (0, 0))
    return pl.pallas_call(
        _sample_conv_kernel,
        grid=(1,),
        in_specs=[sec(0), sec(1), sec(2), full(n), full(n), full(CONV_K)],
        out_specs=[full(n), full(n)],
        out_shape=[jax.ShapeDtypeStruct((n, D_CONV), F32)] * 2,
        name="sample_conv",
    )(z2d, z2d, z2d, s0, s1, conv_w)


def _decode_kernel(pt_ref, zs, *rest, pages_per_step, n_steps, lam_init):
    kpages = rest[:pages_per_step]
    vpages = rest[pages_per_step:2 * pages_per_step]
    (pbias, sbias, rsum, qg, kg, lq1, lk1, lq2, lk2, sg,
     o_ref, kout, vout, q_sc, s_ref, m_ref, l_ref, acc_ref) = rest[2 * pages_per_step:]
    p = pl.program_id(1)
    kc = DECODE_KEY_CHUNK
    q_lo, k_lo, v_lo = D_QK_OFF // LANES, (D_QK_OFF + D_QK) // LANES, (D_QK_OFF + 2 * D_QK) // LANES
    lo = lax.broadcasted_iota(I32, (1, LANES), 1) < HEAD_DIM

    @pl.when(p == 0)
    def _():
        q_sc[...] = _half_norm(zs[0, q_lo:q_lo + N_HEADS], qg[...], lo) * (LOG2E * HEAD_DIM ** -0.5)
        kout[0] = _half_norm(zs[0, k_lo:k_lo + N_HEADS], kg[...], lo)
        vout[0] = zs[0, v_lo:v_lo + N_HEADS]
        m_ref[...] = jnp.full_like(m_ref, NEG_INF)
        l_ref[...] = jnp.zeros_like(l_ref)
        acc_ref[...] = jnp.zeros_like(acc_ref)

    q = q_sc[...]

    def scores(k):
        n = k.shape[0]
        t = (k * q[None]).reshape(n * N_HEADS, LANES).astype(BF16)
        return jnp.dot(t, rsum[...], preferred_element_type=F32).reshape(n, N_HEADS, 2 * LANES)

    m_prev = m_ref[...]
    m_new = m_prev
    for i in range(pages_per_step):
        s = scores(kpages[i][...])
        if i == pages_per_step - 1:
            s = s + jnp.where(p == n_steps - 1, pbias[...], 0.0)
        s_ref[pl.ds(i * PAGE_SIZE, PAGE_SIZE)] = s
        m_new = jnp.maximum(m_new, jnp.max(s, axis=0))
    alpha = jnp.exp2(m_prev - m_new)

    def accumulate(i):
        def body(c, carry):
            l, a0, a1 = carry
            pc = jnp.exp2(s_ref[pl.ds(i * PAGE_SIZE + c * kc, kc)] - m_new[None])
            vc = vpages[i][pl.ds(c * kc, kc)]
            return (l + jnp.sum(pc, axis=0),
                    a0 + jnp.sum(pc[:, :, :LANES] * vc, axis=0),
                    a1 + jnp.sum(pc[:, :, LANES:] * vc, axis=0))
        return body

    carry = (alpha * l_ref[...], alpha[:, :LANES] * acc_ref[0], alpha[:, LANES:] * acc_ref[1])
    for i in range(pages_per_step):
        carry = lax.fori_loop(0, PAGE_SIZE // kc, accumulate(i), carry)
    l, a0, a1 = carry
    m_ref[...] = m_new
    l_ref[...] = l
    acc_ref[0] = a0
    acc_ref[1] = a1

    @pl.when(p == n_steps - 1)
    def _():
        k_own = _half_norm(zs[0, k_lo:k_lo + N_HEADS], kg[...], lo)
        v_own = zs[0, v_lo:v_lo + N_HEADS]
        s_own = scores(k_own[None])[0] + sbias[...]
        m_fin = jnp.maximum(m_new, s_own)
        a_fin = jnp.exp2(m_new - m_fin)
        p_own = jnp.exp2(s_own - m_fin)
        l_fin = a_fin * l + p_own
        o0 = (a_fin[:, :LANES] * a0 + p_own[:, :LANES] * v_own) / l_fin[:, :LANES]
        o1 = (a_fin[:, LANES:] * a1 + p_own[:, LANES:] * v_own) / l_fin[:, LANES:]
        o = o0 - _lambda_value(lq1, lk1, lq2, lk2, lam_init) * o1
        on = o * lax.rsqrt(jnp.mean(o * o, axis=-1, keepdims=True) + EPS) * sg[...]
        o_ref[0] = on * (1.0 - lam_init)


def _sample_attention(zs, ck, cv, page_table, pbias, sbias, qg2, kg2, lams, sg, lam_init, pages_per_step):
    n_dec = zs.shape[0]
    n_pages = page_table.shape[1]
    n_steps = n_pages // pages_per_step
    pt = page_table.reshape(-1)
    page = lambda i: pl.BlockSpec(
        (None, None, PAGE_SIZE, N_HEADS, LANES),
        lambda r, p, pt, i=i: (0, pt[r * n_pages + p * pages_per_step + i], 0, 0, 0))
    vec = lambda shape: pl.BlockSpec(shape, lambda r, p, pt: (0,) * len(shape))
    out_row = pl.BlockSpec((1, N_HEADS, LANES), lambda r, p, pt: (r, 0, 0))
    rsum = (jnp.arange(LANES)[:, None] // HEAD_DIM == jnp.arange(2 * LANES)[None, :] // LANES).astype(BF16)
    grid_spec = pltpu.PrefetchScalarGridSpec(
        num_scalar_prefetch=1,
        grid=(n_dec, n_steps),
        in_specs=[pl.BlockSpec((1,) + zs.shape[1:], lambda r, p, pt: (r, 0, 0))]
        + [page(i) for i in range(pages_per_step)] + [page(i) for i in range(pages_per_step)]
        + [vec((PAGE_SIZE, N_HEADS, 2 * LANES)), vec((N_HEADS, 2 * LANES)), vec((LANES, 2 * LANES)),
           vec((1, LANES)), vec((1, LANES)),
           vec((1, HEAD_DIM)), vec((1, HEAD_DIM)), vec((1, HEAD_DIM)), vec((1, HEAD_DIM)), vec((1, V_DIM))],
        out_specs=[out_row, out_row, out_row],
        scratch_shapes=[
            pltpu.VMEM((N_HEADS, LANES), F32),
            pltpu.VMEM((pages_per_step * PAGE_SIZE, N_HEADS, 2 * LANES), F32),
            pltpu.VMEM((N_HEADS, 2 * LANES), F32),
            pltpu.VMEM((N_HEADS, 2 * LANES), F32),
            pltpu.VMEM((2, N_HEADS, LANES), F32),
        ],
    )
    return pl.pallas_call(
        functools.partial(_decode_kernel, pages_per_step=pages_per_step, n_steps=n_steps, lam_init=lam_init),
        grid_spec=grid_spec,
        out_shape=[jax.ShapeDtypeStruct((n_dec, N_HEADS, LANES), F32)] * 3,
        compiler_params=_cparams(("parallel", "arbitrary")),
        name="sample_attn",
    )(pt, zs, *([ck] * pages_per_step), *([cv] * pages_per_step), pbias, sbias, rsum, qg2, kg2, *lams, sg)


def _merge_kernel(ya, on, ga, gb, h, wa, wb, wo, n2g, wr, br,
                  h1_ref, xp_ref, topi_ref, gate_ref, rank_ref, cnt_ref, carry,
                  *, tm, tiles_per_batch, t_real, n_dec):
    i = pl.program_id(0)

    @pl.when(i == 0)
    def _():
        carry[...] = jnp.zeros_like(carry)

    y_a = jnp.dot(ya[...], wa[...], preferred_element_type=F32)
    o_cat = jnp.concatenate([on[0, hd] for hd in range(N_HEADS)], axis=1)
    y_b = jnp.dot(o_cat, wb[...], preferred_element_type=F32)
    mix = jax.nn.sigmoid(ga[...]) * y_a + jax.nn.sigmoid(gb[...]) * y_b
    h1 = h[...] + jnp.dot(mix.astype(BF16), wo[...], preferred_element_type=F32)
    h1_ref[...] = h1

    xn = h1 * lax.rsqrt(jnp.mean(h1 * h1, axis=-1, keepdims=True) + EPS) * n2g[...]
    xb = xn.astype(BF16)
    xf = xb.astype(F32)
    bits = pltpu.bitcast(xf, U32)
    half = D_MODEL // 2
    xp_ref[...] = bits[:, :half] | (bits[:, half:] >> 16)

    logits = jnp.dot(xb, wr[...], preferred_element_type=F32) + br[...]

    lane = lax.broadcasted_iota(I32, (tm, LANES), 1)
    work = logits
    vals, idxs, sels = [], [], []
    for _ in range(TOP_K):
        mx = jnp.max(work, axis=-1, keepdims=True)
        idx = jnp.min(jnp.where(work == mx, lane, LANES), axis=-1, keepdims=True)
        sel = lane == idx
        vals.append(mx)
        idxs.append(idx)
        sels.append(sel)
        work = jnp.where(sel, -jnp.inf, work)
    exps = [jnp.exp(v - vals[0]) for v in vals]
    denom = exps[0] + exps[1] + exps[2] + exps[3]

    pos_in_batch = (i % tiles_per_batch) * tm + lax.broadcasted_iota(I32, (tm, 1), 0)
    limit = jnp.where(i // tiles_per_batch == 0, t_real + n_dec, t_real)
    valid = pos_in_batch < limit

    onehot = jnp.zeros((tm, LANES), F32)
    for sel in sels:
        onehot = onehot + jnp.where(sel, 1.0, 0.0)
    onehot = jnp.where(valid, onehot, 0.0)
    rr = lax.broadcasted_iota(I32, (tm, tm), 0)
    cc = lax.broadcasted_iota(I32, (tm, tm), 1)
    lower = jnp.where(rr > cc, 1.0, 0.0).astype(BF16)
    before = jnp.dot(lower, onehot.astype(BF16), preferred_element_type=F32) + carry[...]

    topi = jnp.zeros((tm, LANES), I32)
    gate = jnp.zeros((tm, LANES), F32)
    rank = jnp.zeros((tm, LANES), I32)
    for j in range(TOP_K):
        rj = jnp.sum(jnp.where(sels[j], before, 0.0), axis=-1, keepdims=True).astype(I32)
        topi = jnp.where(lane == j, idxs[j], topi)
        gate = jnp.where(lane == j, exps[j] / denom, gate)
        rank = jnp.where(lane == j, rj, rank)
    topi_ref[...] = topi
    gate_ref[...] = gate
    rank_ref[...] = rank
    carry[...] = carry[...] + jnp.sum(onehot, axis=0, keepdims=True)
    cnt_ref[...] = jnp.broadcast_to(carry[...], cnt_ref.shape)


def _merge(ya2d, on, z2d, h2d, wa, wb, wo, n2g, wr, br, tm, t_pad, t_real, n_dec):
    rows = h2d.shape[0]
    tpb = t_pad // tm
    const = lambda shape: pl.BlockSpec(shape, lambda i: (0,) * len(shape), pipeline_mode=pl.Buffered(1))
    row = lambda w, dt=None: pl.BlockSpec((tm, w), lambda i: (i, 0))
    return pl.pallas_call(
        functools.partial(_merge_kernel, tm=tm, tiles_per_batch=tpb, t_real=t_real, n_dec=n_dec),
        grid=(rows // tm,),
        in_specs=[
            row(D_CONV),
            pl.BlockSpec((1, N_HEADS, tm, LANES), lambda i: (i // tpb, 0, i % tpb, 0)),
            pl.BlockSpec((tm, D_MODEL), lambda i: (i, 3)),
            pl.BlockSpec((tm, D_MODEL), lambda i: (i, 4)),
            row(D_MODEL),
            const((D_CONV, D_MODEL)), const((D_ATTN, D_MODEL)), const((D_MODEL, D_MODEL)),
            const((1, D_MODEL)), const((D_MODEL, LANES)), const((1, LANES)),
        ],
        out_specs=[row(D_MODEL), row(D_MODEL // 2), row(LANES), row(LANES), row(LANES),
                   pl.BlockSpec((8, LANES), lambda i: (0, 0))],
        out_shape=[
            jax.ShapeDtypeStruct((rows, D_MODEL), F32),
            jax.ShapeDtypeStruct((rows, D_MODEL // 2), U32),
            jax.ShapeDtypeStruct((rows, LANES), I32),
            jax.ShapeDtypeStruct((rows, LANES), F32),
            jax.ShapeDtypeStruct((rows, LANES), I32),
            jax.ShapeDtypeStruct((8, LANES), F32),
        ],
        scratch_shapes=[pltpu.VMEM((1, LANES), F32)],
        compiler_params=_cparams(("arbitrary",)),
        name="merge_route",
    )(ya2d, on, z2d, z2d, h2d, wa, wb, wo, n2g, wr, br)


def _dispatch_kernel(pos_ref, zpos_ref, x_ref, xs_hbm, zbuf, sem, zsem, *, tm):
    i = pl.program_id(0)

    @pl.when(i == 0)
    def _():
        zbuf[...] = jnp.zeros_like(zbuf)

        def zstart(e, c):
            pltpu.make_async_copy(zbuf, xs_hbm.at[pl.ds(pl.multiple_of(zpos_ref[e], SUB_ROWS), SUB_ROWS)],
                                  zsem).start()
            return c

        def zwait(e, c):
            pltpu.make_async_copy(zbuf, xs_hbm.at[pl.ds(0, SUB_ROWS)], zsem).wait()
            return c

        lax.fori_loop(0, N_EXPERTS, zstart, 0)
        lax.fori_loop(0, N_EXPERTS, zwait, 0)

    base = i * tm

    def start(t, c):
        for j in range(TOP_K):
            dst = pos_ref[(base + t) * TOP_K + j]
            pltpu.make_async_copy(x_ref.at[pl.ds(t, 1)], xs_hbm.at[pl.ds(dst, 1)], sem).start()
        return c

    def wait(t, c):
        for j in range(TOP_K):
            pltpu.make_async_copy(x_ref.at[pl.ds(0, 1)], xs_hbm.at[pl.ds(0, 1)], sem).wait()
        return c

    lax.fori_loop(0, tm, start, 0)
    lax.fori_loop(0, tm, wait, 0)


def _dispatch(pos_flat, zpos, xp, n_slots, tm):
    rows, width = xp.shape
    grid_spec = pltpu.PrefetchScalarGridSpec(
        num_scalar_prefetch=2,
        grid=(rows // tm,),
        in_specs=[pl.BlockSpec((tm, width), lambda i, p, z: (i, 0))],
        out_specs=pl.BlockSpec(memory_space=pl.ANY),
        scratch_shapes=[pltpu.VMEM((SUB_ROWS, width), U32),
                        pltpu.SemaphoreType.DMA(()), pltpu.SemaphoreType.DMA(())],
    )
    return pl.pallas_call(
        functools.partial(_dispatch_kernel, tm=tm),
        grid_spec=grid_spec,
        out_shape=jax.ShapeDtypeStruct((n_slots, width), U32),
        compiler_params=_cparams(("arbitrary",)),
        name="moe_dispatch",
    )(pos_flat, zpos, xp)


def _moe_kernel(item_e, item_start, item_nsub, n_items,
                xs_hbm, w1g, w1u, w2, b1g, b1u, b2,
                ys_hbm, xraw, xa, xb, yacc, sem_in, sem_out, *, nf):
    it = pl.program_id(0)
    f = pl.program_id(1)
    half = D_MODEL // 2

    @pl.when(it < n_items[0])
    def _():
        start = pl.multiple_of(item_start[it], SUB_ROWS)
        nsub = item_nsub[it]

        def in_copy(s):
            r0 = pl.multiple_of(s * SUB_ROWS, SUB_ROWS)
            return pltpu.make_async_copy(xs_hbm.at[pl.ds(start + r0, SUB_ROWS)],
                                         xraw.at[pl.ds(r0, SUB_ROWS)], sem_in.at[s])

        def out_copy(s):
            r0 = pl.multiple_of(s * SUB_ROWS, SUB_ROWS)
            return pltpu.make_async_copy(yacc.at[pl.ds(r0, SUB_ROWS)],
                                         ys_hbm.at[pl.ds(start + r0, SUB_ROWS)], sem_out)

        @pl.when(f == 0)
        def _():
            lax.fori_loop(0, nsub, lambda s, c: (in_copy(s).start(), c)[1], 0)

            def unpack(s, c):
                in_copy(s).wait()
                r0 = pl.multiple_of(s * SUB_ROWS, SUB_ROWS)
                w = xraw[pl.ds(r0, SUB_ROWS), :]
                xa[pl.ds(r0, SUB_ROWS), :] = pltpu.bitcast(w & jnp.uint32(0xFFFF0000), F32).astype(BF16)
                xb[pl.ds(r0, SUB_ROWS), :] = pltpu.bitcast(w << 16, F32).astype(BF16)
                yacc[pl.ds(r0, SUB_ROWS), :] = jnp.broadcast_to(b2[0], (SUB_ROWS, D_MODEL))
                return c

            lax.fori_loop(0, nsub, unpack, 0)

        wg = w1g[0].astype(BF16)
        wu = w1u[0].astype(BF16)
        wd = w2[0].astype(BF16)

        def mlp_rows(r0, m):
            rows = pl.ds(pl.multiple_of(r0, SUB_ROWS), m)
            a = xa[rows, :]
            b = xb[rows, :]
            hg = (jnp.dot(a, wg[:half], preferred_element_type=F32)
                  + jnp.dot(b, wg[half:], preferred_element_type=F32) + b1g[0])
            hu = (jnp.dot(a, wu[:half], preferred_element_type=F32)
                  + jnp.dot(b, wu[half:], preferred_element_type=F32) + b1u[0])
            g = jnp.minimum(hg, SWIGLU_LIMIT)
            up = jnp.clip(hu, -SWIGLU_LIMIT, SWIGLU_LIMIT)
            act = g * jax.nn.sigmoid(SWIGLU_ALPHA * g) * (up + 1.0)
            yacc[rows, :] += jnp.dot(act.astype(BF16), wd, preferred_element_type=F32)

        per_big = BIG_ROWS // SUB_ROWS
        nbig = nsub // per_big
        lax.fori_loop(0, nbig, lambda c, u: (mlp_rows(c * BIG_ROWS, BIG_ROWS), u)[1], 0)
        lax.fori_loop(nbig * per_big, nsub, lambda s, u: (mlp_rows(s * SUB_ROWS, SUB_ROWS), u)[1], 0)

        @pl.when(f == nf - 1)
        def _():
            lax.fori_loop(0, nsub, lambda s, c: (out_copy(s).start(), c)[1], 0)
            lax.fori_loop(0, nsub, lambda s, c: (out_copy(s).wait(), c)[1], 0)


def _moe_experts(items, xs, w1, b1, w2, b2, n_slots, rmax, tf):
    item_e, item_start, item_nsub, n_items = items
    ni = item_e.shape[0]
    nf = D_FF // tf
    half = D_MODEL // 2

    def fidx(it, f, n):
        return jnp.where(it < n[0], f, nf - 1)

    grid_spec = pltpu.PrefetchScalarGridSpec(
        num_scalar_prefetch=4,
        grid=(ni, nf),
        in_specs=[
            pl.BlockSpec(memory_space=pl.ANY),
            pl.BlockSpec((1, D_MODEL, tf), lambda it, f, e, s, ns, n: (e[it], 0, fidx(it, f, n))),
            pl.BlockSpec((1, D_MODEL, tf), lambda it, f, e, s, ns, n: (e[it], 0, nf + fidx(it, f, n))),
            pl.BlockSpec((1, tf, D_MODEL), lambda it, f, e, s, ns, n: (e[it], fidx(it, f, n), 0)),
            pl.BlockSpec((1, 1, tf), lambda it, f, e, s, ns, n: (e[it], 0, fidx(it, f, n))),
            pl.BlockSpec((1, 1, tf), lambda it, f, e, s, ns, n: (e[it], 0, nf + fidx(it, f, n))),
            pl.BlockSpec((1, 1, D_MODEL), lambda it, f, e, s, ns, n: (e[it], 0, 0)),
        ],
        out_specs=pl.BlockSpec(memory_space=pl.ANY),
        scratch_shapes=[
            pltpu.VMEM((rmax, half), U32),
            pltpu.VMEM((rmax, half), BF16),
            pltpu.VMEM((rmax, half), BF16),
            pltpu.VMEM((rmax, D_MODEL), F32),
            pltpu.SemaphoreType.DMA((rmax // SUB_ROWS,)), pltpu.SemaphoreType.DMA(()),
        ],
    )
    return pl.pallas_call(
        functools.partial(_moe_kernel, nf=nf),
        grid_spec=grid_spec,
        out_shape=jax.ShapeDtypeStruct((n_slots, D_MODEL), F32),
        compiler_params=_cparams(("arbitrary", "arbitrary")),
        name="moe_experts",
    )(item_e, item_start, item_nsub, n_items, xs, w1, w1, w2,
      b1.reshape(N_EXPERTS, 1, 2 * D_FF), b1.reshape(N_EXPERTS, 1, 2 * D_FF), b2.reshape(N_EXPERTS, 1, D_MODEL))


def _combine_kernel(pos_ref, ys_hbm, h1_hbm, gate_hbm, y_ref, ysamp_ref, ybuf, hbuf, gbuf, sem, hsem,
                    *, tm, tiles_per_batch, t_pad, n_dec, samp_row0):
    i = pl.program_id(0)
    n_tiles = pl.num_programs(0)

    def gather_and_mix(row0, n):
        hcp = pltpu.make_async_copy(h1_hbm.at[pl.ds(row0, n)], hbuf.at[pl.ds(0, n)], hsem)
        gcp = pltpu.make_async_copy(gate_hbm.at[pl.ds(row0, n)], gbuf.at[pl.ds(0, n)], hsem)
        hcp.start()
        gcp.start()

        def start(t, c):
            for j in range(TOP_K):
                src = pos_ref[(row0 + t) * TOP_K + j]
                pltpu.make_async_copy(ys_hbm.at[pl.ds(src, 1)], ybuf.at[j, pl.ds(t, 1)], sem).start()
            return c

        def wait(t, c):
            for j in range(TOP_K):
                pltpu.make_async_copy(ys_hbm.at[pl.ds(0, 1)], ybuf.at[0, pl.ds(0, 1)], sem).wait()
            return c

        lax.fori_loop(0, n, start, 0)
        hcp.wait()
        gcp.wait()
        lax.fori_loop(0, n, wait, 0)
        g = gbuf[0:n, :]
        out = hbuf[0:n, :]
        for j in range(TOP_K):
            out = out + g[:, j:j + 1] * ybuf[j, 0:n, :]
        return out

    row0 = (i // tiles_per_batch) * t_pad + N_META + (i % tiles_per_batch) * tm
    y_ref[0] = gather_and_mix(pl.multiple_of(row0, 8), tm)

    @pl.when(i == n_tiles - 1)
    def _():
        ysamp_ref[...] = gather_and_mix(samp_row0, n_dec)


def _combine(pos_flat, ys, h1, gate, n_batch, seq, t_pad, n_dec, samp_row0, tm):
    tpb = seq // tm
    grid_spec = pltpu.PrefetchScalarGridSpec(
        num_scalar_prefetch=1,
        grid=(n_batch * tpb,),
        in_specs=[pl.BlockSpec(memory_space=pl.ANY)] * 3,
        out_specs=[pl.BlockSpec((1, tm, D_MODEL), lambda i, p: (i // tpb, i % tpb, 0)),
                   pl.BlockSpec((n_dec, D_MODEL), lambda i, p: (0, 0))],
        scratch_shapes=[
            pltpu.VMEM((TOP_K, tm, D_MODEL), F32),
            pltpu.VMEM((tm, D_MODEL), F32),
            pltpu.VMEM((tm, LANES), F32),
            pltpu.SemaphoreType.DMA(()), pltpu.SemaphoreType.DMA(()),
        ],
    )
    return pl.pallas_call(
        functools.partial(_combine_kernel, tm=tm, tiles_per_batch=tpb, t_pad=t_pad, n_dec=n_dec,
                          samp_row0=samp_row0),
        grid_spec=grid_spec,
        out_shape=[jax.ShapeDtypeStruct((n_batch, seq, D_MODEL), F32),
                   jax.ShapeDtypeStruct((n_dec, D_MODEL), F32)],
        compiler_params=_cparams(("arbitrary",)),
        name="moe_combine",
    )(pos_flat, ys, h1, gate)


def _t5_bucket(rel):
    n = jnp.maximum(rel, 0)
    max_exact = N_BUCKETS // 2
    nf = jnp.maximum(n, 1).astype(F32)
    large = max_exact + (jnp.log(nf / max_exact) / math.log(MAX_DISTANCE / max_exact)
                         * (N_BUCKETS - max_exact)).astype(I32)
    large = jnp.minimum(large, N_BUCKETS - 1)
    return jnp.where(n < max_exact, n, large)


def _bias_tables(rel_bias, tq):
    shifted = (rel_bias - rel_bias[N_BUCKETS - 1][None]) * LOG2E

    def bias_of(rel, out):
        onehot = jax.nn.one_hot(_t5_bucket(rel), N_BUCKETS, dtype=F32)
        return jnp.einsum("...b,bh->" + out, onehot, shifted, precision=lax.Precision.HIGHEST)

    rel0 = jnp.arange(tq)[:, None] - jnp.arange(tq)[None, :]
    diag = jnp.where((rel0 >= 0)[None], bias_of(rel0, "h..."), NEG_INF)
    tiles = jnp.stack([diag, bias_of(tq + rel0, "h...")])
    rel_last = PAGE_SIZE - jnp.arange(PAGE_SIZE)
    pbias = jnp.broadcast_to(bias_of(rel_last, "...h")[:, :, None], (PAGE_SIZE, N_HEADS, 2 * LANES))
    sbias = jnp.broadcast_to(bias_of(jnp.zeros((), I32), "...h")[:, None], (N_HEADS, 2 * LANES))
    return tiles, pbias, sbias


def _routing_tables(counts, topi, rank, valid, n_trash_rows, rmax, n_items_max, real_slots):
    padded = (counts + SUB_ROWS - 1) // SUB_ROWS * SUB_ROWS
    ends = jnp.cumsum(padded)
    off = ends - padded
    pos = off[topi] + rank
    trash_row = jnp.cumsum(jnp.logical_not(valid).astype(I32)) - 1
    trash = real_slots + trash_row[:, None] * TOP_K + jnp.arange(TOP_K, dtype=I32)[None]
    pos_scatter = jnp.where(valid[:, None], pos, trash).reshape(-1)
    pos_gather = jnp.where(valid[:, None], pos, 0).reshape(-1)
    zero_trash = real_slots + -(-(n_trash_rows * TOP_K) // SUB_ROWS) * SUB_ROWS
    zpos = jnp.where(counts > 0, ends - SUB_ROWS, zero_trash).astype(I32)
    per_e = (padded + rmax - 1) // rmax
    item_end = jnp.cumsum(per_e)
    n_items = item_end[-1]
    t = jnp.arange(n_items_max, dtype=I32)
    tt = jnp.minimum(t, n_items - 1)
    e_of = jnp.minimum(jnp.searchsorted(item_end, tt, side="right"), N_EXPERTS - 1).astype(I32)
    k = tt - (item_end - per_e)[e_of]
    start = off[e_of] + k * rmax
    nrows = jnp.minimum(rmax, padded[e_of] - k * rmax)
    nsub = jnp.where(t < n_items, nrows // SUB_ROWS, 0).astype(I32)
    items = (e_of, start.astype(I32), nsub, n_items.reshape(1).astype(I32))
    return pos_scatter.astype(I32), pos_gather.astype(I32), zpos, items, zero_trash + SUB_ROWS


def kernel(x_prompt, x_sample, cache_k, cache_v, state_conv, page_table, meta_tokens, rel_bias, norm1_g, w_in,
           conv_w, q_norm_g, k_norm_g, lambda_q1, lambda_k1, lambda_q2, lambda_k2, subln_g, w_branch_a,
           w_branch_b, w_out, norm2_g, w_router, b_router, w_mlp1, b_mlp1, w_mlp2, b_mlp2):
    n_batch, seq, _ = x_prompt.shape
    n_dec, t_dec, _ = x_sample.shape
    depth = cache_k.shape[0]
    assert depth == 1 and t_dec == 1 and n_dec == 8
    t_real = seq + N_META
    tq = 384
    t_pad = -(-(t_real + n_dec) // tq) * tq
    rows = n_batch * t_pad
    samp_row0 = t_real
    assert samp_row0 % 8 == 0 and t_pad % 1056 == 0
    lam_init = 0.8 - 0.6 * math.exp(-0.3 * 0)

    meta = jnp.broadcast_to(meta_tokens[None], (n_batch, N_META, D_MODEL))
    hp = jnp.concatenate([meta, x_prompt, jnp.zeros((n_batch, t_pad - t_real, D_MODEL), F32)], axis=1)
    hp = hp.at[0, samp_row0:samp_row0 + n_dec].set(x_sample[:, 0])
    h2d = hp.reshape(rows, D_MODEL)

    z2d = _in_proj(h2d, norm1_g[0], w_in[0], tm=1056, tn=512)
    z3 = z2d.reshape(n_batch, t_pad, D_IN)

    qg2 = jnp.tile(q_norm_g[0], 2).reshape(1, LANES)
    kg2 = jnp.tile(k_norm_g[0], 2).reshape(1, LANES)
    sg = subln_g[0].reshape(1, V_DIM)
    lams = [v[0].reshape(1, HEAD_DIM) for v in (lambda_q1, lambda_k1, lambda_q2, lambda_k2)]
    bias_tiles, pbias, sbias = _bias_tables(rel_bias, tq)

    ya, k_p, v_p, qs, kb, vb, ust = _prep(z3, conv_w[0], qg2, kg2, n_batch, t_pad, t_real, tm=tq)
    on = _prompt_attention(qs, kb, vb, bias_tiles, lams, sg, lam_init, t_pad, tq)

    row_block = samp_row0 // n_dec
    ya_s, u_s = _sample_conv(z2d, state_conv[0, :, 0], state_conv[0, :, 1], conv_w[0], row_block, n_dec)
    zs = z2d[samp_row0:samp_row0 + n_dec].reshape(n_dec, D_IN // LANES, LANES)
    o_s, k_s, v_s = _sample_attention(zs, cache_k, cache_v, page_table, pbias, sbias, qg2, kg2, lams, sg,
                                      lam_init, pages_per_step=4)

    ya = ya.at[0, samp_row0:samp_row0 + n_dec].set(ya_s.astype(BF16))
    on = on.at[0, :, samp_row0:samp_row0 + n_dec].set(o_s.transpose(1, 0, 2).astype(BF16))

    wr = jnp.pad(w_router[0], ((0, 0), (0, LANES - N_EXPERTS)))
    br = jnp.concatenate([b_router[0], jnp.full((LANES - N_EXPERTS,), NEG_INF, F32)]).reshape(1, LANES)
    h1, xp, topi, gate, rank, cnt = _merge(
        ya.reshape(rows, D_CONV), on, z2d, h2d,
        w_branch_a[0].astype(BF16), w_branch_b[0].astype(BF16), w_out[0].astype(BF16),
        norm2_g[0].reshape(1, D_MODEL), wr.astype(BF16), br, tm=tq // 2, t_pad=t_pad, t_real=t_real, n_dec=n_dec)

    rmax = 1536
    n_assign = (n_batch * t_real + n_dec) * TOP_K
    real_slots = -(-(n_assign + N_EXPERTS * (SUB_ROWS - 1)) // SUB_ROWS) * SUB_ROWS
    n_items_max = N_EXPERTS + real_slots // rmax
    pos_in_batch = jnp.arange(rows, dtype=I32) % t_pad
    valid = (pos_in_batch < t_real) | ((jnp.arange(rows) < t_pad) & (pos_in_batch < t_real + n_dec))
    pos_scatter, pos_gather, zpos, items, n_slots = _routing_tables(
        cnt[0, :N_EXPERTS].astype(I32), topi[:, :TOP_K], rank[:, :TOP_K], valid,
        rows - n_assign // TOP_K, rmax, n_items_max, real_slots)

    xs = _dispatch(pos_scatter, zpos, xp, n_slots, tm=tq)
    ys = _moe_experts(items, xs, w_mlp1[0], b_mlp1[0], w_mlp2[0], b_mlp2[0], n_slots, rmax, tf=256)
    y_prompt, y_samp = _combine(pos_gather, ys, h1, gate, n_batch, seq, t_pad, n_dec, samp_row0, tm=512)

    return (y_prompt,
            y_samp.reshape(n_dec, 1, D_MODEL),
            k_p.reshape(1, n_batch, t_real, N_HEADS, 2 * HEAD_DIM),
            v_p.reshape(1, n_batch, t_real, N_HEADS, V_DIM),
            ust[:, 6:8].reshape(1, n_batch, CONV_K - 1, D_CONV),
            k_s.reshape(1, n_dec, 1, N_HEADS, 2 * HEAD_DIM),
            v_s.reshape(1, n_dec, 1, N_HEADS, V_DIM),
            jnp.stack([state_conv[0, :, 1], u_s], axis=1).reshape(1, n_dec, CONV_K - 1, D_CONV))
```

```python
import functools
import math

import jax
import jax.numpy as jnp
import numpy as np
from jax import lax
from jax.experimental import pallas as pl
from jax.experimental.pallas import tpu as pltpu

D_MODEL = 2048
N_META = 16
D_CONV = 1024
CONV_K = 3
N_HEADS = 8
HEAD_DIM = 64
V_DIM = 128
D_QK = 1024
D_ATTN = 1024
D_IN = 10240
N_BUCKETS = 32
MAX_DISTANCE = 128
N_EXPERTS = 32
TOP_K = 4
D_FF = 2048
SWIGLU_LIMIT = 7.0
SWIGLU_ALPHA = 1.702
EPS = 1e-6
NEG_INF = -1e30
PAGE_SIZE = 128

F32 = jnp.float32
BF16 = jnp.bfloat16
I32 = jnp.int32
U32 = jnp.uint32

LANES = 128
SUB_ROWS = 128
BIG_ROWS = 512
ATTN_ROW_CHUNK = 64
LOG2E = 1.4426950408889634
DECODE_KEY_CHUNK = 16
D_QK_OFF = 3 * D_CONV
VMEM_LIMIT = 56 * 1024 * 1024


def _cparams(sem, vmem=VMEM_LIMIT):
    return pltpu.CompilerParams(dimension_semantics=sem, vmem_limit_bytes=vmem)


def _assemble_kernel(xp_hbm, meta_hbm, xs_hbm, out_hbm, zbuf, sems, *, n_batch, seq, t_real, t_pad, n_dec):
    zbuf[...] = jnp.zeros_like(zbuf)
    copies = []

    def add(src, dst):
        copies.append(pltpu.make_async_copy(src, dst, sems.at[len(copies)]))

    for b in range(n_batch):
        pad0 = t_real + (n_dec if b == 0 else 0)
        add(meta_hbm, out_hbm.at[b, pl.ds(0, N_META)])
        add(xp_hbm.at[b], out_hbm.at[b, pl.ds(N_META, seq)])
        add(zbuf.at[pl.ds(0, t_pad - pad0)], out_hbm.at[b, pl.ds(pad0, t_pad - pad0)])
    add(xs_hbm, out_hbm.at[0, pl.ds(t_real, n_dec)])
    for c in copies:
        c.start()
    for c in copies:
        c.wait()


def _assemble(x_prompt, meta_tokens, x_sample2d, t_pad):
    n_batch, seq, _ = x_prompt.shape
    n_dec = x_sample2d.shape[0]
    t_real = seq + N_META
    return pl.pallas_call(
        functools.partial(_assemble_kernel, n_batch=n_batch, seq=seq, t_real=t_real, t_pad=t_pad, n_dec=n_dec),
        grid=(1,),
        in_specs=[pl.BlockSpec(memory_space=pl.ANY)] * 3,
        out_specs=pl.BlockSpec(memory_space=pl.ANY),
        out_shape=jax.ShapeDtypeStruct((n_batch, t_pad, D_MODEL), F32),
        scratch_shapes=[pltpu.VMEM((t_pad - t_real, D_MODEL), F32),
                        pltpu.SemaphoreType.DMA((3 * n_batch + 1,))],
        name="assemble",
    )(x_prompt, meta_tokens, x_sample2d)


def _in_proj_kernel(x_ref, g_ref, w_ref, z_ref, xn_ref):
    @pl.when(pl.program_id(1) == 0)
    def _():
        x = x_ref[...]
        ms = jnp.mean(x * x, axis=-1, keepdims=True)
        xn_ref[...] = (x * lax.rsqrt(ms + EPS) * g_ref[...]).astype(BF16)

    z_ref[...] = jnp.dot(xn_ref[...], w_ref[...].astype(BF16), preferred_element_type=F32)


def _in_proj(h2d, g, w, tm, tn):
    rows = h2d.shape[0]
    return pl.pallas_call(
        _in_proj_kernel,
        grid=(rows // tm, D_IN // tn),
        in_specs=[
            pl.BlockSpec((tm, D_MODEL), lambda i, j: (i, 0)),
            pl.BlockSpec((1, D_MODEL), lambda i, j: (0, 0)),
            pl.BlockSpec((D_MODEL, tn), lambda i, j: (0, j)),
        ],
        out_specs=pl.BlockSpec((tm, tn), lambda i, j: (i, j)),
        out_shape=jax.ShapeDtypeStruct((rows, D_IN), F32),
        scratch_shapes=[pltpu.VMEM((tm, D_MODEL), BF16)],
        compiler_params=_cparams(("parallel", "arbitrary")),
        name="in_proj",
    )(h2d, g.reshape(1, D_MODEL), w)


def _half_norm(x, g2, lo):
    t = x * x
    s_lo = jnp.sum(jnp.where(lo, t, 0.0), axis=-1, keepdims=True)
    s_hi = jnp.sum(jnp.where(lo, 0.0, t), axis=-1, keepdims=True)
    inv = jnp.where(lo, lax.rsqrt(s_lo * (1.0 / HEAD_DIM) + EPS), lax.rsqrt(s_hi * (1.0 / HEAD_DIM) + EPS))
    return x * inv * g2


def _prep_kernel(zb, zc, zx, zq, zk, zv, cw, qg, kg,
                 ya_ref, kout, vout, qs, kb, vb, ust, carry, *, tm, state_tile, state_row):
    i = pl.program_id(1)

    @pl.when(i == 0)
    def _():
        carry[...] = jnp.zeros_like(carry)

    u = zc[0] * zx[0]
    prev = carry[...]
    row = lax.broadcasted_iota(I32, (tm, 1), 0)
    u1 = jnp.where(row == 0, prev[7:8], pltpu.roll(u, 1, 0))
    u2 = jnp.where(row == 0, prev[6:7], jnp.where(row == 1, prev[7:8], pltpu.roll(u, 2, 0)))
    y = cw[0:1] * u2 + cw[1:2] * u1 + cw[2:3] * u
    ya_ref[0] = (zb[0] * y).astype(BF16)
    carry[...] = u[tm - 8:tm]

    @pl.when(i == state_tile)
    def _():
        ust[0] = u[state_row:state_row + 8]

    lo = lax.broadcasted_iota(I32, (1, LANES), 1) < HEAD_DIM
    for h in range(N_HEADS):
        sl = slice(h * LANES, (h + 1) * LANES)
        qn = _half_norm(zq[0, :, sl], qg[...], lo) * (LOG2E * HEAD_DIM ** -0.5)
        qs[0, 0, h] = jnp.where(lo, qn, 0.0).astype(BF16)
        qs[0, 1, h] = jnp.where(lo, 0.0, qn).astype(BF16)
        kn = _half_norm(zk[0, :, sl], kg[...], lo)
        kout[0, :, sl] = kn
        kb[0, h] = kn.astype(BF16)
        vb[0, h] = zv[0, :, sl].astype(BF16)
    vout[0] = zv[0]


def _prep(z3, conv_w, qg2, kg2, n_batch, t_pad, t_real, tm):
    nt = t_pad // tm
    state_tile = (t_real - 2) // tm
    state_row = ((t_real - 2) % tm) // 8 * 8
    sec = lambda s: pl.BlockSpec((1, tm, 1024), lambda b, i, s=s: (b, i, s))
    small = lambda shape: pl.BlockSpec(shape, lambda b, i: (0,) * len(shape))
    return pl.pallas_call(
        functools.partial(_prep_kernel, tm=tm, state_tile=state_tile, state_row=state_row),
        grid=(n_batch, nt),
        in_specs=[sec(0), sec(1), sec(2), sec(3), sec(4), sec(5),
                  small((CONV_K, D_CONV)), small((1, LANES)), small((1, LANES))],
        out_specs=[
            pl.BlockSpec((1, tm, D_CONV), lambda b, i: (b, i, 0)),
            pl.BlockSpec((1, tm, D_QK), lambda b, i: (b, i, 0)),
            pl.BlockSpec((1, tm, D_ATTN), lambda b, i: (b, i, 0)),
            pl.BlockSpec((1, 2, N_HEADS, tm, LANES), lambda b, i: (b, 0, 0, i, 0)),
            pl.BlockSpec((1, N_HEADS, tm, LANES), lambda b, i: (b, 0, i, 0)),
            pl.BlockSpec((1, N_HEADS, tm, LANES), lambda b, i: (b, 0, i, 0)),
            pl.BlockSpec((1, 8, D_CONV), lambda b, i: (b, 0, 0)),
        ],
        out_shape=[
            jax.ShapeDtypeStruct((n_batch, t_pad, D_CONV), BF16),
            jax.ShapeDtypeStruct((n_batch, t_real, D_QK), F32),
            jax.ShapeDtypeStruct((n_batch, t_real, D_ATTN), F32),
            jax.ShapeDtypeStruct((n_batch, 2, N_HEADS, t_pad, LANES), BF16),
            jax.ShapeDtypeStruct((n_batch, N_HEADS, t_pad, LANES), BF16),
            jax.ShapeDtypeStruct((n_batch, N_HEADS, t_pad, LANES), BF16),
            jax.ShapeDtypeStruct((n_batch, 8, D_CONV), F32),
        ],
        scratch_shapes=[pltpu.VMEM((8, D_CONV), F32)],
        compiler_params=_cparams(("parallel", "arbitrary")),
        name="prep",
    )(z3, z3, z3, z3, z3, z3, conv_w, qg2, kg2)


def _lambda_value(lq1, lk1, lq2, lk2, lam_init):
    a = jnp.sum(lq1[...] * lk1[...], axis=-1, keepdims=True)
    b = jnp.sum(lq2[...] * lk2[...], axis=-1, keepdims=True)
    return jnp.exp(a) - jnp.exp(b) + lam_init


def _attn_kernel(qi_tab, ki_tab, q_ref, k_ref, v_ref, bias_ref, lq1, lk1, lq2, lk2, sg,
                 o_ref, m_ref, l_ref, acc_ref, s_ref, p_ref, a_ref, *, tq, tk, lam_init):
    step = pl.program_id(1)
    qi = qi_tab[step]
    ki = ki_tab[step]
    rc = ATTN_ROW_CHUNK

    @pl.when(ki == 0)
    def _():
        m_ref[...] = jnp.full_like(m_ref, NEG_INF)
        l_ref[...] = jnp.zeros_like(l_ref)
        acc_ref[...] = jnp.zeros_like(acc_ref)

    def head_step(h, near):
        q = q_ref[0, :, h].reshape(2 * tq, LANES)
        s_ref[...] = lax.dot_general(q, k_ref[0, h], (((1,), (1,)), ((), ())), preferred_element_type=F32)

        def chunk(c, carry):
            r0 = pl.multiple_of(c * rc, rc)
            rows = pl.ds(r0, rc)
            s = s_ref[rows, :]
            if near:
                b0 = pl.multiple_of((c % (tq // rc)) * rc, rc)
                s = s + bias_ref[0, h, pl.ds(b0, rc), :]
            m_prev = m_ref[h, rows, :]
            m_new = jnp.maximum(m_prev, jnp.max(s, axis=-1, keepdims=True))
            alpha = jnp.exp2(m_prev - m_new)
            p = jnp.exp2(s - jnp.concatenate([m_new] * (tk // LANES), axis=1))
            l_ref[h, rows, :] = alpha * l_ref[h, rows, :] + jnp.sum(p, axis=-1, keepdims=True)
            m_ref[h, rows, :] = m_new
            a_ref[rows, :] = alpha
            p_ref[rows, :] = p.astype(BF16)
            return carry

        lax.fori_loop(0, 2 * tq // rc, chunk, 0, unroll=True)
        acc_ref[h] = a_ref[...] * acc_ref[h] + jnp.dot(p_ref[...], v_ref[0, h], preferred_element_type=F32)

    @pl.when(ki >= qi - 1)
    def _():
        lax.fori_loop(0, N_HEADS, lambda h, c: (head_step(h, True), c)[1], 0)

    @pl.when(ki < qi - 1)
    def _():
        lax.fori_loop(0, N_HEADS, lambda h, c: (head_step(h, False), c)[1], 0)

    @pl.when(ki == qi)
    def _():
        lam = _lambda_value(lq1, lk1, lq2, lk2, lam_init)

        def fin(h, c):
            acc = acc_ref[h]
            l = l_ref[h]
            o = acc[:tq] / l[:tq] - lam * (acc[tq:] / l[tq:])
            on = o * lax.rsqrt(jnp.mean(o * o, axis=-1, keepdims=True) + EPS) * sg[...]
            o_ref[0, h] = (on * (1.0 - lam_init)).astype(BF16)
            return c

        lax.fori_loop(0, N_HEADS, fin, 0)


def _prompt_attention(qs, kb, vb, bias_tiles, lams, sg, lam_init, t_pad, tq):
    n_batch = qs.shape[0]
    nq = t_pad // tq
    pairs = [(qi, ki) for qi in range(nq) for ki in range(qi + 1)]
    qi_tab = jnp.asarray(np.array([p[0] for p in pairs], np.int32))
    ki_tab = jnp.asarray(np.array([p[1] for p in pairs], np.int32))
    vec = lambda n: pl.BlockSpec((1, n), lambda b, s, qt, kt: (0, 0))
    grid_spec = pltpu.PrefetchScalarGridSpec(
        num_scalar_prefetch=2,
        grid=(n_batch, len(pairs)),
        in_specs=[
            pl.BlockSpec((1, 2, N_HEADS, tq, LANES), lambda b, s, qt, kt: (b, 0, 0, qt[s], 0)),
            pl.BlockSpec((1, N_HEADS, tq, LANES), lambda b, s, qt, kt: (b, 0, kt[s], 0)),
            pl.BlockSpec((1, N_HEADS, tq, LANES), lambda b, s, qt, kt: (b, 0, kt[s], 0)),
            pl.BlockSpec((1, N_HEADS, tq, tq),
                         lambda b, s, qt, kt: (jnp.minimum(qt[s] - kt[s], 1), 0, 0, 0)),
            vec(HEAD_DIM), vec(HEAD_DIM), vec(HEAD_DIM), vec(HEAD_DIM), vec(V_DIM),
        ],
        out_specs=pl.BlockSpec((1, N_HEADS, tq, LANES), lambda b, s, qt, kt: (b, 0, qt[s], 0)),
        scratch_shapes=[
            pltpu.VMEM((N_HEADS, 2 * tq, LANES), F32),
            pltpu.VMEM((N_HEADS, 2 * tq, LANES), F32),
            pltpu.VMEM((N_HEADS, 2 * tq, LANES), F32),
            pltpu.VMEM((2 * tq, tq), F32),
            pltpu.VMEM((2 * tq, tq), BF16),
            pltpu.VMEM((2 * tq, LANES), F32),
        ],
    )
    return pl.pallas_call(
        functools.partial(_attn_kernel, tq=tq, tk=tq, lam_init=lam_init),
        grid_spec=grid_spec,
        out_shape=jax.ShapeDtypeStruct((n_batch, N_HEADS, t_pad, LANES), BF16),
        compiler_params=_cparams(("parallel", "arbitrary")),
        name="prompt_attn",
    )(qi_tab, ki_tab, qs, kb, vb, bias_tiles, *lams, sg)


def _sample_conv_kernel(zb, zc, zx, s0, s1, cw, ya_ref, u_ref):
    u = zc[...] * zx[...]
    y = cw[0:1] * s0[...] + cw[1:2] * s1[...] + cw[2:3] * u
    ya_ref[...] = zb[...] * y
    u_ref[...] = u


def _sample_conv(z2d, s0, s1, conv_w, row_block, n):
    sec = lambda s: pl.BlockSpec((n, D_CONV), lambda i, s=s: (row_block, s))
    full = lambda r: pl.BlockSpec((r, D_CONV), lambda i: (0, 0))
    return pl.pallas_call(
        _sample_conv_kernel,
        grid=(1,),
        in_specs=[sec(0), sec(1), sec(2), full(n), full(n), full(CONV_K)],
        out_specs=[full(n), full(n)],
        out_shape=[jax.ShapeDtypeStruct((n, D_CONV), F32)] * 2,
        name="sample_conv",
    )(z2d, z2d, z2d, s0, s1, conv_w)


def _decode_kernel(pt_ref, zs, *rest, pages_per_step, n_steps, lam_init):
    kpages = rest[:pages_per_step]
    vpages = rest[pages_per_step:2 * pages_per_step]
    (pbias, sbias, rsum, qg, kg, lq1, lk1, lq2, lk2, sg,
     o_ref, kout, vout, q_sc, s_ref, m_ref, l_ref, acc_ref) = rest[2 * pages_per_step:]
    p = pl.program_id(1)
    kc = DECODE_KEY_CHUNK
    q_lo, k_lo, v_lo = D_QK_OFF // LANES, (D_QK_OFF + D_QK) // LANES, (D_QK_OFF + 2 * D_QK) // LANES
    lo = lax.broadcasted_iota(I32, (1, LANES), 1) < HEAD_DIM

    @pl.when(p == 0)
    def _():
        q_sc[...] = _half_norm(zs[0, q_lo:q_lo + N_HEADS], qg[...], lo) * (LOG2E * HEAD_DIM ** -0.5)
        kout[0] = _half_norm(zs[0, k_lo:k_lo + N_HEADS], kg[...], lo)
        vout[0] = zs[0, v_lo:v_lo + N_HEADS]
        m_ref[...] = jnp.full_like(m_ref, NEG_INF)
        l_ref[...] = jnp.zeros_like(l_ref)
        acc_ref[...] = jnp.zeros_like(acc_ref)

    q = q_sc[...]

    def scores(k):
        n = k.shape[0]
        t = (k * q[None]).reshape(n * N_HEADS, LANES).astype(BF16)
        return jnp.dot(t, rsum[...], preferred_element_type=F32).reshape(n, N_HEADS, 2 * LANES)

    m_prev = m_ref[...]
    m_new = m_prev
    for i in range(pages_per_step):
        s = scores(kpages[i][...])
        if i == pages_per_step - 1:
            s = s + jnp.where(p == n_steps - 1, pbias[...], 0.0)
        s_ref[pl.ds(i * PAGE_SIZE, PAGE_SIZE)] = s
        m_new = jnp.maximum(m_new, jnp.max(s, axis=0))
    alpha = jnp.exp2(m_prev - m_new)

    def accumulate(i):
        def body(c, carry):
            l, a0, a1 = carry
            pc = jnp.exp2(s_ref[pl.ds(i * PAGE_SIZE + c * kc, kc)] - m_new[None])
            vc = vpages[i][pl.ds(c * kc, kc)]
            return (l + jnp.sum(pc, axis=0),
                    a0 + jnp.sum(pc[:, :, :LANES] * vc, axis=0),
                    a1 + jnp.sum(pc[:, :, LANES:] * vc, axis=0))
        return body

    carry = (alpha * l_ref[...], alpha[:, :LANES] * acc_ref[0], alpha[:, LANES:] * acc_ref[1])
    for i in range(pages_per_step):
        carry = lax.fori_loop(0, PAGE_SIZE // kc, accumulate(i), carry)
    l, a0, a1 = carry
    m_ref[...] = m_new
    l_ref[...] = l
    acc_ref[0] = a0
    acc_ref[1] = a1

    @pl.when(p == n_steps - 1)
    def _():
        k_own = _half_norm(zs[0, k_lo:k_lo + N_HEADS], kg[...], lo)
        v_own = zs[0, v_lo:v_lo + N_HEADS]
        s_own = scores(k_own[None])[0] + sbias[...]
        m_fin = jnp.maximum(m_new, s_own)
        a_fin = jnp.exp2(m_new - m_fin)
        p_own = jnp.exp2(s_own - m_fin)
        l_fin = a_fin * l + p_own
        o0 = (a_fin[:, :LANES] * a0 + p_own[:, :LANES] * v_own) / l_fin[:, :LANES]
        o1 = (a_fin[:, LANES:] * a1 + p_own[:, LANES:] * v_own) / l_fin[:, LANES:]
        o = o0 - _lambda_value(lq1, lk1, lq2, lk2, lam_init) * o1
        on = o * lax.rsqrt(jnp.mean(o * o, axis=-1, keepdims=True) + EPS) * sg[...]
        o_ref[0] = on * (1.0 - lam_init)


def _sample_attention(zs, ck, cv, page_table, pbias, sbias, qg2, kg2, lams, sg, lam_init, pages_per_step):
    n_dec = zs.shape[0]
    n_pages = page_table.shape[1]
    n_steps = n_pages // pages_per_step
    pt = page_table.reshape(-1)
    page = lambda i: pl.BlockSpec(
        (None, None, PAGE_SIZE, N_HEADS, LANES),
        lambda r, p, pt, i=i: (0, pt[r * n_pages + p * pages_per_step + i], 0, 0, 0))
    vec = lambda shape: pl.BlockSpec(shape, lambda r, p, pt: (0,) * len(shape))
    out_row = pl.BlockSpec((1, N_HEADS, LANES), lambda r, p, pt: (r, 0, 0))
    rsum = (jnp.arange(LANES)[:, None] // HEAD_DIM == jnp.arange(2 * LANES)[None, :] // LANES).astype(BF16)
    grid_spec = pltpu.PrefetchScalarGridSpec(
        num_scalar_prefetch=1,
        grid=(n_dec, n_steps),
        in_specs=[pl.BlockSpec((1,) + zs.shape[1:], lambda r, p, pt: (r, 0, 0))]
        + [page(i) for i in range(pages_per_step)] + [page(i) for i in range(pages_per_step)]
        + [vec((PAGE_SIZE, N_HEADS, 2 * LANES)), vec((N_HEADS, 2 * LANES)), vec((LANES, 2 * LANES)),
           vec((1, LANES)), vec((1, LANES)),
           vec((1, HEAD_DIM)), vec((1, HEAD_DIM)), vec((1, HEAD_DIM)), vec((1, HEAD_DIM)), vec((1, V_DIM))],
        out_specs=[out_row, out_row, out_row],
        scratch_shapes=[
            pltpu.VMEM((N_HEADS, LANES), F32),
            pltpu.VMEM((pages_per_step * PAGE_SIZE, N_HEADS, 2 * LANES), F32),
            pltpu.VMEM((N_HEADS, 2 * LANES), F32),
            pltpu.VMEM((N_HEADS, 2 * LANES), F32),
            pltpu.VMEM((2, N_HEADS, LANES), F32),
        ],
    )
    return pl.pallas_call(
        functools.partial(_decode_kernel, pages_per_step=pages_per_step, n_steps=n_steps, lam_init=lam_init),
        grid_spec=grid_spec,
        out_shape=[jax.ShapeDtypeStruct((n_dec, N_HEADS, LANES), F32)] * 3,
        compiler_params=_cparams(("parallel", "arbitrary")),
        name="sample_attn",
    )(pt, zs, *([ck] * pages_per_step), *([cv] * pages_per_step), pbias, sbias, rsum, qg2, kg2, *lams, sg)


def _merge_kernel(ya, on, ga, gb, h, wa, wb, wo, n2g, wr, br,
                  h1_ref, xp_ref, topi_ref, gate_ref, rank_ref, cnt_ref, carry,
                  *, tm, tiles_per_batch, t_real, n_dec):
    i = pl.program_id(0)

    @pl.when(i == 0)
    def _():
        carry[...] = jnp.zeros_like(carry)

    y_a = jnp.dot(ya[...], wa[...], preferred_element_type=F32)
    o_cat = jnp.concatenate([on[0, hd] for hd in range(N_HEADS)], axis=1)
    y_b = jnp.dot(o_cat, wb[...], preferred_element_type=F32)
    mix = jax.nn.sigmoid(ga[...]) * y_a + jax.nn.sigmoid(gb[...]) * y_b
    h1 = h[...] + jnp.dot(mix.astype(BF16), wo[...], preferred_element_type=F32)
    h1_ref[...] = h1

    xn = h1 * lax.rsqrt(jnp.mean(h1 * h1, axis=-1, keepdims=True) + EPS) * n2g[...]
    xb = xn.astype(BF16)
    xf = xb.astype(F32)
    bits = pltpu.bitcast(xf, U32)
    half = D_MODEL // 2
    xp_ref[...] = bits[:, :half] | (bits[:, half:] >> 16)

    logits = jnp.dot(xb, wr[...], preferred_element_type=F32) + br[...]

    lane = lax.broadcasted_iota(I32, (tm, LANES), 1)
    work = logits
    vals, idxs, sels = [], [], []
    for _ in range(TOP_K):
        mx = jnp.max(work, axis=-1, keepdims=True)
        idx = jnp.min(jnp.where(work == mx, lane, LANES), axis=-1, keepdims=True)
        sel = lane == idx
        vals.append(mx)
        idxs.append(idx)
        sels.append(sel)
        work = jnp.where(sel, -jnp.inf, work)
    exps = [jnp.exp(v - vals[0]) for v in vals]
    denom = exps[0] + exps[1] + exps[2] + exps[3]

    pos_in_batch = (i % tiles_per_batch) * tm + lax.broadcasted_iota(I32, (tm, 1), 0)
    limit = jnp.where(i // tiles_per_batch == 0, t_real + n_dec, t_real)
    valid = pos_in_batch < limit

    onehot = jnp.zeros((tm, LANES), F32)
    for sel in sels:
        onehot = onehot + jnp.where(sel, 1.0, 0.0)
    onehot = jnp.where(valid, onehot, 0.0)
    rr = lax.broadcasted_iota(I32, (tm, tm), 0)
    cc = lax.broadcasted_iota(I32, (tm, tm), 1)
    lower = jnp.where(rr > cc, 1.0, 0.0).astype(BF16)
    before = jnp.dot(lower, onehot.astype(BF16), preferred_element_type=F32) + carry[...]

    topi = jnp.zeros((tm, LANES), I32)
    gate = jnp.zeros((tm, LANES), F32)
    rank = jnp.zeros((tm, LANES), I32)
    for j in range(TOP_K):
        rj = jnp.sum(jnp.where(sels[j], before, 0.0), axis=-1, keepdims=True).astype(I32)
        topi = jnp.where(lane == j, idxs[j], topi)
        gate = jnp.where(lane == j, exps[j] / denom, gate)
        rank = jnp.where(lane == j, rj, rank)
    topi_ref[...] = topi
    gate_ref[...] = gate
    rank_ref[...] = rank
    carry[...] = carry[...] + jnp.sum(onehot, axis=0, keepdims=True)
    cnt_ref[...] = jnp.broadcast_to(carry[...], cnt_ref.shape)


def _merge(ya2d, on, z2d, h2d, wa, wb, wo, n2g, wr, br, tm, t_pad, t_real, n_dec):
    rows = h2d.shape[0]
    tpb = t_pad // tm
    const = lambda shape: pl.BlockSpec(shape, lambda i: (0,) * len(shape), pipeline_mode=pl.Buffered(1))
    row = lambda w, dt=None: pl.BlockSpec((tm, w), lambda i: (i, 0))
    return pl.pallas_call(
        functools.partial(_merge_kernel, tm=tm, tiles_per_batch=tpb, t_real=t_real, n_dec=n_dec),
        grid=(rows // tm,),
        in_specs=[
            row(D_CONV),
            pl.BlockSpec((1, N_HEADS, tm, LANES), lambda i: (i // tpb, 0, i % tpb, 0)),
            pl.BlockSpec((tm, D_MODEL), lambda i: (i, 3)),
            pl.BlockSpec((tm, D_MODEL), lambda i: (i, 4)),
            row(D_MODEL),
            const((D_CONV, D_MODEL)), const((D_ATTN, D_MODEL)), const((D_MODEL, D_MODEL)),
            const((1, D_MODEL)), const((D_MODEL, LANES)), const((1, LANES)),
        ],
        out_specs=[row(D_MODEL), row(D_MODEL // 2), row(LANES), row(LANES), row(LANES),
                   pl.BlockSpec((8, LANES), lambda i: (0, 0))],
        out_shape=[
            jax.ShapeDtypeStruct((rows, D_MODEL), F32),
            jax.ShapeDtypeStruct((rows, D_MODEL // 2), U32),
            jax.ShapeDtypeStruct((rows, LANES), I32),
            jax.ShapeDtypeStruct((rows, LANES), F32),
            jax.ShapeDtypeStruct((rows, LANES), I32),
            jax.ShapeDtypeStruct((8, LANES), F32),
        ],
        scratch_shapes=[pltpu.VMEM((1, LANES), F32)],
        compiler_params=_cparams(("arbitrary",)),
        name="merge_route",
    )(ya2d, on, z2d, z2d, h2d, wa, wb, wo, n2g, wr, br)


def _dispatch_kernel(pos_ref, zpos_ref, x_ref, xs_hbm, zbuf, sem, zsem, *, tm):
    i = pl.program_id(0)

    @pl.when(i == 0)
    def _():
        zbuf[...] = jnp.zeros_like(zbuf)

        def zstart(e, c):
            pltpu.make_async_copy(zbuf, xs_hbm.at[pl.ds(pl.multiple_of(zpos_ref[e], SUB_ROWS), SUB_ROWS)],
                                  zsem).start()
            return c

        def zwait(e, c):
            pltpu.make_async_copy(zbuf, xs_hbm.at[pl.ds(0, SUB_ROWS)], zsem).wait()
            return c

        lax.fori_loop(0, N_EXPERTS, zstart, 0)
        lax.fori_loop(0, N_EXPERTS, zwait, 0)

    base = i * tm

    def start(t, c):
        for j in range(TOP_K):
            dst = pos_ref[(base + t) * TOP_K + j]
            pltpu.make_async_copy(x_ref.at[pl.ds(t, 1)], xs_hbm.at[pl.ds(dst, 1)], sem).start()
        return c

    def wait(t, c):
        for j in range(TOP_K):
            pltpu.make_async_copy(x_ref.at[pl.ds(0, 1)], xs_hbm.at[pl.ds(0, 1)], sem).wait()
        return c

    lax.fori_loop(0, tm, start, 0)
    lax.fori_loop(0, tm, wait, 0)


def _dispatch(pos_flat, zpos, xp, n_slots, tm):
    rows, width = xp.shape
    grid_spec = pltpu.PrefetchScalarGridSpec(
        num_scalar_prefetch=2,
        grid=(rows // tm,),
        in_specs=[pl.BlockSpec((tm, width), lambda i, p, z: (i, 0))],
        out_specs=pl.BlockSpec(memory_space=pl.ANY),
        scratch_shapes=[pltpu.VMEM((SUB_ROWS, width), U32),
                        pltpu.SemaphoreType.DMA(()), pltpu.SemaphoreType.DMA(())],
    )
    return pl.pallas_call(
        functools.partial(_dispatch_kernel, tm=tm),
        grid_spec=grid_spec,
        out_shape=jax.ShapeDtypeStruct((n_slots, width), U32),
        compiler_params=_cparams(("arbitrary",)),
        name="moe_dispatch",
    )(pos_flat, zpos, xp)


def _moe_kernel(item_e, item_start, item_nsub, n_items,
                xs_hbm, w1g, w1u, w2, b1g, b1u, b2,
                ys_hbm, xraw, xa, xb, yacc, sem_in, sem_out, *, nf):
    it = pl.program_id(0)
    f = pl.program_id(1)
    half = D_MODEL // 2

    @pl.when(it < n_items[0])
    def _():
        start = pl.multiple_of(item_start[it], SUB_ROWS)
        nsub = item_nsub[it]

        def in_copy(s):
            r0 = pl.multiple_of(s * SUB_ROWS, SUB_ROWS)
            return pltpu.make_async_copy(xs_hbm.at[pl.ds(start + r0, SUB_ROWS)],
                                         xraw.at[pl.ds(r0, SUB_ROWS)], sem_in.at[s])

        def out_copy(r0, m):
            rows = pl.ds(pl.multiple_of(r0, SUB_ROWS), m)
            return pltpu.make_async_copy(yacc.at[rows], ys_hbm.at[pl.ds(start + r0, m)], sem_out)

        def mlp_rows(r0, m, first, last):
            r0 = pl.multiple_of(r0, SUB_ROWS)
            rows = pl.ds(r0, m)
            if first:
                for k in range(m // SUB_ROWS):
                    sub = pl.ds(r0 + k * SUB_ROWS, SUB_ROWS)
                    in_copy(r0 // SUB_ROWS + k).wait()
                    w = xraw[sub, :]
                    xa[sub, :] = pltpu.bitcast(w & jnp.uint32(0xFFFF0000), F32).astype(BF16)
                    xb[sub, :] = pltpu.bitcast(w << 16, F32).astype(BF16)
            wg = w1g[0].astype(BF16)
            wu = w1u[0].astype(BF16)
            wd = w2[0].astype(BF16)
            a = xa[rows, :]
            b = xb[rows, :]
            hg = (jnp.dot(a, wg[:half], preferred_element_type=F32)
                  + jnp.dot(b, wg[half:], preferred_element_type=F32) + b1g[0])
            hu = (jnp.dot(a, wu[:half], preferred_element_type=F32)
                  + jnp.dot(b, wu[half:], preferred_element_type=F32) + b1u[0])
            g = jnp.minimum(hg, SWIGLU_LIMIT)
            up = jnp.clip(hu, -SWIGLU_LIMIT, SWIGLU_LIMIT)
            act = g * jax.nn.sigmoid(SWIGLU_ALPHA * g) * (up + 1.0)
            y = jnp.dot(act.astype(BF16), wd, preferred_element_type=F32)
            yacc[rows, :] = y + (b2[0] if first else yacc[rows, :])
            if last:
                out_copy(r0, m).start()

        per_big = BIG_ROWS // SUB_ROWS
        nbig = nsub // per_big

        def sweep(first, last):
            lax.fori_loop(0, nbig, lambda c, u: (mlp_rows(c * BIG_ROWS, BIG_ROWS, first, last), u)[1], 0)
            lax.fori_loop(nbig * per_big, nsub,
                          lambda s, u: (mlp_rows(s * SUB_ROWS, SUB_ROWS, first, last), u)[1], 0)

        @pl.when(f == 0)
        def _():
            lax.fori_loop(0, nsub, lambda s, c: (in_copy(s).start(), c)[1], 0)
            sweep(True, False)

        @pl.when((f > 0) & (f < nf - 1))
        def _():
            sweep(False, False)

        @pl.when(f == nf - 1)
        def _():
            sweep(False, True)
            lax.fori_loop(0, nbig, lambda c, u: (out_copy(c * BIG_ROWS, BIG_ROWS).wait(), u)[1], 0)
            lax.fori_loop(nbig * per_big, nsub, lambda s, u: (out_copy(s * SUB_ROWS, SUB_ROWS).wait(), u)[1], 0)


def _moe_experts(items, xs, w1, b1, w2, b2, n_slots, rmax, tf):
    item_e, item_start, item_nsub, n_items = items
    ni = item_e.shape[0]
    nf = D_FF // tf
    half = D_MODEL // 2

    def fidx(it, f, n):
        return jnp.where(it < n[0], f, nf - 1)

    grid_spec = pltpu.PrefetchScalarGridSpec(
        num_scalar_prefetch=4,
        grid=(ni, nf),
        in_specs=[
            pl.BlockSpec(memory_space=pl.ANY),
            pl.BlockSpec((1, D_MODEL, tf), lambda it, f, e, s, ns, n: (e[it], 0, fidx(it, f, n))),
            pl.BlockSpec((1, D_MODEL, tf), lambda it, f, e, s, ns, n: (e[it], 0, nf + fidx(it, f, n))),
            pl.BlockSpec((1, tf, D_MODEL), lambda it, f, e, s, ns, n: (e[it], fidx(it, f, n), 0)),
            pl.BlockSpec((1, 1, tf), lambda it, f, e, s, ns, n: (e[it], 0, fidx(it, f, n))),
            pl.BlockSpec((1, 1, tf), lambda it, f, e, s, ns, n: (e[it], 0, nf + fidx(it, f, n))),
            pl.BlockSpec((1, 1, D_MODEL), lambda it, f, e, s, ns, n: (e[it], 0, 0)),
        ],
        out_specs=pl.BlockSpec(memory_space=pl.ANY),
        scratch_shapes=[
            pltpu.VMEM((rmax, half), U32),
            pltpu.VMEM((rmax, half), BF16),
            pltpu.VMEM((rmax, half), BF16),
            pltpu.VMEM((rmax, D_MODEL), F32),
            pltpu.SemaphoreType.DMA((rmax // SUB_ROWS,)), pltpu.SemaphoreType.DMA(()),
        ],
    )
    return pl.pallas_call(
        functools.partial(_moe_kernel, nf=nf),
        grid_spec=grid_spec,
        out_shape=jax.ShapeDtypeStruct((n_slots, D_MODEL), F32),
        compiler_params=_cparams(("arbitrary", "arbitrary")),
        name="moe_experts",
    )(item_e, item_start, item_nsub, n_items, xs, w1, w1, w2,
      b1.reshape(N_EXPERTS, 1, 2 * D_FF), b1.reshape(N_EXPERTS, 1, 2 * D_FF), b2.reshape(N_EXPERTS, 1, D_MODEL))


def _combine_kernel(pos_ref, ys_hbm, h1_hbm, gate_hbm, y_ref, ysamp_ref, ybuf, hbuf, gbuf, sem, hsem,
                    *, tm, tiles_per_batch, t_pad, n_dec, samp_row0):
    i = pl.program_id(0)
    n_tiles = pl.num_programs(0)

    def gather_and_mix(row0, n):
        hcp = pltpu.make_async_copy(h1_hbm.at[pl.ds(row0, n)], hbuf.at[pl.ds(0, n)], hsem)
        gcp = pltpu.make_async_copy(gate_hbm.at[pl.ds(row0, n)], gbuf.at[pl.ds(0, n)], hsem)
        hcp.start()
        gcp.start()

        def start(t, c):
            for j in range(TOP_K):
                src = pos_ref[(row0 + t) * TOP_K + j]
                pltpu.make_async_copy(ys_hbm.at[pl.ds(src, 1)], ybuf.at[j, pl.ds(t, 1)], sem).start()
            return c

        def wait(t, c):
            for j in range(TOP_K):
                pltpu.make_async_copy(ys_hbm.at[pl.ds(0, 1)], ybuf.at[0, pl.ds(0, 1)], sem).wait()
            return c

        lax.fori_loop(0, n, start, 0)
        hcp.wait()
        gcp.wait()
        lax.fori_loop(0, n, wait, 0)
        g = gbuf[0:n, :]
        out = hbuf[0:n, :]
        for j in range(TOP_K):
            out = out + g[:, j:j + 1] * ybuf[j, 0:n, :]
        return out

    row0 = (i // tiles_per_batch) * t_pad + N_META + (i % tiles_per_batch) * tm
    y_ref[0] = gather_and_mix(pl.multiple_of(row0, 8), tm)

    @pl.when(i == n_tiles - 1)
    def _():
        ysamp_ref[...] = gather_and_mix(samp_row0, n_dec)


def _combine(pos_flat, ys, h1, gate, n_batch, seq, t_pad, n_dec, samp_row0, tm):
    tpb = seq // tm
    grid_spec = pltpu.PrefetchScalarGridSpec(
        num_scalar_prefetch=1,
        grid=(n_batch * tpb,),
        in_specs=[pl.BlockSpec(memory_space=pl.ANY)] * 3,
        out_specs=[pl.BlockSpec((1, tm, D_MODEL), lambda i, p: (i // tpb, i % tpb, 0)),
                   pl.BlockSpec((n_dec, D_MODEL), lambda i, p: (0, 0))],
        scratch_shapes=[
            pltpu.VMEM((TOP_K, tm, D_MODEL), F32),
            pltpu.VMEM((tm, D_MODEL), F32),
            pltpu.VMEM((tm, LANES), F32),
            pltpu.SemaphoreType.DMA(()), pltpu.SemaphoreType.DMA(()),
        ],
    )
    return pl.pallas_call(
        functools.partial(_combine_kernel, tm=tm, tiles_per_batch=tpb, t_pad=t_pad, n_dec=n_dec,
                          samp_row0=samp_row0),
        grid_spec=grid_spec,
        out_shape=[jax.ShapeDtypeStruct((n_batch, seq, D_MODEL), F32),
                   jax.ShapeDtypeStruct((n_dec, D_MODEL), F32)],
        compiler_params=_cparams(("arbitrary",)),
        name="moe_combine",
    )(pos_flat, ys, h1, gate)


def _t5_bucket(rel):
    n = jnp.maximum(rel, 0)
    max_exact = N_BUCKETS // 2
    nf = jnp.maximum(n, 1).astype(F32)
    large = max_exact + (jnp.log(nf / max_exact) / math.log(MAX_DISTANCE / max_exact)
                         * (N_BUCKETS - max_exact)).astype(I32)
    large = jnp.minimum(large, N_BUCKETS - 1)
    return jnp.where(n < max_exact, n, large)


def _bias_tables(rel_bias, tq):
    shifted = (rel_bias - rel_bias[N_BUCKETS - 1][None]) * LOG2E

    def bias_of(rel, out):
        onehot = jax.nn.one_hot(_t5_bucket(rel), N_BUCKETS, dtype=F32)
        return jnp.einsum("...b,bh->" + out, onehot, shifted, precision=lax.Precision.HIGHEST)

    rel0 = jnp.arange(tq)[:, None] - jnp.arange(tq)[None, :]
    diag = jnp.where((rel0 >= 0)[None], bias_of(rel0, "h..."), NEG_INF)
    tiles = jnp.stack([diag, bias_of(tq + rel0, "h...")])
    rel_last = PAGE_SIZE - jnp.arange(PAGE_SIZE)
    pbias = jnp.broadcast_to(bias_of(rel_last, "...h")[:, :, None], (PAGE_SIZE, N_HEADS, 2 * LANES))
    sbias = jnp.broadcast_to(bias_of(jnp.zeros((), I32), "...h")[:, None], (N_HEADS, 2 * LANES))
    return tiles, pbias, sbias


def _routing_tables(counts, topi, rank, valid, n_trash_rows, rmax, n_items_max, real_slots):
    padded = (counts + SUB_ROWS - 1) // SUB_ROWS * SUB_ROWS
    ends = jnp.cumsum(padded)
    off = ends - padded
    pos = off[topi] + rank
    trash_row = jnp.cumsum(jnp.logical_not(valid).astype(I32)) - 1
    trash = real_slots + trash_row[:, None] * TOP_K + jnp.arange(TOP_K, dtype=I32)[None]
    pos_scatter = jnp.where(valid[:, None], pos, trash).reshape(-1)
    pos_gather = jnp.where(valid[:, None], pos, 0).reshape(-1)
    zero_trash = real_slots + -(-(n_trash_rows * TOP_K) // SUB_ROWS) * SUB_ROWS
    zpos = jnp.where(counts > 0, ends - SUB_ROWS, zero_trash).astype(I32)
    per_e = (padded + rmax - 1) // rmax
    item_end = jnp.cumsum(per_e)
    n_items = item_end[-1]
    t = jnp.arange(n_items_max, dtype=I32)
    tt = jnp.minimum(t, n_items - 1)
    e_of = jnp.minimum(jnp.searchsorted(item_end, tt, side="right"), N_EXPERTS - 1).astype(I32)
    k = tt - (item_end - per_e)[e_of]
    start = off[e_of] + k * rmax
    nrows = jnp.minimum(rmax, padded[e_of] - k * rmax)
    nsub = jnp.where(t < n_items, nrows // SUB_ROWS, 0).astype(I32)
    items = (e_of, start.astype(I32), nsub, n_items.reshape(1).astype(I32))
    return pos_scatter.astype(I32), pos_gather.astype(I32), zpos, items, zero_trash + SUB_ROWS


def kernel(x_prompt, x_sample, cache_k, cache_v, state_conv, page_table, meta_tokens, rel_bias, norm1_g, w_in,
           conv_w, q_norm_g, k_norm_g, lambda_q1, lambda_k1, lambda_q2, lambda_k2, subln_g, w_branch_a,
           w_branch_b, w_out, norm2_g, w_router, b_router, w_mlp1, b_mlp1, w_mlp2, b_mlp2):
    n_batch, seq, _ = x_prompt.shape
    n_dec, t_dec, _ = x_sample.shape
    depth = cache_k.shape[0]
    assert depth == 1 and t_dec == 1 and n_dec == 8
    t_real = seq + N_META
    tq = 384
    t_pad = -(-(t_real + n_dec) // tq) * tq
    rows = n_batch * t_pad
    samp_row0 = t_real
    assert samp_row0 % 8 == 0 and t_pad % 24 == 0
    lam_init = 0.8 - 0.6 * math.exp(-0.3 * 0)

    h2d = _assemble(x_prompt, meta_tokens, x_sample.reshape(n_dec, D_MODEL), t_pad).reshape(rows, D_MODEL)

    z2d = _in_proj(h2d, norm1_g[0], w_in[0], tm=t_pad // 3, tn=512)
    z3 = z2d.reshape(n_batch, t_pad, D_IN)

    qg2 = jnp.tile(q_norm_g[0], 2).reshape(1, LANES)
    kg2 = jnp.tile(k_norm_g[0], 2).reshape(1, LANES)
    sg = subln_g[0].reshape(1, V_DIM)
    lams = [v[0].reshape(1, HEAD_DIM) for v in (lambda_q1, lambda_k1, lambda_q2, lambda_k2)]
    bias_tiles, pbias, sbias = _bias_tables(rel_bias, tq)

    ya, k_p, v_p, qs, kb, vb, ust = _prep(z3, conv_w[0], qg2, kg2, n_batch, t_pad, t_real, tm=tq)
    on = _prompt_attention(qs, kb, vb, bias_tiles, lams, sg, lam_init, t_pad, tq)

    row_block = samp_row0 // n_dec
    ya_s, u_s = _sample_conv(z2d, state_conv[0, :, 0], state_conv[0, :, 1], conv_w[0], row_block, n_dec)
    zs = z2d[samp_row0:samp_row0 + n_dec].reshape(n_dec, D_IN // LANES, LANES)
    o_s, k_s, v_s = _sample_attention(zs, cache_k, cache_v, page_table, pbias, sbias, qg2, kg2, lams, sg,
                                      lam_init, pages_per_step=4)

    ya = ya.at[0, samp_row0:samp_row0 + n_dec].set(ya_s.astype(BF16))
    on = on.at[0, :, samp_row0:samp_row0 + n_dec].set(o_s.transpose(1, 0, 2).astype(BF16))

    wr = jnp.pad(w_router[0], ((0, 0), (0, LANES - N_EXPERTS)))
    br = jnp.concatenate([b_router[0], jnp.full((LANES - N_EXPERTS,), NEG_INF, F32)]).reshape(1, LANES)
    h1, xp, topi, gate, rank, cnt = _merge(
        ya.reshape(rows, D_CONV), on, z2d, h2d,
        w_branch_a[0].astype(BF16), w_branch_b[0].astype(BF16), w_out[0].astype(BF16),
        norm2_g[0].reshape(1, D_MODEL), wr.astype(BF16), br, tm=tq // 2, t_pad=t_pad, t_real=t_real, n_dec=n_dec)

    rmax = 1536
    n_assign = (n_batch * t_real + n_dec) * TOP_K
    real_slots = -(-(n_assign + N_EXPERTS * (SUB_ROWS - 1)) // SUB_ROWS) * SUB_ROWS
    n_items_max = N_EXPERTS + real_slots // rmax
    pos_in_batch = jnp.arange(rows, dtype=I32) % t_pad
    valid = (pos_in_batch < t_real) | ((jnp.arange(rows) < t_pad) & (pos_in_batch < t_real + n_dec))
    pos_scatter, pos_gather, zpos, items, n_slots = _routing_tables(
        cnt[0, :N_EXPERTS].astype(I32), topi[:, :TOP_K], rank[:, :TOP_K], valid,
        rows - n_assign // TOP_K, rmax, n_items_max, real_slots)

    xs = _dispatch(pos_scatter, zpos, xp, n_slots, tm=tq)
    ys = _moe_experts(items, xs, w_mlp1[0], b_mlp1[0], w_mlp2[0], b_mlp2[0], n_slots, rmax, tf=256)
    y_prompt, y_samp = _combine(pos_gather, ys, h1, gate, n_batch, seq, t_pad, n_dec, samp_row0, tm=512)

    return (y_prompt,
            y_samp.reshape(n_dec, 1, D_MODEL),
            k_p.reshape(1, n_batch, t_real, N_HEADS, 2 * HEAD_DIM),
            v_p.reshape(1, n_batch, t_real, N_HEADS, V_DIM),
            ust[:, 6:8].reshape(1, n_batch, CONV_K - 1, D_CONV),
            k_s.reshape(1, n_dec, 1, N_HEADS, 2 * HEAD_DIM),
            v_s.reshape(1, n_dec, 1, N_HEADS, V_DIM),
            jnp.stack([state_conv[0, :, 1], u_s], axis=1).reshape(1, n_dec, CONV_K - 1, D_CONV))
```

```python
import functools
import math

import jax
import jax.numpy as jnp
import numpy as np
from jax import lax
from jax.experimental import pallas as pl
from jax.experimental.pallas import tpu as pltpu

D_MODEL = 2048
N_META = 16
D_CONV = 1024
CONV_K = 3
N_HEADS = 8
HEAD_DIM = 64
V_DIM = 128
D_QK = 1024
D_ATTN = 1024
D_IN = 10240
N_BUCKETS = 32
MAX_DISTANCE = 128
N_EXPERTS = 32
TOP_K = 4
D_FF = 2048
SWIGLU_LIMIT = 7.0
SWIGLU_ALPHA = 1.702
EPS = 1e-6
NEG_INF = -1e30
PAGE_SIZE = 128

F32 = jnp.float32
BF16 = jnp.bfloat16
I32 = jnp.int32
U32 = jnp.uint32

LANES = 128
SUB_ROWS = 128
BIG_ROWS = 512
ATTN_ROW_CHUNK = 64
LOG2E = 1.4426950408889634
DECODE_KEY_CHUNK = 16
D_QK_OFF = 3 * D_CONV
VMEM_LIMIT = 56 * 1024 * 1024


def _cparams(sem, vmem=VMEM_LIMIT):
    return pltpu.CompilerParams(dimension_semantics=sem, vmem_limit_bytes=vmem)


def _in_proj_kernel(xp_hbm, meta_ref, xs_ref, g_ref, w_ref, z_ref, h_ref, xn_ref, sems,
                    *, tm, tiles_per_batch, seq, n_dec):
    i = pl.program_id(0)
    b = i // tiles_per_batch
    t_real = seq + N_META

    @pl.when(pl.program_id(1) == 0)
    def _():
        for t in range(tiles_per_batch):
            @pl.when(i % tiles_per_batch == t)
            def _(t=t):
                lo = t * tm
                p0, p1 = max(lo, N_META), min(lo + tm, t_real)
                n_chunks = 4
                step = -(-(p1 - p0) // (8 * n_chunks)) * 8
                copies = []
                for c in range(n_chunks):
                    r0, r1 = p0 + c * step, min(p0 + (c + 1) * step, p1)
                    copies.append(pltpu.make_async_copy(
                        xp_hbm.at[b, pl.ds(r0 - N_META, r1 - r0)], h_ref.at[pl.ds(r0 - lo, r1 - r0)], sems.at[c]))
                for cp in copies:
                    cp.start()
                if lo < N_META:
                    h_ref[0:N_META, :] = meta_ref[...]
                if lo + tm > t_real:
                    h_ref[t_real - lo:tm, :] = jnp.zeros((lo + tm - t_real, D_MODEL), F32)

                    @pl.when(b == 0)
                    def _():
                        h_ref[t_real - lo:t_real - lo + n_dec, :] = xs_ref[...]
                for cp in copies:
                    cp.wait()

        x = h_ref[...]
        ms = jnp.mean(x * x, axis=-1, keepdims=True)
        xn_ref[...] = (x * lax.rsqrt(ms + EPS) * g_ref[...]).astype(BF16)

    z_ref[...] = jnp.dot(xn_ref[...], w_ref[...].astype(BF16), preferred_element_type=F32)


def _in_proj(x_prompt, meta_tokens, x_sample2d, g, w, t_pad, tm, tn):
    n_batch, seq, _ = x_prompt.shape
    n_dec = x_sample2d.shape[0]
    rows = n_batch * t_pad
    const = lambda shape: pl.BlockSpec(shape, lambda i, j: (0,) * len(shape))
    return pl.pallas_call(
        functools.partial(_in_proj_kernel, tm=tm, tiles_per_batch=t_pad // tm, seq=seq, n_dec=n_dec),
        grid=(rows // tm, D_IN // tn),
        in_specs=[
            pl.BlockSpec(memory_space=pl.ANY),
            const((N_META, D_MODEL)), const((n_dec, D_MODEL)), const((1, D_MODEL)),
            pl.BlockSpec((D_MODEL, tn), lambda i, j: (0, j)),
        ],
        out_specs=[pl.BlockSpec((tm, tn), lambda i, j: (i, j)),
                   pl.BlockSpec((tm, D_MODEL), lambda i, j: (i, 0))],
        out_shape=[jax.ShapeDtypeStruct((rows, D_IN), F32),
                   jax.ShapeDtypeStruct((rows, D_MODEL), F32)],
        scratch_shapes=[pltpu.VMEM((tm, D_MODEL), BF16), pltpu.SemaphoreType.DMA((4,))],
        compiler_params=_cparams(("parallel", "arbitrary")),
        name="in_proj",
    )(x_prompt, meta_tokens, x_sample2d, g.reshape(1, D_MODEL), w)


def _half_norm(x, g2, lo):
    t = x * x
    s_lo = jnp.sum(jnp.where(lo, t, 0.0), axis=-1, keepdims=True)
    s_hi = jnp.sum(jnp.where(lo, 0.0, t), axis=-1, keepdims=True)
    inv = jnp.where(lo, lax.rsqrt(s_lo * (1.0 / HEAD_DIM) + EPS), lax.rsqrt(s_hi * (1.0 / HEAD_DIM) + EPS))
    return x * inv * g2


def _prep_kernel(zb, zc, zx, zq, zk, zv, cw, qg, kg,
                 ya_ref, kout, vout, qs, kb, vb, ust, carry, *, tm, state_tile, state_row):
    i = pl.program_id(1)

    @pl.when(i == 0)
    def _():
        carry[...] = jnp.zeros_like(carry)

    u = zc[0] * zx[0]
    prev = carry[...]
    row = lax.broadcasted_iota(I32, (tm, 1), 0)
    u1 = jnp.where(row == 0, prev[7:8], pltpu.roll(u, 1, 0))
    u2 = jnp.where(row == 0, prev[6:7], jnp.where(row == 1, prev[7:8], pltpu.roll(u, 2, 0)))
    y = cw[0:1] * u2 + cw[1:2] * u1 + cw[2:3] * u
    ya_ref[0] = (zb[0] * y).astype(BF16)
    carry[...] = u[tm - 8:tm]

    @pl.when(i == state_tile)
    def _():
        ust[0] = u[state_row:state_row + 8]

    lo = lax.broadcasted_iota(I32, (1, LANES), 1) < HEAD_DIM
    for h in range(N_HEADS):
        sl = slice(h * LANES, (h + 1) * LANES)
        qn = _half_norm(zq[0, :, sl], qg[...], lo) * (LOG2E * HEAD_DIM ** -0.5)
        qs[0, 0, h] = jnp.where(lo, qn, 0.0).astype(BF16)
        qs[0, 1, h] = jnp.where(lo, 0.0, qn).astype(BF16)
        kn = _half_norm(zk[0, :, sl], kg[...], lo)
        kout[0, :, sl] = kn
        kb[0, h] = kn.astype(BF16)
        vb[0, h] = zv[0, :, sl].astype(BF16)
    vout[0] = zv[0]


def _prep(z3, conv_w, qg2, kg2, n_batch, t_pad, t_real, tm):
    nt = t_pad // tm
    state_tile = (t_real - 2) // tm
    state_row = ((t_real - 2) % tm) // 8 * 8
    sec = lambda s: pl.BlockSpec((1, tm, 1024), lambda b, i, s=s: (b, i, s))
    small = lambda shape: pl.BlockSpec(shape, lambda b, i: (0,) * len(shape))
    return pl.pallas_call(
        functools.partial(_prep_kernel, tm=tm, state_tile=state_tile, state_row=state_row),
        grid=(n_batch, nt),
        in_specs=[sec(0), sec(1), sec(2), sec(3), sec(4), sec(5),
                  small((CONV_K, D_CONV)), small((1, LANES)), small((1, LANES))],
        out_specs=[
            pl.BlockSpec((1, tm, D_CONV), lambda b, i: (b, i, 0)),
            pl.BlockSpec((1, tm, D_QK), lambda b, i: (b, i, 0)),
            pl.BlockSpec((1, tm, D_ATTN), lambda b, i: (b, i, 0)),
            pl.BlockSpec((1, 2, N_HEADS, tm, LANES), lambda b, i: (b, 0, 0, i, 0)),
            pl.BlockSpec((1, N_HEADS, tm, LANES), lambda b, i: (b, 0, i, 0)),
            pl.BlockSpec((1, N_HEADS, tm, LANES), lambda b, i: (b, 0, i, 0)),
            pl.BlockSpec((1, 8, D_CONV), lambda b, i: (b, 0, 0)),
        ],
        out_shape=[
            jax.ShapeDtypeStruct((n_batch, t_pad, D_CONV), BF16),
            jax.ShapeDtypeStruct((n_batch, t_real, D_QK), F32),
            jax.ShapeDtypeStruct((n_batch, t_real, D_ATTN), F32),
            jax.ShapeDtypeStruct((n_batch, 2, N_HEADS, t_pad, LANES), BF16),
            jax.ShapeDtypeStruct((n_batch, N_HEADS, t_pad, LANES), BF16),
            jax.ShapeDtypeStruct((n_batch, N_HEADS, t_pad, LANES), BF16),
            jax.ShapeDtypeStruct((n_batch, 8, D_CONV), F32),
        ],
        scratch_shapes=[pltpu.VMEM((8, D_CONV), F32)],
        compiler_params=_cparams(("parallel", "arbitrary")),
        name="prep",
    )(z3, z3, z3, z3, z3, z3, conv_w, qg2, kg2)


def _lambda_value(lq1, lk1, lq2, lk2, lam_init):
    a = jnp.sum(lq1[...] * lk1[...], axis=-1, keepdims=True)
    b = jnp.sum(lq2[...] * lk2[...], axis=-1, keepdims=True)
    return jnp.exp(a) - jnp.exp(b) + lam_init


def _attn_kernel(qi_tab, ki_tab, q_ref, k_ref, v_ref, bias_ref, lq1, lk1, lq2, lk2, sg,
                 o_ref, m_ref, l_ref, acc_ref, s_ref, p_ref, a_ref, *, tq, tk, lam_init):
    step = pl.program_id(1)
    qi = qi_tab[step]
    ki = ki_tab[step]
    rc = ATTN_ROW_CHUNK

    @pl.when(ki == 0)
    def _():
        m_ref[...] = jnp.full_like(m_ref, NEG_INF)
        l_ref[...] = jnp.zeros_like(l_ref)
        acc_ref[...] = jnp.zeros_like(acc_ref)

    def score(h, slot):
        q = q_ref[0, :, h].reshape(2 * tq, LANES)
        s_ref[slot] = lax.dot_general(q, k_ref[0, h], (((1,), (1,)), ((), ())), preferred_element_type=F32)

    def softmax_update(h, slot, near):
        for c in range(2 * tq // rc):
            rows = pl.ds(c * rc, rc)
            s = s_ref[slot, rows, :]
            if near:
                s = s + bias_ref[0, h, pl.ds((c % (tq // rc)) * rc, rc), :]
            m_prev = m_ref[h, rows, :]
            m_new = jnp.maximum(m_prev, jnp.max(s, axis=-1, keepdims=True))
            alpha = jnp.exp2(m_prev - m_new)
            p = jnp.exp2(s - jnp.concatenate([m_new] * (tk // LANES), axis=1))
            l_ref[h, rows, :] = alpha * l_ref[h, rows, :] + jnp.sum(p, axis=-1, keepdims=True)
            m_ref[h, rows, :] = m_new
            a_ref[slot, rows, :] = alpha
            p_ref[slot, rows, :] = p.astype(BF16)

    def weighted_values(h, slot):
        acc_ref[h] = a_ref[slot] * acc_ref[h] + jnp.dot(p_ref[slot], v_ref[0, h], preferred_element_type=F32)

    def head_pair(hp, near):
        for slot in range(2):
            score(2 * hp + slot, slot)
            softmax_update(2 * hp + slot, slot, near)
            weighted_values(2 * hp + slot, slot)

    @pl.when(ki >= qi - 1)
    def _():
        lax.fori_loop(0, N_HEADS // 2, lambda hp, c: (head_pair(hp, True), c)[1], 0)

    @pl.when(ki < qi - 1)
    def _():
        lax.fori_loop(0, N_HEADS // 2, lambda hp, c: (head_pair(hp, False), c)[1], 0)

    @pl.when(ki == qi)
    def _():
        lam = _lambda_value(lq1, lk1, lq2, lk2, lam_init)

        def fin(h, c):
            acc = acc_ref[h]
            l = l_ref[h]
            o = acc[:tq] / l[:tq] - lam * (acc[tq:] / l[tq:])
            on = o * lax.rsqrt(jnp.mean(o * o, axis=-1, keepdims=True) + EPS) * sg[...]
            o_ref[0, h] = (on * (1.0 - lam_init)).astype(BF16)
            return c

        lax.fori_loop(0, N_HEADS, fin, 0)


def _prompt_attention(qs, kb, vb, bias_tiles, lams, sg, lam_init, t_pad, tq):
    n_batch = qs.shape[0]
    nq = t_pad // tq
    pairs = [(qi, ki) for qi in range(nq) for ki in range(qi + 1)]
    qi_tab = jnp.asarray(np.array([p[0] for p in pairs], np.int32))
    ki_tab = jnp.asarray(np.array([p[1] for p in pairs], np.int32))
    vec = lambda n: pl.BlockSpec((1, n), lambda b, s, qt, kt: (0, 0))
    grid_spec = pltpu.PrefetchScalarGridSpec(
        num_scalar_prefetch=2,
        grid=(n_batch, len(pairs)),
        in_specs=[
            pl.BlockSpec((1, 2, N_HEADS, tq, LANES), lambda b, s, qt, kt: (b, 0, 0, qt[s], 0)),
            pl.BlockSpec((1, N_HEADS, tq, LANES), lambda b, s, qt, kt: (b, 0, kt[s], 0)),
            pl.BlockSpec((1, N_HEADS, tq, LANES), lambda b, s, qt, kt: (b, 0, kt[s], 0)),
            pl.BlockSpec((1, N_HEADS, tq, tq),
                         lambda b, s, qt, kt: (jnp.minimum(qt[s] - kt[s], 1), 0, 0, 0)),
            vec(HEAD_DIM), vec(HEAD_DIM), vec(HEAD_DIM), vec(HEAD_DIM), vec(V_DIM),
        ],
        out_specs=pl.BlockSpec((1, N_HEADS, tq, LANES), lambda b, s, qt, kt: (b, 0, qt[s], 0)),
        scratch_shapes=[
            pltpu.VMEM((N_HEADS, 2 * tq, LANES), F32),
            pltpu.VMEM((N_HEADS, 2 * tq, LANES), F32),
            pltpu.VMEM((N_HEADS, 2 * tq, LANES), F32),
            pltpu.VMEM((2, 2 * tq, tq), F32),
            pltpu.VMEM((2, 2 * tq, tq), BF16),
            pltpu.VMEM((2, 2 * tq, LANES), F32),
        ],
    )
    return pl.pallas_call(
        functools.partial(_attn_kernel, tq=tq, tk=tq, lam_init=lam_init),
        grid_spec=grid_spec,
        out_shape=jax.ShapeDtypeStruct((n_batch, N_HEADS, t_pad, LANES), BF16),
        compiler_params=_cparams(("parallel", "arbitrary")),
        name="prompt_attn",
    )(qi_tab, ki_tab, qs, kb, vb, bias_tiles, *lams, sg)


def _sample_conv_kernel(zb, zc, zx, s0, s1, cw, ya_ref, u_ref):
    u = zc[...] * zx[...]
    y = cw[0:1] * s0[...] + cw[1:2] * s1[...] + cw[2:3] * u
    ya_ref[...] = zb[...] * y
    u_ref[...] = u


def _sample_conv(z2d, s0, s1, conv_w, row_block, n):
    sec = lambda s: pl.BlockSpec((n, D_CONV), lambda i, s=s: (row_block, s))
    full = lambda r: pl.BlockSpec((r, D_CONV), lambda i: (0, 0))
    return pl.pallas_call(
        _sample_conv_kernel,
        grid=(1,),
        in_specs=[sec(0), sec(1), sec(2), full(n), full(n), full(CONV_K)],
        out_specs=[full(n), full(n)],
        out_shape=[jax.ShapeDtypeStruct((n, D_CONV), F32)] * 2,
        name="sample_conv",
    )(z2d, z2d, z2d, s0, s1, conv_w)


def _decode_kernel(pt_ref, zs, *rest, pages_per_step, n_steps, lam_init):
    kpages = rest[:pages_per_step]
    vpages = rest[pages_per_step:2 * pages_per_step]
    (pbias, sbias, rsum, qg, kg, lq1, lk1, lq2, lk2, sg,
     o_ref, kout, vout, q_sc, s_ref, m_ref, l_ref, acc_ref) = rest[2 * pages_per_step:]
    p = pl.program_id(1)
    kc = DECODE_KEY_CHUNK
    q_lo, k_lo, v_lo = D_QK_OFF // LANES, (D_QK_OFF + D_QK) // LANES, (D_QK_OFF + 2 * D_QK) // LANES
    lo = lax.broadcasted_iota(I32, (1, LANES), 1) < HEAD_DIM

    @pl.when(p == 0)
    def _():
        q_sc[...] = _half_norm(zs[0, q_lo:q_lo + N_HEADS], qg[...], lo) * (LOG2E * HEAD_DIM ** -0.5)
        kout[0] = _half_norm(zs[0, k_lo:k_lo + N_HEADS], kg[...], lo)
        vout[0] = zs[0, v_lo:v_lo + N_HEADS]
        m_ref[...] = jnp.full_like(m_ref, NEG_INF)
        l_ref[...] = jnp.zeros_like(l_ref)
        acc_ref[...] = jnp.zeros_like(acc_ref)

    q = q_sc[...]

    def scores(k):
        n = k.shape[0]
        t = (k * q[None]).reshape(n * N_HEADS, LANES).astype(BF16)
        return jnp.dot(t, rsum[...], preferred_element_type=F32).reshape(n, N_HEADS, 2 * LANES)

    m_prev = m_ref[...]
    m_new = m_prev
    for i in range(pages_per_step):
        s = scores(kpages[i][...])
        if i == pages_per_step - 1:
            s = s + jnp.where(p == n_steps - 1, pbias[...], 0.0)
        s_ref[pl.ds(i * PAGE_SIZE, PAGE_SIZE)] = s
        m_new = jnp.maximum(m_new, jnp.max(s, axis=0))
    alpha = jnp.exp2(m_prev - m_new)

    def accumulate(i):
        def body(c, carry):
            l, a0, a1 = carry
            pc = jnp.exp2(s_ref[pl.ds(i * PAGE_SIZE + c * kc, kc)] - m_new[None])
            vc = vpages[i][pl.ds(c * kc, kc)]
            return (l + jnp.sum(pc, axis=0),
                    a0 + jnp.sum(pc[:, :, :LANES] * vc, axis=0),
                    a1 + jnp.sum(pc[:, :, LANES:] * vc, axis=0))
        return body

    carry = (alpha * l_ref[...], alpha[:, :LANES] * acc_ref[0], alpha[:, LANES:] * acc_ref[1])
    for i in range(pages_per_step):
        carry = lax.fori_loop(0, PAGE_SIZE // kc, accumulate(i), carry)
    l, a0, a1 = carry
    m_ref[...] = m_new
    l_ref[...] = l
    acc_ref[0] = a0
    acc_ref[1] = a1

    @pl.when(p == n_steps - 1)
    def _():
        k_own = _half_norm(zs[0, k_lo:k_lo + N_HEADS], kg[...], lo)
        v_own = zs[0, v_lo:v_lo + N_HEADS]
        s_own = scores(k_own[None])[0] + sbias[...]
        m_fin = jnp.maximum(m_new, s_own)
        a_fin = jnp.exp2(m_new - m_fin)
        p_own = jnp.exp2(s_own - m_fin)
        l_fin = a_fin * l + p_own
        o0 = (a_fin[:, :LANES] * a0 + p_own[:, :LANES] * v_own) / l_fin[:, :LANES]
        o1 = (a_fin[:, LANES:] * a1 + p_own[:, LANES:] * v_own) / l_fin[:, LANES:]
        o = o0 - _lambda_value(lq1, lk1, lq2, lk2, lam_init) * o1
        on = o * lax.rsqrt(jnp.mean(o * o, axis=-1, keepdims=True) + EPS) * sg[...]
        o_ref[0] = on * (1.0 - lam_init)


def _sample_attention(zs, ck, cv, page_table, pbias, sbias, qg2, kg2, lams, sg, lam_init, pages_per_step):
    n_dec = zs.shape[0]
    n_pages = page_table.shape[1]
    n_steps = n_pages // pages_per_step
    pt = page_table.reshape(-1)
    page = lambda i: pl.BlockSpec(
        (None, None, PAGE_SIZE, N_HEADS, LANES),
        lambda r, p, pt, i=i: (0, pt[r * n_pages + p * pages_per_step + i], 0, 0, 0))
    vec = lambda shape: pl.BlockSpec(shape, lambda r, p, pt: (0,) * len(shape))
    out_row = pl.BlockSpec((1, N_HEADS, LANES), lambda r, p, pt: (r, 0, 0))
    rsum = (jnp.arange(LANES)[:, None] // HEAD_DIM == jnp.arange(2 * LANES)[None, :] // LANES).astype(BF16)
    grid_spec = pltpu.PrefetchScalarGridSpec(
        num_scalar_prefetch=1,
        grid=(n_dec, n_steps),
        in_specs=[pl.BlockSpec((1,) + zs.shape[1:], lambda r, p, pt: (r, 0, 0))]
        + [page(i) for i in range(pages_per_step)] + [page(i) for i in range(pages_per_step)]
        + [vec((PAGE_SIZE, N_HEADS, 2 * LANES)), vec((N_HEADS, 2 * LANES)), vec((LANES, 2 * LANES)),
           vec((1, LANES)), vec((1, LANES)),
           vec((1, HEAD_DIM)), vec((1, HEAD_DIM)), vec((1, HEAD_DIM)), vec((1, HEAD_DIM)), vec((1, V_DIM))],
        out_specs=[out_row, out_row, out_row],
        scratch_shapes=[
            pltpu.VMEM((N_HEADS, LANES), F32),
            pltpu.VMEM((pages_per_step * PAGE_SIZE, N_HEADS, 2 * LANES), F32),
            pltpu.VMEM((N_HEADS, 2 * LANES), F32),
            pltpu.VMEM((N_HEADS, 2 * LANES), F32),
            pltpu.VMEM((2, N_HEADS, LANES), F32),
        ],
    )
    return pl.pallas_call(
        functools.partial(_decode_kernel, pages_per_step=pages_per_step, n_steps=n_steps, lam_init=lam_init),
        grid_spec=grid_spec,
        out_shape=[jax.ShapeDtypeStruct((n_dec, N_HEADS, LANES), F32)] * 3,
        compiler_params=_cparams(("parallel", "arbitrary")),
        name="sample_attn",
    )(pt, zs, *([ck] * pages_per_step), *([cv] * pages_per_step), pbias, sbias, rsum, qg2, kg2, *lams, sg)


def _merge_kernel(ya, on, ga, gb, h, wa, wb, wo, n2g, wr, br,
                  h1_ref, xp_ref, topi_ref, gate_ref, rank_ref, cnt_ref, carry,
                  *, tm, tiles_per_batch, t_real, n_dec):
    i = pl.program_id(0)

    @pl.when(i == 0)
    def _():
        carry[...] = jnp.zeros_like(carry)

    y_a = jnp.dot(ya[...], wa[...], preferred_element_type=F32)
    o_cat = jnp.concatenate([on[0, hd] for hd in range(N_HEADS)], axis=1)
    y_b = jnp.dot(o_cat, wb[...], preferred_element_type=F32)
    mix = jax.nn.sigmoid(ga[...]) * y_a + jax.nn.sigmoid(gb[...]) * y_b
    h1 = h[...] + jnp.dot(mix.astype(BF16), wo[...], preferred_element_type=F32)
    h1_ref[...] = h1

    xn = h1 * lax.rsqrt(jnp.mean(h1 * h1, axis=-1, keepdims=True) + EPS) * n2g[...]
    xb = xn.astype(BF16)
    xf = xb.astype(F32)
    bits = pltpu.bitcast(xf, U32)
    half = D_MODEL // 2
    xp_ref[...] = bits[:, :half] | (bits[:, half:] >> 16)

    logits = jnp.dot(xb, wr[...], preferred_element_type=F32) + br[...]

    lane = lax.broadcasted_iota(I32, (tm, LANES), 1)
    work = logits
    vals, idxs, sels = [], [], []
    for _ in range(TOP_K):
        mx = jnp.max(work, axis=-1, keepdims=True)
        idx = jnp.min(jnp.where(work == mx, lane, LANES), axis=-1, keepdims=True)
        sel = lane == idx
        vals.append(mx)
        idxs.append(idx)
        sels.append(sel)
        work = jnp.where(sel, -jnp.inf, work)
    exps = [jnp.exp(v - vals[0]) for v in vals]
    denom = exps[0] + exps[1] + exps[2] + exps[3]

    pos_in_batch = (i % tiles_per_batch) * tm + lax.broadcasted_iota(I32, (tm, 1), 0)
    limit = jnp.where(i // tiles_per_batch == 0, t_real + n_dec, t_real)
    valid = pos_in_batch < limit

    onehot = jnp.zeros((tm, LANES), F32)
    for sel in sels:
        onehot = onehot + jnp.where(sel, 1.0, 0.0)
    onehot = jnp.where(valid, onehot, 0.0)
    rr = lax.broadcasted_iota(I32, (tm, tm), 0)
    cc = lax.broadcasted_iota(I32, (tm, tm), 1)
    lower = jnp.where(rr > cc, 1.0, 0.0).astype(BF16)
    before = jnp.dot(lower, onehot.astype(BF16), preferred_element_type=F32) + carry[...]

    topi = jnp.zeros((tm, LANES), I32)
    gate = jnp.zeros((tm, LANES), F32)
    rank = jnp.zeros((tm, LANES), I32)
    for j in range(TOP_K):
        rj = jnp.sum(jnp.where(sels[j], before, 0.0), axis=-1, keepdims=True).astype(I32)
        topi = jnp.where(lane == j, idxs[j], topi)
        gate = jnp.where(lane == j, exps[j] / denom, gate)
        rank = jnp.where(lane == j, rj, rank)
    topi_ref[...] = topi
    gate_ref[...] = gate
    rank_ref[...] = rank
    carry[...] = carry[...] + jnp.sum(onehot, axis=0, keepdims=True)
    cnt_ref[...] = jnp.broadcast_to(carry[...], cnt_ref.shape)


def _merge(ya2d, on, z2d, h2d, wa, wb, wo, n2g, wr, br, tm, t_pad, t_real, n_dec):
    rows = h2d.shape[0]
    tpb = t_pad // tm
    const = lambda shape: pl.BlockSpec(shape, lambda i: (0,) * len(shape), pipeline_mode=pl.Buffered(1))
    row = lambda w, dt=None: pl.BlockSpec((tm, w), lambda i: (i, 0))
    return pl.pallas_call(
        functools.partial(_merge_kernel, tm=tm, tiles_per_batch=tpb, t_real=t_real, n_dec=n_dec),
        grid=(rows // tm,),
        in_specs=[
            row(D_CONV),
            pl.BlockSpec((1, N_HEADS, tm, LANES), lambda i: (i // tpb, 0, i % tpb, 0)),
            pl.BlockSpec((tm, D_MODEL), lambda i: (i, 3)),
            pl.BlockSpec((tm, D_MODEL), lambda i: (i, 4)),
            row(D_MODEL),
            const((D_CONV, D_MODEL)), const((D_ATTN, D_MODEL)), const((D_MODEL, D_MODEL)),
            const((1, D_MODEL)), const((D_MODEL, LANES)), const((1, LANES)),
        ],
        out_specs=[row(D_MODEL), row(D_MODEL // 2), row(LANES), row(LANES), row(LANES),
                   pl.BlockSpec((8, LANES), lambda i: (0, 0))],
        out_shape=[
            jax.ShapeDtypeStruct((rows, D_MODEL), F32),
            jax.ShapeDtypeStruct((rows, D_MODEL // 2), U32),
            jax.ShapeDtypeStruct((rows, LANES), I32),
            jax.ShapeDtypeStruct((rows, LANES), F32),
            jax.ShapeDtypeStruct((rows, LANES), I32),
            jax.ShapeDtypeStruct((8, LANES), F32),
        ],
        scratch_shapes=[pltpu.VMEM((1, LANES), F32)],
        compiler_params=_cparams(("arbitrary",)),
        name="merge_route",
    )(ya2d, on, z2d, z2d, h2d, wa, wb, wo, n2g, wr, br)


def _dispatch_kernel(pos_ref, zpos_ref, x_ref, xs_hbm, zbuf, sem, zsem, *, tm):
    i = pl.program_id(0)

    @pl.when(i == 0)
    def _():
        zbuf[...] = jnp.zeros_like(zbuf)

        def zstart(e, c):
            pltpu.make_async_copy(zbuf, xs_hbm.at[pl.ds(pl.multiple_of(zpos_ref[e], SUB_ROWS), SUB_ROWS)],
                                  zsem).start()
            return c

        def zwait(e, c):
            pltpu.make_async_copy(zbuf, xs_hbm.at[pl.ds(0, SUB_ROWS)], zsem).wait()
            return c

        lax.fori_loop(0, N_EXPERTS, zstart, 0)
        lax.fori_loop(0, N_EXPERTS, zwait, 0)

    base = i * tm

    def start(t, c):
        for j in range(TOP_K):
            dst = pos_ref[(base + t) * TOP_K + j]
            pltpu.make_async_copy(x_ref.at[pl.ds(t, 1)], xs_hbm.at[pl.ds(dst, 1)], sem).start()
        return c

    def wait(t, c):
        for j in range(TOP_K):
            pltpu.make_async_copy(x_ref.at[pl.ds(0, 1)], xs_hbm.at[pl.ds(0, 1)], sem).wait()
        return c

    lax.fori_loop(0, tm, start, 0)
    lax.fori_loop(0, tm, wait, 0)


def _dispatch(pos_flat, zpos, xp, n_slots, tm):
    rows, width = xp.shape
    grid_spec = pltpu.PrefetchScalarGridSpec(
        num_scalar_prefetch=2,
        grid=(rows // tm,),
        in_specs=[pl.BlockSpec((tm, width), lambda i, p, z: (i, 0))],
        out_specs=pl.BlockSpec(memory_space=pl.ANY),
        scratch_shapes=[pltpu.VMEM((SUB_ROWS, width), U32),
                        pltpu.SemaphoreType.DMA(()), pltpu.SemaphoreType.DMA(())],
    )
    return pl.pallas_call(
        functools.partial(_dispatch_kernel, tm=tm),
        grid_spec=grid_spec,
        out_shape=jax.ShapeDtypeStruct((n_slots, width), U32),
        compiler_params=_cparams(("arbitrary",)),
        name="moe_dispatch",
    )(pos_flat, zpos, xp)


def _moe_kernel(item_e, item_start, item_nsub, n_items,
                xs_hbm, w1g, w1u, w2, b1g, b1u, b2,
                ys_hbm, xraw, xa, xb, yacc, sem_in, sem_out, *, nf):
    it = pl.program_id(0)
    f = pl.program_id(1)
    half = D_MODEL // 2

    @pl.when(it < n_items[0])
    def _():
        start = pl.multiple_of(item_start[it], SUB_ROWS)
        nsub = item_nsub[it]

        def in_copy(s):
            r0 = pl.multiple_of(s * SUB_ROWS, SUB_ROWS)
            return pltpu.make_async_copy(xs_hbm.at[pl.ds(start + r0, SUB_ROWS)],
                                         xraw.at[pl.ds(r0, SUB_ROWS)], sem_in.at[s])

        def out_copy(r0, m):
            rows = pl.ds(pl.multiple_of(r0, SUB_ROWS), m)
            return pltpu.make_async_copy(yacc.at[rows], ys_hbm.at[pl.ds(start + r0, m)], sem_out)

        def mlp_rows(r0, m, first, last):
            r0 = pl.multiple_of(r0, SUB_ROWS)
            rows = pl.ds(r0, m)
            if first:
                for k in range(m // SUB_ROWS):
                    sub = pl.ds(r0 + k * SUB_ROWS, SUB_ROWS)
                    in_copy(r0 // SUB_ROWS + k).wait()
                    w = xraw[sub, :]
                    xa[sub, :] = pltpu.bitcast(w & jnp.uint32(0xFFFF0000), F32).astype(BF16)
                    xb[sub, :] = pltpu.bitcast(w << 16, F32).astype(BF16)
            wg = w1g[0].astype(BF16)
            wu = w1u[0].astype(BF16)
            wd = w2[0].astype(BF16)
            a = xa[rows, :]
            b = xb[rows, :]
            hg = (jnp.dot(a, wg[:half], preferred_element_type=F32)
                  + jnp.dot(b, wg[half:], preferred_element_type=F32) + b1g[0])
            hu = (jnp.dot(a, wu[:half], preferred_element_type=F32)
                  + jnp.dot(b, wu[half:], preferred_element_type=F32) + b1u[0])
            g = jnp.minimum(hg, SWIGLU_LIMIT)
            up = jnp.clip(hu, -SWIGLU_LIMIT, SWIGLU_LIMIT)
            act = g * jax.nn.sigmoid(SWIGLU_ALPHA * g) * (up + 1.0)
            y = jnp.dot(act.astype(BF16), wd, preferred_element_type=F32)
            yacc[rows, :] = y + (b2[0] if first else yacc[rows, :])
            if last:
                out_copy(r0, m).start()

        per_big = BIG_ROWS // SUB_ROWS
        nbig = nsub // per_big

        def sweep(first, last):
            lax.fori_loop(0, nbig, lambda c, u: (mlp_rows(c * BIG_ROWS, BIG_ROWS, first, last), u)[1], 0)
            lax.fori_loop(nbig * per_big, nsub,
                          lambda s, u: (mlp_rows(s * SUB_ROWS, SUB_ROWS, first, last), u)[1], 0)

        @pl.when(f == 0)
        def _():
            lax.fori_loop(0, nsub, lambda s, c: (in_copy(s).start(), c)[1], 0)
            sweep(True, False)

        @pl.when((f > 0) & (f < nf - 1))
        def _():
            sweep(False, False)

        @pl.when(f == nf - 1)
        def _():
            sweep(False, True)
            lax.fori_loop(0, nbig, lambda c, u: (out_copy(c * BIG_ROWS, BIG_ROWS).wait(), u)[1], 0)
            lax.fori_loop(nbig * per_big, nsub, lambda s, u: (out_copy(s * SUB_ROWS, SUB_ROWS).wait(), u)[1], 0)


def _moe_experts(items, xs, w1, b1, w2, b2, n_slots, rmax, tf):
    item_e, item_start, item_nsub, n_items = items
    ni = item_e.shape[0]
    nf = D_FF // tf
    half = D_MODEL // 2

    def fidx(it, f, n):
        return jnp.where(it < n[0], f, nf - 1)

    grid_spec = pltpu.PrefetchScalarGridSpec(
        num_scalar_prefetch=4,
        grid=(ni, nf),
        in_specs=[
            pl.BlockSpec(memory_space=pl.ANY),
            pl.BlockSpec((1, D_MODEL, tf), lambda it, f, e, s, ns, n: (e[it], 0, fidx(it, f, n))),
            pl.BlockSpec((1, D_MODEL, tf), lambda it, f, e, s, ns, n: (e[it], 0, nf + fidx(it, f, n))),
            pl.BlockSpec((1, tf, D_MODEL), lambda it, f, e, s, ns, n: (e[it], fidx(it, f, n), 0)),
            pl.BlockSpec((1, 1, tf), lambda it, f, e, s, ns, n: (e[it], 0, fidx(it, f, n))),
            pl.BlockSpec((1, 1, tf), lambda it, f, e, s, ns, n: (e[it], 0, nf + fidx(it, f, n))),
            pl.BlockSpec((1, 1, D_MODEL), lambda it, f, e, s, ns, n: (e[it], 0, 0)),
        ],
        out_specs=pl.BlockSpec(memory_space=pl.ANY),
        scratch_shapes=[
            pltpu.VMEM((rmax, half), U32),
            pltpu.VMEM((rmax, half), BF16),
            pltpu.VMEM((rmax, half), BF16),
            pltpu.VMEM((rmax, D_MODEL), F32),
            pltpu.SemaphoreType.DMA((rmax // SUB_ROWS,)), pltpu.SemaphoreType.DMA(()),
        ],
    )
    return pl.pallas_call(
        functools.partial(_moe_kernel, nf=nf),
        grid_spec=grid_spec,
        out_shape=jax.ShapeDtypeStruct((n_slots, D_MODEL), F32),
        compiler_params=_cparams(("arbitrary", "arbitrary")),
        name="moe_experts",
    )(item_e, item_start, item_nsub, n_items, xs, w1, w1, w2,
      b1.reshape(N_EXPERTS, 1, 2 * D_FF), b1.reshape(N_EXPERTS, 1, 2 * D_FF), b2.reshape(N_EXPERTS, 1, D_MODEL))


def _combine_kernel(pos_ref, ys_hbm, h1_hbm, gate_hbm, y_ref, ysamp_ref, ybuf, hbuf, gbuf, sem, hsem,
                    *, tm, tiles_per_batch, t_pad, n_dec, samp_row0):
    i = pl.program_id(0)
    n_tiles = pl.num_programs(0)

    def gather_and_mix(row0, n):
        hcp = pltpu.make_async_copy(h1_hbm.at[pl.ds(row0, n)], hbuf.at[pl.ds(0, n)], hsem)
        gcp = pltpu.make_async_copy(gate_hbm.at[pl.ds(row0, n)], gbuf.at[pl.ds(0, n)], hsem)
        hcp.start()
        gcp.start()

        def start(t, c):
            for j in range(TOP_K):
                src = pos_ref[(row0 + t) * TOP_K + j]
                pltpu.make_async_copy(ys_hbm.at[pl.ds(src, 1)], ybuf.at[j, pl.ds(t, 1)], sem).start()
            return c

        def wait(t, c):
            for j in range(TOP_K):
                pltpu.make_async_copy(ys_hbm.at[pl.ds(0, 1)], ybuf.at[0, pl.ds(0, 1)], sem).wait()
            return c

        lax.fori_loop(0, n, start, 0)
        hcp.wait()
        gcp.wait()
        lax.fori_loop(0, n, wait, 0)
        g = gbuf[0:n, :]
        out = hbuf[0:n, :]
        for j in range(TOP_K):
            out = out + g[:, j:j + 1] * ybuf[j, 0:n, :]
        return out

    row0 = (i // tiles_per_batch) * t_pad + N_META + (i % tiles_per_batch) * tm
    y_ref[0] = gather_and_mix(pl.multiple_of(row0, 8), tm)

    @pl.when(i == n_tiles - 1)
    def _():
        ysamp_ref[...] = gather_and_mix(samp_row0, n_dec)


def _combine(pos_flat, ys, h1, gate, n_batch, seq, t_pad, n_dec, samp_row0, tm):
    tpb = seq // tm
    grid_spec = pltpu.PrefetchScalarGridSpec(
        num_scalar_prefetch=1,
        grid=(n_batch * tpb,),
        in_specs=[pl.BlockSpec(memory_space=pl.ANY)] * 3,
        out_specs=[pl.BlockSpec((1, tm, D_MODEL), lambda i, p: (i // tpb, i % tpb, 0)),
                   pl.BlockSpec((n_dec, D_MODEL), lambda i, p: (0, 0))],
        scratch_shapes=[
            pltpu.VMEM((TOP_K, tm, D_MODEL), F32),
            pltpu.VMEM((tm, D_MODEL), F32),
            pltpu.VMEM((tm, LANES), F32),
            pltpu.SemaphoreType.DMA(()), pltpu.SemaphoreType.DMA(()),
        ],
    )
    return pl.pallas_call(
        functools.partial(_combine_kernel, tm=tm, tiles_per_batch=tpb, t_pad=t_pad, n_dec=n_dec,
                          samp_row0=samp_row0),
        grid_spec=grid_spec,
        out_shape=[jax.ShapeDtypeStruct((n_batch, seq, D_MODEL), F32),
                   jax.ShapeDtypeStruct((n_dec, D_MODEL), F32)],
        compiler_params=_cparams(("arbitrary",)),
        name="moe_combine",
    )(pos_flat, ys, h1, gate)


def _t5_bucket(rel):
    n = jnp.maximum(rel, 0)
    max_exact = N_BUCKETS // 2
    nf = jnp.maximum(n, 1).astype(F32)
    large = max_exact + (jnp.log(nf / max_exact) / math.log(MAX_DISTANCE / max_exact)
                         * (N_BUCKETS - max_exact)).astype(I32)
    large = jnp.minimum(large, N_BUCKETS - 1)
    return jnp.where(n < max_exact, n, large)


def _bias_tables(rel_bias, tq):
    shifted = (rel_bias - rel_bias[N_BUCKETS - 1][None]) * LOG2E

    def bias_of(rel, out):
        onehot = jax.nn.one_hot(_t5_bucket(rel), N_BUCKETS, dtype=F32)
        return jnp.einsum("...b,bh->" + out, onehot, shifted, precision=lax.Precision.HIGHEST)

    rel0 = jnp.arange(tq)[:, None] - jnp.arange(tq)[None, :]
    diag = jnp.where((rel0 >= 0)[None], bias_of(rel0, "h..."), NEG_INF)
    tiles = jnp.stack([diag, bias_of(tq + rel0, "h...")])
    rel_last = PAGE_SIZE - jnp.arange(PAGE_SIZE)
    pbias = jnp.broadcast_to(bias_of(rel_last, "...h")[:, :, None], (PAGE_SIZE, N_HEADS, 2 * LANES))
    sbias = jnp.broadcast_to(bias_of(jnp.zeros((), I32), "...h")[:, None], (N_HEADS, 2 * LANES))
    return tiles, pbias, sbias


def _routing_tables(counts, topi, rank, valid, n_trash_rows, rmax, n_items_max, real_slots):
    padded = (counts + SUB_ROWS - 1) // SUB_ROWS * SUB_ROWS
    ends = jnp.cumsum(padded)
    off = ends - padded
    pos = off[topi] + rank
    trash_row = jnp.cumsum(jnp.logical_not(valid).astype(I32)) - 1
    trash = real_slots + trash_row[:, None] * TOP_K + jnp.arange(TOP_K, dtype=I32)[None]
    pos_scatter = jnp.where(valid[:, None], pos, trash).reshape(-1)
    pos_gather = jnp.where(valid[:, None], pos, 0).reshape(-1)
    zero_trash = real_slots + -(-(n_trash_rows * TOP_K) // SUB_ROWS) * SUB_ROWS
    zpos = jnp.where(counts > 0, ends - SUB_ROWS, zero_trash).astype(I32)
    per_e = (padded + rmax - 1) // rmax
    item_end = jnp.cumsum(per_e)
    n_items = item_end[-1]
    t = jnp.arange(n_items_max, dtype=I32)
    tt = jnp.minimum(t, n_items - 1)
    e_of = jnp.minimum(jnp.searchsorted(item_end, tt, side="right"), N_EXPERTS - 1).astype(I32)
    k = tt - (item_end - per_e)[e_of]
    start = off[e_of] + k * rmax
    nrows = jnp.minimum(rmax, padded[e_of] - k * rmax)
    nsub = jnp.where(t < n_items, nrows // SUB_ROWS, 0).astype(I32)
    items = (e_of, start.astype(I32), nsub, n_items.reshape(1).astype(I32))
    return pos_scatter.astype(I32), pos_gather.astype(I32), zpos, items, zero_trash + SUB_ROWS


def kernel(x_prompt, x_sample, cache_k, cache_v, state_conv, page_table, meta_tokens, rel_bias, norm1_g, w_in,
           conv_w, q_norm_g, k_norm_g, lambda_q1, lambda_k1, lambda_q2, lambda_k2, subln_g, w_branch_a,
           w_branch_b, w_out, norm2_g, w_router, b_router, w_mlp1, b_mlp1, w_mlp2, b_mlp2):
    n_batch, seq, _ = x_prompt.shape
    n_dec, t_dec, _ = x_sample.shape
    depth = cache_k.shape[0]
    assert depth == 1 and t_dec == 1 and n_dec == 8
    t_real = seq + N_META
    tq = 384
    t_pad = -(-(t_real + n_dec) // tq) * tq
    rows = n_batch * t_pad
    samp_row0 = t_real
    assert samp_row0 % 8 == 0 and t_pad % 24 == 0
    lam_init = 0.8 - 0.6 * math.exp(-0.3 * 0)

    z2d, h2d = _in_proj(x_prompt, meta_tokens, x_sample.reshape(n_dec, D_MODEL), norm1_g[0], w_in[0],
                        t_pad, tm=t_pad // 3, tn=512)
    z3 = z2d.reshape(n_batch, t_pad, D_IN)

    qg2 = jnp.tile(q_norm_g[0], 2).reshape(1, LANES)
    kg2 = jnp.tile(k_norm_g[0], 2).reshape(1, LANES)
    sg = subln_g[0].reshape(1, V_DIM)
    lams = [v[0].reshape(1, HEAD_DIM) for v in (lambda_q1, lambda_k1, lambda_q2, lambda_k2)]
    bias_tiles, pbias, sbias = _bias_tables(rel_bias, tq)

    ya, k_p, v_p, qs, kb, vb, ust = _prep(z3, conv_w[0], qg2, kg2, n_batch, t_pad, t_real, tm=tq)
    on = _prompt_attention(qs, kb, vb, bias_tiles, lams, sg, lam_init, t_pad, tq)

    row_block = samp_row0 // n_dec
    ya_s, u_s = _sample_conv(z2d, state_conv[0, :, 0], state_conv[0, :, 1], conv_w[0], row_block, n_dec)
    zs = z2d[samp_row0:samp_row0 + n_dec].reshape(n_dec, D_IN // LANES, LANES)
    o_s, k_s, v_s = _sample_attention(zs, cache_k, cache_v, page_table, pbias, sbias, qg2, kg2, lams, sg,
                                      lam_init, pages_per_step=4)

    ya = ya.at[0, samp_row0:samp_row0 + n_dec].set(ya_s.astype(BF16))
    on = on.at[0, :, samp_row0:samp_row0 + n_dec].set(o_s.transpose(1, 0, 2).astype(BF16))

    wr = jnp.pad(w_router[0], ((0, 0), (0, LANES - N_EXPERTS)))
    br = jnp.concatenate([b_router[0], jnp.full((LANES - N_EXPERTS,), NEG_INF, F32)]).reshape(1, LANES)
    h1, xp, topi, gate, rank, cnt = _merge(
        ya.reshape(rows, D_CONV), on, z2d, h2d,
        w_branch_a[0].astype(BF16), w_branch_b[0].astype(BF16), w_out[0].astype(BF16),
        norm2_g[0].reshape(1, D_MODEL), wr.astype(BF16), br, tm=tq // 2, t_pad=t_pad, t_real=t_real, n_dec=n_dec)

    rmax = 1536
    n_assign = (n_batch * t_real + n_dec) * TOP_K
    real_slots = -(-(n_assign + N_EXPERTS * (SUB_ROWS - 1)) // SUB_ROWS) * SUB_ROWS
    n_items_max = N_EXPERTS + real_slots // rmax
    pos_in_batch = jnp.arange(rows, dtype=I32) % t_pad
    valid = (pos_in_batch < t_real) | ((jnp.arange(rows) < t_pad) & (pos_in_batch < t_real + n_dec))
    pos_scatter, pos_gather, zpos, items, n_slots = _routing_tables(
        cnt[0, :N_EXPERTS].astype(I32), topi[:, :TOP_K], rank[:, :TOP_K], valid,
        rows - n_assign // TOP_K, rmax, n_items_max, real_slots)

    xs = _dispatch(pos_scatter, zpos, xp, n_slots, tm=tq)
    ys = _moe_experts(items, xs, w_mlp1[0], b_mlp1[0], w_mlp2[0], b_mlp2[0], n_slots, rmax, tf=256)
    y_prompt, y_samp = _combine(pos_gather, ys, h1, gate, n_batch, seq, t_pad, n_dec, samp_row0, tm=512)

    return (y_prompt,
            y_samp.reshape(n_dec, 1, D_MODEL),
            k_p.reshape(1, n_batch, t_real, N_HEADS, 2 * HEAD_DIM),
            v_p.reshape(1, n_batch, t_real, N_HEADS, V_DIM),
            ust[:, 6:8].reshape(1, n_batch, CONV_K - 1, D_CONV),
            k_s.reshape(1, n_dec, 1, N_HEADS, 2 * HEAD_DIM),
            v_s.reshape(1, n_dec, 1, N_HEADS, V_DIM),
            jnp.stack([state_conv[0, :, 1], u_s], axis=1).reshape(1, n_dec, CONV_K - 1, D_CONV))
```

```python
import functools
import math

import jax
import jax.numpy as jnp
import numpy as np
from jax import lax
from jax.experimental import pallas as pl
from jax.experimental.pallas import tpu as pltpu

D_MODEL = 2048
N_META = 16
D_CONV = 1024
CONV_K = 3
N_HEADS = 8
HEAD_DIM = 64
V_DIM = 128
D_QK = 1024
D_ATTN = 1024
D_IN = 10240
N_BUCKETS = 32
MAX_DISTANCE = 128
N_EXPERTS = 32
TOP_K = 4
D_FF = 2048
SWIGLU_LIMIT = 7.0
SWIGLU_ALPHA = 1.702
EPS = 1e-6
NEG_INF = -1e30
PAGE_SIZE = 128

F32 = jnp.float32
BF16 = jnp.bfloat16
I32 = jnp.int32
U32 = jnp.uint32

LANES = 128
SUB_ROWS = 128
BIG_ROWS = 1024
MID_ROWS = 512
ATTN_ROW_CHUNK = 64
LOG2E = 1.4426950408889634
DMA_LOOP_UNROLL = 4
DECODE_KEY_CHUNK = 16
D_QK_OFF = 3 * D_CONV
VMEM_LIMIT = 56 * 1024 * 1024


def _cparams(sem, vmem=VMEM_LIMIT):
    return pltpu.CompilerParams(dimension_semantics=sem, vmem_limit_bytes=vmem)


def _in_proj_kernel(xp_hbm, meta_ref, xs_ref, g_ref, w_ref, z_ref, h_ref, xn_ref, sems,
                    *, tm, tiles_per_batch, seq, n_dec):
    i = pl.program_id(0)
    b = i // tiles_per_batch
    t_real = seq + N_META

    @pl.when(pl.program_id(1) == 0)
    def _():
        for t in range(tiles_per_batch):
            @pl.when(i % tiles_per_batch == t)
            def _(t=t):
                lo = t * tm
                p0, p1 = max(lo, N_META), min(lo + tm, t_real)
                n_chunks = 4
                step = -(-(p1 - p0) // (8 * n_chunks)) * 8
                copies = []
                for c in range(n_chunks):
                    r0, r1 = p0 + c * step, min(p0 + (c + 1) * step, p1)
                    copies.append(pltpu.make_async_copy(
                        xp_hbm.at[b, pl.ds(r0 - N_META, r1 - r0)], h_ref.at[pl.ds(r0 - lo, r1 - r0)], sems.at[c]))
                for cp in copies:
                    cp.start()
                if lo < N_META:
                    h_ref[0:N_META, :] = meta_ref[...]
                if lo + tm > t_real:
                    h_ref[t_real - lo:tm, :] = jnp.zeros((lo + tm - t_real, D_MODEL), F32)

                    @pl.when(b == 0)
                    def _():
                        h_ref[t_real - lo:t_real - lo + n_dec, :] = xs_ref[...]
                for cp in copies:
                    cp.wait()

        x = h_ref[...]
        ms = jnp.mean(x * x, axis=-1, keepdims=True)
        xn_ref[...] = (x * lax.rsqrt(ms + EPS) * g_ref[...]).astype(BF16)

    z_ref[...] = jnp.dot(xn_ref[...], w_ref[...].astype(BF16), preferred_element_type=F32)


def _in_proj(x_prompt, meta_tokens, x_sample2d, g, w, t_pad, tm, tn):
    n_batch, seq, _ = x_prompt.shape
    n_dec = x_sample2d.shape[0]
    rows = n_batch * t_pad
    const = lambda shape: pl.BlockSpec(shape, lambda i, j: (0,) * len(shape))
    return pl.pallas_call(
        functools.partial(_in_proj_kernel, tm=tm, tiles_per_batch=t_pad // tm, seq=seq, n_dec=n_dec),
        grid=(rows // tm, D_IN // tn),
        in_specs=[
            pl.BlockSpec(memory_space=pl.ANY),
            const((N_META, D_MODEL)), const((n_dec, D_MODEL)), const((1, D_MODEL)),
            pl.BlockSpec((D_MODEL, tn), lambda i, j: (0, j)),
        ],
        out_specs=[pl.BlockSpec((tm, tn), lambda i, j: (i, j)),
                   pl.BlockSpec((tm, D_MODEL), lambda i, j: (i, 0))],
        out_shape=[jax.ShapeDtypeStruct((rows, D_IN), F32),
                   jax.ShapeDtypeStruct((rows, D_MODEL), F32)],
        scratch_shapes=[pltpu.VMEM((tm, D_MODEL), BF16), pltpu.SemaphoreType.DMA((4,))],
        compiler_params=_cparams(("parallel", "arbitrary")),
        name="in_proj",
    )(x_prompt, meta_tokens, x_sample2d, g.reshape(1, D_MODEL), w)


def _half_norm(x, g2, lo):
    t = x * x
    s_lo = jnp.sum(jnp.where(lo, t, 0.0), axis=-1, keepdims=True)
    s_hi = jnp.sum(jnp.where(lo, 0.0, t), axis=-1, keepdims=True)
    inv = jnp.where(lo, lax.rsqrt(s_lo * (1.0 / HEAD_DIM) + EPS), lax.rsqrt(s_hi * (1.0 / HEAD_DIM) + EPS))
    return x * inv * g2


def _prep_kernel(zb, zc, zx, zq, zk, zv, cw, qg, kg,
                 ya_ref, kout, vout, qs, kb, vb, ust, carry, *, tm, state_tile, state_row):
    i = pl.program_id(1)

    @pl.when(i == 0)
    def _():
        carry[...] = jnp.zeros_like(carry)

    u = zc[0] * zx[0]
    prev = carry[...]
    row = lax.broadcasted_iota(I32, (tm, 1), 0)
    u1 = jnp.where(row == 0, prev[7:8], pltpu.roll(u, 1, 0))
    u2 = jnp.where(row == 0, prev[6:7], jnp.where(row == 1, prev[7:8], pltpu.roll(u, 2, 0)))
    y = cw[0:1] * u2 + cw[1:2] * u1 + cw[2:3] * u
    ya_ref[0] = (zb[0] * y).astype(BF16)
    carry[...] = u[tm - 8:tm]

    @pl.when(i == state_tile)
    def _():
        ust[0] = u[state_row:state_row + 8]

    lo = lax.broadcasted_iota(I32, (1, LANES), 1) < HEAD_DIM
    for h in range(N_HEADS):
        sl = slice(h * LANES, (h + 1) * LANES)
        qn = _half_norm(zq[0, :, sl], qg[...], lo) * (LOG2E * HEAD_DIM ** -0.5)
        qs[0, 0, h] = jnp.where(lo, qn, 0.0).astype(BF16)
        qs[0, 1, h] = jnp.where(lo, 0.0, qn).astype(BF16)
        kn = _half_norm(zk[0, :, sl], kg[...], lo)
        kout[0, :, sl] = kn
        kb[0, h] = kn.astype(BF16)
        vb[0, h] = zv[0, :, sl].astype(BF16)
    vout[0] = zv[0]


def _prep(z3, conv_w, qg2, kg2, n_batch, t_pad, t_real, tm):
    nt = t_pad // tm
    state_tile = (t_real - 2) // tm
    state_row = ((t_real - 2) % tm) // 8 * 8
    sec = lambda s: pl.BlockSpec((1, tm, 1024), lambda b, i, s=s: (b, i, s))
    small = lambda shape: pl.BlockSpec(shape, lambda b, i: (0,) * len(shape))
    return pl.pallas_call(
        functools.partial(_prep_kernel, tm=tm, state_tile=state_tile, state_row=state_row),
        grid=(n_batch, nt),
        in_specs=[sec(0), sec(1), sec(2), sec(3), sec(4), sec(5),
                  small((CONV_K, D_CONV)), small((1, LANES)), small((1, LANES))],
        out_specs=[
            pl.BlockSpec((1, tm, D_CONV), lambda b, i: (b, i, 0)),
            pl.BlockSpec((1, tm, D_QK), lambda b, i: (b, i, 0)),
            pl.BlockSpec((1, tm, D_ATTN), lambda b, i: (b, i, 0)),
            pl.BlockSpec((1, 2, N_HEADS, tm, LANES), lambda b, i: (b, 0, 0, i, 0)),
            pl.BlockSpec((1, N_HEADS, tm, LANES), lambda b, i: (b, 0, i, 0)),
            pl.BlockSpec((1, N_HEADS, tm, LANES), lambda b, i: (b, 0, i, 0)),
            pl.BlockSpec((1, 8, D_CONV), lambda b, i: (b, 0, 0)),
        ],
        out_shape=[
            jax.ShapeDtypeStruct((n_batch, t_pad, D_CONV), BF16),
            jax.ShapeDtypeStruct((n_batch, t_real, D_QK), F32),
            jax.ShapeDtypeStruct((n_batch, t_real, D_ATTN), F32),
            jax.ShapeDtypeStruct((n_batch, 2, N_HEADS, t_pad, LANES), BF16),
            jax.ShapeDtypeStruct((n_batch, N_HEADS, t_pad, LANES), BF16),
            jax.ShapeDtypeStruct((n_batch, N_HEADS, t_pad, LANES), BF16),
            jax.ShapeDtypeStruct((n_batch, 8, D_CONV), F32),
        ],
        scratch_shapes=[pltpu.VMEM((8, D_CONV), F32)],
        compiler_params=_cparams(("parallel", "arbitrary")),
        name="prep",
    )(z3, z3, z3, z3, z3, z3, conv_w, qg2, kg2)


def _lambda_value(lq1, lk1, lq2, lk2, lam_init):
    a = jnp.sum(lq1[...] * lk1[...], axis=-1, keepdims=True)
    b = jnp.sum(lq2[...] * lk2[...], axis=-1, keepdims=True)
    return jnp.exp(a) - jnp.exp(b) + lam_init


def _attn_kernel(qi_tab, ki_tab, q_ref, k_ref, v_ref, bias_ref, lq1, lk1, lq2, lk2, sg,
                 o_ref, m_ref, l_ref, acc_ref, s_ref, p_ref, a_ref, *, tq, tk, lam_init):
    step = pl.program_id(1)
    qi = qi_tab[step]
    ki = ki_tab[step]
    rc = ATTN_ROW_CHUNK

    @pl.when(ki == 0)
    def _():
        m_ref[...] = jnp.full_like(m_ref, NEG_INF)
        l_ref[...] = jnp.zeros_like(l_ref)
        acc_ref[...] = jnp.zeros_like(acc_ref)

    def score(h, slot):
        q = q_ref[0, :, h].reshape(2 * tq, LANES)
        s_ref[slot] = lax.dot_general(q, k_ref[0, h], (((1,), (1,)), ((), ())), preferred_element_type=F32)

    def softmax_update(h, slot, near):
        for c in range(2 * tq // rc):
            rows = pl.ds(c * rc, rc)
            s = s_ref[slot, rows, :]
            if near:
                s = s + bias_ref[0, h, pl.ds((c % (tq // rc)) * rc, rc), :]
            m_prev = m_ref[h, rows, :]
            m_new = jnp.maximum(m_prev, jnp.max(s, axis=-1, keepdims=True))
            alpha = jnp.exp2(m_prev - m_new)
            p = jnp.exp2(s - jnp.concatenate([m_new] * (tk // LANES), axis=1))
            l_ref[h, rows, :] = alpha * l_ref[h, rows, :] + jnp.sum(p, axis=-1, keepdims=True)
            m_ref[h, rows, :] = m_new
            a_ref[slot, rows, :] = alpha
            p_ref[slot, rows, :] = p.astype(BF16)

    def weighted_values(h, slot):
        acc_ref[h] = a_ref[slot] * acc_ref[h] + jnp.dot(p_ref[slot], v_ref[0, h], preferred_element_type=F32)

    def head_pair(hp, near):
        for slot in range(2):
            score(2 * hp + slot, slot)
            softmax_update(2 * hp + slot, slot, near)
            weighted_values(2 * hp + slot, slot)

    @pl.when(ki >= qi - 1)
    def _():
        lax.fori_loop(0, N_HEADS // 2, lambda hp, c: (head_pair(hp, True), c)[1], 0)

    @pl.when(ki < qi - 1)
    def _():
        lax.fori_loop(0, N_HEADS // 2, lambda hp, c: (head_pair(hp, False), c)[1], 0)

    @pl.when(ki == qi)
    def _():
        lam = _lambda_value(lq1, lk1, lq2, lk2, lam_init)

        def fin(h, c):
            acc = acc_ref[h]
            l = l_ref[h]
            o = acc[:tq] / l[:tq] - lam * (acc[tq:] / l[tq:])
            on = o * lax.rsqrt(jnp.mean(o * o, axis=-1, keepdims=True) + EPS) * sg[...]
            o_ref[0, h] = (on * (1.0 - lam_init)).astype(BF16)
            return c

        lax.fori_loop(0, N_HEADS, fin, 0)


def _prompt_attention(qs, kb, vb, bias_tiles, lams, sg, lam_init, t_pad, tq):
    n_batch = qs.shape[0]
    nq = t_pad // tq
    pairs = [(qi, ki) for qi in range(nq) for ki in range(qi + 1)]
    qi_tab = jnp.asarray(np.array([p[0] for p in pairs], np.int32))
    ki_tab = jnp.asarray(np.array([p[1] for p in pairs], np.int32))
    vec = lambda n: pl.BlockSpec((1, n), lambda b, s, qt, kt: (0, 0))
    grid_spec = pltpu.PrefetchScalarGridSpec(
        num_scalar_prefetch=2,
        grid=(n_batch, len(pairs)),
        in_specs=[
            pl.BlockSpec((1, 2, N_HEADS, tq, LANES), lambda b, s, qt, kt: (b, 0, 0, qt[s], 0)),
            pl.BlockSpec((1, N_HEADS, tq, LANES), lambda b, s, qt, kt: (b, 0, kt[s], 0)),
            pl.BlockSpec((1, N_HEADS, tq, LANES), lambda b, s, qt, kt: (b, 0, kt[s], 0)),
            pl.BlockSpec((1, N_HEADS, tq, tq),
                         lambda b, s, qt, kt: (jnp.minimum(qt[s] - kt[s], 1), 0, 0, 0)),
            vec(HEAD_DIM), vec(HEAD_DIM), vec(HEAD_DIM), vec(HEAD_DIM), vec(V_DIM),
        ],
        out_specs=pl.BlockSpec((1, N_HEADS, tq, LANES), lambda b, s, qt, kt: (b, 0, qt[s], 0)),
        scratch_shapes=[
            pltpu.VMEM((N_HEADS, 2 * tq, LANES), F32),
            pltpu.VMEM((N_HEADS, 2 * tq, LANES), F32),
            pltpu.VMEM((N_HEADS, 2 * tq, LANES), F32),
            pltpu.VMEM((2, 2 * tq, tq), F32),
            pltpu.VMEM((2, 2 * tq, tq), BF16),
            pltpu.VMEM((2, 2 * tq, LANES), F32),
        ],
    )
    return pl.pallas_call(
        functools.partial(_attn_kernel, tq=tq, tk=tq, lam_init=lam_init),
        grid_spec=grid_spec,
        out_shape=jax.ShapeDtypeStruct((n_batch, N_HEADS, t_pad, LANES), BF16),
        compiler_params=_cparams(("parallel", "arbitrary")),
        name="prompt_attn",
    )(qi_tab, ki_tab, qs, kb, vb, bias_tiles, *lams, sg)


def _sample_conv_kernel(zb, zc, zx, s0, s1, cw, ya_ref, u_ref):
    u = zc[...] * zx[...]
    y = cw[0:1] * s0[...] + cw[1:2] * s1[...] + cw[2:3] * u
    ya_ref[...] = zb[...] * y
    u_ref[...] = u


def _sample_conv(z2d, s0, s1, conv_w, row_block, n):
    sec = lambda s: pl.BlockSpec((n, D_CONV), lambda i, s=s: (row_block, s))
    full = lambda r: pl.BlockSpec((r, D_CONV), lambda i: (0, 0))
    return pl.pallas_call(
        _sample_conv_kernel,
        grid=(1,),
        in_specs=[sec(0), sec(1), sec(2), full(n), full(n), full(CONV_K)],
        out_specs=[full(n), full(n)],
        out_shape=[jax.ShapeDtypeStruct((n, D_CONV), F32)] * 2,
        name="sample_conv",
    )(z2d, z2d, z2d, s0, s1, conv_w)


def _decode_kernel(pt_ref, zs, *rest, pages_per_step, n_steps, lam_init):
    kpages = rest[:pages_per_step]
    vpages = rest[pages_per_step:2 * pages_per_step]
    (pbias, pbias_max, sbias, rsum, qg, kg, lq1, lk1, lq2, lk2, sg,
     o_ref, kout, vout, q_sc, s_ref, m_ref, l_ref, acc_ref) = rest[2 * pages_per_step:]
    p = pl.program_id(1)
    kc = DECODE_KEY_CHUNK
    q_lo, k_lo, v_lo = D_QK_OFF // LANES, (D_QK_OFF + D_QK) // LANES, (D_QK_OFF + 2 * D_QK) // LANES
    lo = lax.broadcasted_iota(I32, (1, LANES), 1) < HEAD_DIM

    @pl.when(p == 0)
    def _():
        q_sc[...] = _half_norm(zs[0, q_lo:q_lo + N_HEADS], qg[...], lo) * (LOG2E * HEAD_DIM ** -0.5)
        kout[0] = _half_norm(zs[0, k_lo:k_lo + N_HEADS], kg[...], lo)
        vout[0] = zs[0, v_lo:v_lo + N_HEADS]
        m_ref[...] = jnp.full_like(m_ref, NEG_INF)
        l_ref[...] = jnp.zeros_like(l_ref)
        acc_ref[...] = jnp.zeros_like(acc_ref)

    q = q_sc[...]

    def scores(k):
        n = k.shape[0]
        t = (k * q[None]).reshape(n * N_HEADS, LANES).astype(BF16)
        return jnp.dot(t, rsum[...], preferred_element_type=F32).reshape(n, N_HEADS, 2 * LANES)

    m_prev = m_ref[...]
    m_new = m_prev
    for i in range(pages_per_step):
        s = scores(kpages[i][...])
        s_ref[pl.ds(i * PAGE_SIZE, PAGE_SIZE)] = s
        m_new = jnp.maximum(m_new, jnp.max(s, axis=0))

    m_new = m_new + jnp.where(p == n_steps - 1, pbias_max[...], 0.0)

    @pl.when(p == n_steps - 1)
    def _():
        last = pl.ds((pages_per_step - 1) * PAGE_SIZE, PAGE_SIZE)
        s_ref[last] = s_ref[last] + pbias[...]

    alpha = jnp.exp2(m_prev - m_new)

    def accumulate(i):
        def body(c, carry):
            l, a0, a1 = carry
            pc = jnp.exp2(s_ref[pl.ds(i * PAGE_SIZE + c * kc, kc)] - m_new[None])
            vc = vpages[i][pl.ds(c * kc, kc)]
            return (l + jnp.sum(pc, axis=0),
                    a0 + jnp.sum(pc[:, :, :LANES] * vc, axis=0),
                    a1 + jnp.sum(pc[:, :, LANES:] * vc, axis=0))
        return body

    carry = (alpha * l_ref[...], alpha[:, :LANES] * acc_ref[0], alpha[:, LANES:] * acc_ref[1])
    for i in range(pages_per_step):
        carry = lax.fori_loop(0, PAGE_SIZE // kc, accumulate(i), carry)
    l, a0, a1 = carry
    m_ref[...] = m_new
    l_ref[...] = l
    acc_ref[0] = a0
    acc_ref[1] = a1

    @pl.when(p == n_steps - 1)
    def _():
        k_own = _half_norm(zs[0, k_lo:k_lo + N_HEADS], kg[...], lo)
        v_own = zs[0, v_lo:v_lo + N_HEADS]
        s_own = scores(k_own[None])[0] + sbias[...]
        m_fin = jnp.maximum(m_new, s_own)
        a_fin = jnp.exp2(m_new - m_fin)
        p_own = jnp.exp2(s_own - m_fin)
        l_fin = a_fin * l + p_own
        o0 = (a_fin[:, :LANES] * a0 + p_own[:, :LANES] * v_own) / l_fin[:, :LANES]
        o1 = (a_fin[:, LANES:] * a1 + p_own[:, LANES:] * v_own) / l_fin[:, LANES:]
        o = o0 - _lambda_value(lq1, lk1, lq2, lk2, lam_init) * o1
        on = o * lax.rsqrt(jnp.mean(o * o, axis=-1, keepdims=True) + EPS) * sg[...]
        o_ref[0] = on * (1.0 - lam_init)


def _sample_attention(zs, ck, cv, page_table, pbias, sbias, qg2, kg2, lams, sg, lam_init, pages_per_step):
    n_dec = zs.shape[0]
    n_pages = page_table.shape[1]
    n_steps = n_pages // pages_per_step
    pt = page_table.reshape(-1)
    page = lambda i: pl.BlockSpec(
        (None, None, PAGE_SIZE, N_HEADS, LANES),
        lambda r, p, pt, i=i: (0, pt[r * n_pages + p * pages_per_step + i], 0, 0, 0))
    vec = lambda shape: pl.BlockSpec(shape, lambda r, p, pt: (0,) * len(shape))
    out_row = pl.BlockSpec((1, N_HEADS, LANES), lambda r, p, pt: (r, 0, 0))
    rsum = (jnp.arange(LANES)[:, None] // HEAD_DIM == jnp.arange(2 * LANES)[None, :] // LANES).astype(BF16)
    grid_spec = pltpu.PrefetchScalarGridSpec(
        num_scalar_prefetch=1,
        grid=(n_dec, n_steps),
        in_specs=[pl.BlockSpec((1,) + zs.shape[1:], lambda r, p, pt: (r, 0, 0))]
        + [page(i) for i in range(pages_per_step)] + [page(i) for i in range(pages_per_step)]
        + [vec((PAGE_SIZE, N_HEADS, 2 * LANES)), vec((N_HEADS, 2 * LANES)), vec((N_HEADS, 2 * LANES)),
           vec((LANES, 2 * LANES)),
           vec((1, LANES)), vec((1, LANES)),
           vec((1, HEAD_DIM)), vec((1, HEAD_DIM)), vec((1, HEAD_DIM)), vec((1, HEAD_DIM)), vec((1, V_DIM))],
        out_specs=[out_row, out_row, out_row],
        scratch_shapes=[
            pltpu.VMEM((N_HEADS, LANES), F32),
            pltpu.VMEM((pages_per_step * PAGE_SIZE, N_HEADS, 2 * LANES), F32),
            pltpu.VMEM((N_HEADS, 2 * LANES), F32),
            pltpu.VMEM((N_HEADS, 2 * LANES), F32),
            pltpu.VMEM((2, N_HEADS, LANES), F32),
        ],
    )
    return pl.pallas_call(
        functools.partial(_decode_kernel, pages_per_step=pages_per_step, n_steps=n_steps, lam_init=lam_init),
        grid_spec=grid_spec,
        out_shape=[jax.ShapeDtypeStruct((n_dec, N_HEADS, LANES), F32)] * 3,
        compiler_params=_cparams(("parallel", "arbitrary")),
        name="sample_attn",
    )(pt, zs, *([ck] * pages_per_step), *([cv] * pages_per_step),
      pbias, jnp.maximum(jnp.max(pbias, axis=0), 0.0), sbias, rsum, qg2, kg2, *lams, sg)


def _merge_kernel(ya, on, ga, gb, h, wa, wb, wo, n2g, wr, br,
                  h1_ref, xp_ref, topi_ref, gate_ref, rank_ref, cnt_ref, carry,
                  *, tm, tiles_per_batch, t_real, n_dec):
    i = pl.program_id(0)

    @pl.when(i == 0)
    def _():
        carry[...] = jnp.zeros_like(carry)

    y_a = jnp.dot(ya[...], wa[...], preferred_element_type=F32)
    o_cat = jnp.concatenate([on[0, hd] for hd in range(N_HEADS)], axis=1)
    y_b = jnp.dot(o_cat, wb[...], preferred_element_type=F32)
    mix = jax.nn.sigmoid(ga[...]) * y_a + jax.nn.sigmoid(gb[...]) * y_b
    h1 = h[...] + jnp.dot(mix.astype(BF16), wo[...], preferred_element_type=F32)
    h1_ref[...] = h1

    xn = h1 * lax.rsqrt(jnp.mean(h1 * h1, axis=-1, keepdims=True) + EPS) * n2g[...]
    xb = xn.astype(BF16)
    xf = xb.astype(F32)
    bits = pltpu.bitcast(xf, U32)
    half = D_MODEL // 2
    xp_ref[...] = bits[:, :half] | (bits[:, half:] >> 16)

    logits = jnp.dot(xb, wr[...], preferred_element_type=F32) + br[...]

    lane = lax.broadcasted_iota(I32, (tm, LANES), 1)
    work = logits
    vals, idxs, sels = [], [], []
    for _ in range(TOP_K):
        mx = jnp.max(work, axis=-1, keepdims=True)
        idx = jnp.min(jnp.where(work == mx, lane, LANES), axis=-1, keepdims=True)
        sel = lane == idx
        vals.append(mx)
        idxs.append(idx)
        sels.append(sel)
        work = jnp.where(sel, -jnp.inf, work)
    exps = [jnp.exp(v - vals[0]) for v in vals]
    denom = exps[0] + exps[1] + exps[2] + exps[3]

    pos_in_batch = (i % tiles_per_batch) * tm + lax.broadcasted_iota(I32, (tm, 1), 0)
    limit = jnp.where(i // tiles_per_batch == 0, t_real + n_dec, t_real)
    valid = pos_in_batch < limit

    onehot = jnp.zeros((tm, LANES), F32)
    for sel in sels:
        onehot = onehot + jnp.where(sel, 1.0, 0.0)
    onehot = jnp.where(valid, onehot, 0.0)
    rr = lax.broadcasted_iota(I32, (tm, tm), 0)
    cc = lax.broadcasted_iota(I32, (tm, tm), 1)
    lower = jnp.where(rr > cc, 1.0, 0.0).astype(BF16)
    before = jnp.dot(lower, onehot.astype(BF16), preferred_element_type=F32) + carry[...]

    topi = jnp.zeros((tm, LANES), I32)
    gate = jnp.zeros((tm, LANES), F32)
    rank = jnp.zeros((tm, LANES), I32)
    for j in range(TOP_K):
        rj = jnp.sum(jnp.where(sels[j], before, 0.0), axis=-1, keepdims=True).astype(I32)
        topi = jnp.where(lane == j, idxs[j], topi)
        gate = jnp.where(lane == j, exps[j] / denom, gate)
        rank = jnp.where(lane == j, rj, rank)
    topi_ref[...] = topi
    gate_ref[...] = gate
    rank_ref[...] = rank
    carry[...] = carry[...] + jnp.sum(onehot, axis=0, keepdims=True)
    cnt_ref[...] = jnp.broadcast_to(carry[...], cnt_ref.shape)


def _merge(ya2d, on, z2d, h2d, wa, wb, wo, n2g, wr, br, tm, t_pad, t_real, n_dec):
    rows = h2d.shape[0]
    tpb = t_pad // tm
    const = lambda shape: pl.BlockSpec(shape, lambda i: (0,) * len(shape), pipeline_mode=pl.Buffered(1))
    row = lambda w, dt=None: pl.BlockSpec((tm, w), lambda i: (i, 0))
    return pl.pallas_call(
        functools.partial(_merge_kernel, tm=tm, tiles_per_batch=tpb, t_real=t_real, n_dec=n_dec),
        grid=(rows // tm,),
        in_specs=[
            row(D_CONV),
            pl.BlockSpec((1, N_HEADS, tm, LANES), lambda i: (i // tpb, 0, i % tpb, 0)),
            pl.BlockSpec((tm, D_MODEL), lambda i: (i, 3)),
            pl.BlockSpec((tm, D_MODEL), lambda i: (i, 4)),
            row(D_MODEL),
            const((D_CONV, D_MODEL)), const((D_ATTN, D_MODEL)), const((D_MODEL, D_MODEL)),
            const((1, D_MODEL)), const((D_MODEL, LANES)), const((1, LANES)),
        ],
        out_specs=[row(D_MODEL), row(D_MODEL // 2), row(LANES), row(LANES), row(LANES),
                   pl.BlockSpec((8, LANES), lambda i: (0, 0))],
        out_shape=[
            jax.ShapeDtypeStruct((rows, D_MODEL), F32),
            jax.ShapeDtypeStruct((rows, D_MODEL // 2), U32),
            jax.ShapeDtypeStruct((rows, LANES), I32),
            jax.ShapeDtypeStruct((rows, LANES), F32),
            jax.ShapeDtypeStruct((rows, LANES), I32),
            jax.ShapeDtypeStruct((8, LANES), F32),
        ],
        scratch_shapes=[pltpu.VMEM((1, LANES), F32)],
        compiler_params=_cparams(("arbitrary",)),
        name="merge_route",
    )(ya2d, on, z2d, z2d, h2d, wa, wb, wo, n2g, wr, br)


def _dispatch_kernel(pos_ref, zpos_ref, x_ref, xs_hbm, zbuf, sem, zsem, *, tm):
    i = pl.program_id(0)

    @pl.when(i == 0)
    def _():
        zbuf[...] = jnp.zeros_like(zbuf)

        def zstart(e, c):
            pltpu.make_async_copy(zbuf, xs_hbm.at[pl.ds(pl.multiple_of(zpos_ref[e], SUB_ROWS), SUB_ROWS)],
                                  zsem).start()
            return c

        def zwait(e, c):
            pltpu.make_async_copy(zbuf, xs_hbm.at[pl.ds(0, SUB_ROWS)], zsem).wait()
            return c

        lax.fori_loop(0, N_EXPERTS, zstart, 0)
        lax.fori_loop(0, N_EXPERTS, zwait, 0)

    base = i * tm

    def start(t, c):
        for j in range(TOP_K):
            dst = pos_ref[(base + t) * TOP_K + j]
            pltpu.make_async_copy(x_ref.at[pl.ds(t, 1)], xs_hbm.at[pl.ds(dst, 1)], sem).start()
        return c

    def wait(t, c):
        for j in range(TOP_K):
            pltpu.make_async_copy(x_ref.at[pl.ds(0, 1)], xs_hbm.at[pl.ds(0, 1)], sem).wait()
        return c

    lax.fori_loop(0, tm, start, 0, unroll=DMA_LOOP_UNROLL)
    lax.fori_loop(0, tm, wait, 0, unroll=DMA_LOOP_UNROLL)


def _dispatch(pos_flat, zpos, xp, n_slots, tm):
    rows, width = xp.shape
    grid_spec = pltpu.PrefetchScalarGridSpec(
        num_scalar_prefetch=2,
        grid=(rows // tm,),
        in_specs=[pl.BlockSpec((tm, width), lambda i, p, z: (i, 0))],
        out_specs=pl.BlockSpec(memory_space=pl.ANY),
        scratch_shapes=[pltpu.VMEM((SUB_ROWS, width), U32),
                        pltpu.SemaphoreType.DMA(()), pltpu.SemaphoreType.DMA(())],
    )
    return pl.pallas_call(
        functools.partial(_dispatch_kernel, tm=tm),
        grid_spec=grid_spec,
        out_shape=jax.ShapeDtypeStruct((n_slots, width), U32),
        compiler_params=_cparams(("arbitrary",)),
        name="moe_dispatch",
    )(pos_flat, zpos, xp)


def _moe_kernel(item_e, item_start, item_nsub, n_items,
                xs_hbm, w1g, w1u, w2, b1g, b1u, b2,
                ys_hbm, xraw, xa, xb, yacc, sem_in, sem_out, *, nf):
    it = pl.program_id(0)
    f = pl.program_id(1)
    half = D_MODEL // 2

    @pl.when(it < n_items[0])
    def _():
        start = pl.multiple_of(item_start[it], SUB_ROWS)
        nsub = item_nsub[it]

        def in_copy(s):
            r0 = pl.multiple_of(s * SUB_ROWS, SUB_ROWS)
            return pltpu.make_async_copy(xs_hbm.at[pl.ds(start + r0, SUB_ROWS)],
                                         xraw.at[pl.ds(r0, SUB_ROWS)], sem_in.at[s])

        def out_copy(r0, m):
            rows = pl.ds(pl.multiple_of(r0, SUB_ROWS), m)
            return pltpu.make_async_copy(yacc.at[rows], ys_hbm.at[pl.ds(start + r0, m)], sem_out)

        def mlp_rows(r0, m, first, last):
            r0 = pl.multiple_of(r0, SUB_ROWS)
            rows = pl.ds(r0, m)
            if first:
                for k in range(m // SUB_ROWS):
                    sub = pl.ds(r0 + k * SUB_ROWS, SUB_ROWS)
                    in_copy(r0 // SUB_ROWS + k).wait()
                    w = xraw[sub, :]
                    xa[sub, :] = pltpu.bitcast(w & jnp.uint32(0xFFFF0000), F32).astype(BF16)
                    xb[sub, :] = pltpu.bitcast(w << 16, F32).astype(BF16)
            wg = w1g[0].astype(BF16)
            wu = w1u[0].astype(BF16)
            wd = w2[0].astype(BF16)
            a = xa[rows, :]
            b = xb[rows, :]
            hg = (jnp.dot(a, wg[:half], preferred_element_type=F32)
                  + jnp.dot(b, wg[half:], preferred_element_type=F32) + b1g[0])
            hu = (jnp.dot(a, wu[:half], preferred_element_type=F32)
                  + jnp.dot(b, wu[half:], preferred_element_type=F32) + b1u[0])
            g = jnp.minimum(hg, SWIGLU_LIMIT)
            up = jnp.clip(hu, -SWIGLU_LIMIT, SWIGLU_LIMIT)
            act = g * jax.nn.sigmoid(SWIGLU_ALPHA * g) * (up + 1.0)
            y = jnp.dot(act.astype(BF16), wd, preferred_element_type=F32)
            yacc[rows, :] = y + (b2[0] if first else yacc[rows, :])
            if last:
                out_copy(r0, m).start()

        per_big, per_mid = BIG_ROWS // SUB_ROWS, MID_ROWS // SUB_ROWS
        nbig = nsub // per_big
        mid0 = nbig * BIG_ROWS
        nmid = (nsub - nbig * per_big) // per_mid
        tail0 = nbig * per_big + nmid * per_mid

        def sweep(first, last):
            lax.fori_loop(0, nbig, lambda c, u: (mlp_rows(c * BIG_ROWS, BIG_ROWS, first, last), u)[1], 0)
            lax.fori_loop(0, nmid, lambda c, u: (mlp_rows(mid0 + c * MID_ROWS, MID_ROWS, first, last), u)[1], 0)
            lax.fori_loop(tail0, nsub, lambda s, u: (mlp_rows(s * SUB_ROWS, SUB_ROWS, first, last), u)[1], 0)

        @pl.when(f == 0)
        def _():
            lax.fori_loop(0, nsub, lambda s, c: (in_copy(s).start(), c)[1], 0)
            sweep(True, False)

        @pl.when((f > 0) & (f < nf - 1))
        def _():
            sweep(False, False)

        @pl.when(f == nf - 1)
        def _():
            sweep(False, True)
            lax.fori_loop(0, nbig, lambda c, u: (out_copy(c * BIG_ROWS, BIG_ROWS).wait(), u)[1], 0)
            lax.fori_loop(0, nmid, lambda c, u: (out_copy(mid0 + c * MID_ROWS, MID_ROWS).wait(), u)[1], 0)
            lax.fori_loop(tail0, nsub, lambda s, u: (out_copy(s * SUB_ROWS, SUB_ROWS).wait(), u)[1], 0)


def _moe_experts(items, xs, w1, b1, w2, b2, n_slots, rmax, tf):
    item_e, item_start, item_nsub, n_items = items
    ni = item_e.shape[0]
    nf = D_FF // tf
    half = D_MODEL // 2

    def fidx(it, f, n):
        return jnp.where(it < n[0], f, nf - 1)

    grid_spec = pltpu.PrefetchScalarGridSpec(
        num_scalar_prefetch=4,
        grid=(ni, nf),
        in_specs=[
            pl.BlockSpec(memory_space=pl.ANY),
            pl.BlockSpec((1, D_MODEL, tf), lambda it, f, e, s, ns, n: (e[it], 0, fidx(it, f, n))),
            pl.BlockSpec((1, D_MODEL, tf), lambda it, f, e, s, ns, n: (e[it], 0, nf + fidx(it, f, n))),
            pl.BlockSpec((1, tf, D_MODEL), lambda it, f, e, s, ns, n: (e[it], fidx(it, f, n), 0)),
            pl.BlockSpec((1, 1, tf), lambda it, f, e, s, ns, n: (e[it], 0, fidx(it, f, n))),
            pl.BlockSpec((1, 1, tf), lambda it, f, e, s, ns, n: (e[it], 0, nf + fidx(it, f, n))),
            pl.BlockSpec((1, 1, D_MODEL), lambda it, f, e, s, ns, n: (e[it], 0, 0)),
        ],
        out_specs=pl.BlockSpec(memory_space=pl.ANY),
        scratch_shapes=[
            pltpu.VMEM((rmax, half), U32),
            pltpu.VMEM((rmax, half), BF16),
            pltpu.VMEM((rmax, half), BF16),
            pltpu.VMEM((rmax, D_MODEL), F32),
            pltpu.SemaphoreType.DMA((rmax // SUB_ROWS,)), pltpu.SemaphoreType.DMA(()),
        ],
    )
    return pl.pallas_call(
        functools.partial(_moe_kernel, nf=nf),
        grid_spec=grid_spec,
        out_shape=jax.ShapeDtypeStruct((n_slots, D_MODEL), F32),
        compiler_params=_cparams(("arbitrary", "arbitrary")),
        name="moe_experts",
    )(item_e, item_start, item_nsub, n_items, xs, w1, w1, w2,
      b1.reshape(N_EXPERTS, 1, 2 * D_FF), b1.reshape(N_EXPERTS, 1, 2 * D_FF), b2.reshape(N_EXPERTS, 1, D_MODEL))


def _combine_kernel(pos_ref, ys_hbm, h1_hbm, gate_hbm, y_ref, ysamp_ref, ybuf, hbuf, gbuf, sem, hsem,
                    *, tm, tiles_per_batch, t_pad, n_dec, samp_row0):
    i = pl.program_id(0)
    n_tiles = pl.num_programs(0)

    def gather_and_mix(row0, n):
        hcp = pltpu.make_async_copy(h1_hbm.at[pl.ds(row0, n)], hbuf.at[pl.ds(0, n)], hsem)
        gcp = pltpu.make_async_copy(gate_hbm.at[pl.ds(row0, n)], gbuf.at[pl.ds(0, n)], hsem)
        hcp.start()
        gcp.start()

        def start(t, c):
            for j in range(TOP_K):
                src = pos_ref[(row0 + t) * TOP_K + j]
                pltpu.make_async_copy(ys_hbm.at[pl.ds(src, 1)], ybuf.at[j, pl.ds(t, 1)], sem).start()
            return c

        def wait(t, c):
            for j in range(TOP_K):
                pltpu.make_async_copy(ys_hbm.at[pl.ds(0, 1)], ybuf.at[0, pl.ds(0, 1)], sem).wait()
            return c

        lax.fori_loop(0, n, start, 0, unroll=DMA_LOOP_UNROLL)
        hcp.wait()
        gcp.wait()
        lax.fori_loop(0, n, wait, 0, unroll=DMA_LOOP_UNROLL)
        g = gbuf[0:n, :]
        out = hbuf[0:n, :]
        for j in range(TOP_K):
            out = out + g[:, j:j + 1] * ybuf[j, 0:n, :]
        return out

    row0 = (i // tiles_per_batch) * t_pad + N_META + (i % tiles_per_batch) * tm
    y_ref[0] = gather_and_mix(pl.multiple_of(row0, 8), tm)

    @pl.when(i == n_tiles - 1)
    def _():
        ysamp_ref[...] = gather_and_mix(samp_row0, n_dec)


def _combine(pos_flat, ys, h1, gate, n_batch, seq, t_pad, n_dec, samp_row0, tm):
    tpb = seq // tm
    grid_spec = pltpu.PrefetchScalarGridSpec(
        num_scalar_prefetch=1,
        grid=(n_batch * tpb,),
        in_specs=[pl.BlockSpec(memory_space=pl.ANY)] * 3,
        out_specs=[pl.BlockSpec((1, tm, D_MODEL), lambda i, p: (i // tpb, i % tpb, 0)),
                   pl.BlockSpec((n_dec, D_MODEL), lambda i, p: (0, 0))],
        scratch_shapes=[
            pltpu.VMEM((TOP_K, tm, D_MODEL), F32),
            pltpu.VMEM((tm, D_MODEL), F32),
            pltpu.VMEM((tm, LANES), F32),
            pltpu.SemaphoreType.DMA(()), pltpu.SemaphoreType.DMA(()),
        ],
    )
    return pl.pallas_call(
        functools.partial(_combine_kernel, tm=tm, tiles_per_batch=tpb, t_pad=t_pad, n_dec=n_dec,
                          samp_row0=samp_row0),
        grid_spec=grid_spec,
        out_shape=[jax.ShapeDtypeStruct((n_batch, seq, D_MODEL), F32),
                   jax.ShapeDtypeStruct((n_dec, D_MODEL), F32)],
        compiler_params=_cparams(("arbitrary",)),
        name="moe_combine",
    )(pos_flat, ys, h1, gate)


def _t5_bucket(rel):
    n = jnp.maximum(rel, 0)
    max_exact = N_BUCKETS // 2
    nf = jnp.maximum(n, 1).astype(F32)
    large = max_exact + (jnp.log(nf / max_exact) / math.log(MAX_DISTANCE / max_exact)
                         * (N_BUCKETS - max_exact)).astype(I32)
    large = jnp.minimum(large, N_BUCKETS - 1)
    return jnp.where(n < max_exact, n, large)


def _bias_tables(rel_bias, tq):
    shifted = (rel_bias - rel_bias[N_BUCKETS - 1][None]) * LOG2E

    def bias_of(rel, out):
        onehot = jax.nn.one_hot(_t5_bucket(rel), N_BUCKETS, dtype=F32)
        return jnp.einsum("...b,bh->" + out, onehot, shifted, precision=lax.Precision.HIGHEST)

    rel0 = jnp.arange(tq)[:, None] - jnp.arange(tq)[None, :]
    diag = jnp.where((rel0 >= 0)[None], bias_of(rel0, "h..."), NEG_INF)
    tiles = jnp.stack([diag, bias_of(tq + rel0, "h...")])
    rel_last = PAGE_SIZE - jnp.arange(PAGE_SIZE)
    pbias = jnp.broadcast_to(bias_of(rel_last, "...h")[:, :, None], (PAGE_SIZE, N_HEADS, 2 * LANES))
    sbias = jnp.broadcast_to(bias_of(jnp.zeros((), I32), "...h")[:, None], (N_HEADS, 2 * LANES))
    return tiles, pbias, sbias


def _routing_tables(counts, topi, rank, valid, n_trash_rows, rmax, n_items_max, real_slots):
    padded = (counts + SUB_ROWS - 1) // SUB_ROWS * SUB_ROWS
    ends = jnp.cumsum(padded)
    off = ends - padded
    pos = off[topi] + rank
    trash_row = jnp.cumsum(jnp.logical_not(valid).astype(I32)) - 1
    trash = real_slots + trash_row[:, None] * TOP_K + jnp.arange(TOP_K, dtype=I32)[None]
    pos_scatter = jnp.where(valid[:, None], pos, trash).reshape(-1)
    pos_gather = jnp.where(valid[:, None], pos, 0).reshape(-1)
    zero_trash = real_slots + -(-(n_trash_rows * TOP_K) // SUB_ROWS) * SUB_ROWS
    zpos = jnp.where(counts > 0, ends - SUB_ROWS, zero_trash).astype(I32)
    per_e = (padded + rmax - 1) // rmax
    item_end = jnp.cumsum(per_e)
    n_items = item_end[-1]
    t = jnp.arange(n_items_max, dtype=I32)
    tt = jnp.minimum(t, n_items - 1)
    e_of = jnp.minimum(jnp.searchsorted(item_end, tt, side="right"), N_EXPERTS - 1).astype(I32)
    k = tt - (item_end - per_e)[e_of]
    start = off[e_of] + k * rmax
    nrows = jnp.minimum(rmax, padded[e_of] - k * rmax)
    nsub = jnp.where(t < n_items, nrows // SUB_ROWS, 0).astype(I32)
    items = (e_of, start.astype(I32), nsub, n_items.reshape(1).astype(I32))
    return pos_scatter.astype(I32), pos_gather.astype(I32), zpos, items, zero_trash + SUB_ROWS


def kernel(x_prompt, x_sample, cache_k, cache_v, state_conv, page_table, meta_tokens, rel_bias, norm1_g, w_in,
           conv_w, q_norm_g, k_norm_g, lambda_q1, lambda_k1, lambda_q2, lambda_k2, subln_g, w_branch_a,
           w_branch_b, w_out, norm2_g, w_router, b_router, w_mlp1, b_mlp1, w_mlp2, b_mlp2):
    n_batch, seq, _ = x_prompt.shape
    n_dec, t_dec, _ = x_sample.shape
    depth = cache_k.shape[0]
    assert depth == 1 and t_dec == 1 and n_dec == 8
    t_real = seq + N_META
    tq = 384
    t_pad = -(-(t_real + n_dec) // tq) * tq
    rows = n_batch * t_pad
    samp_row0 = t_real
    assert samp_row0 % 8 == 0 and t_pad % 24 == 0
    lam_init = 0.8 - 0.6 * math.exp(-0.3 * 0)

    z2d, h2d = _in_proj(x_prompt, meta_tokens, x_sample.reshape(n_dec, D_MODEL), norm1_g[0], w_in[0],
                        t_pad, tm=t_pad // 3, tn=512)
    z3 = z2d.reshape(n_batch, t_pad, D_IN)

    qg2 = jnp.tile(q_norm_g[0], 2).reshape(1, LANES)
    kg2 = jnp.tile(k_norm_g[0], 2).reshape(1, LANES)
    sg = subln_g[0].reshape(1, V_DIM)
    lams = [v[0].reshape(1, HEAD_DIM) for v in (lambda_q1, lambda_k1, lambda_q2, lambda_k2)]
    bias_tiles, pbias, sbias = _bias_tables(rel_bias, tq)

    ya, k_p, v_p, qs, kb, vb, ust = _prep(z3, conv_w[0], qg2, kg2, n_batch, t_pad, t_real, tm=tq)
    on = _prompt_attention(qs, kb, vb, bias_tiles, lams, sg, lam_init, t_pad, tq)

    row_block = samp_row0 // n_dec
    ya_s, u_s = _sample_conv(z2d, state_conv[0, :, 0], state_conv[0, :, 1], conv_w[0], row_block, n_dec)
    zs = z2d[samp_row0:samp_row0 + n_dec].reshape(n_dec, D_IN // LANES, LANES)
    o_s, k_s, v_s = _sample_attention(zs, cache_k, cache_v, page_table, pbias, sbias, qg2, kg2, lams, sg,
                                      lam_init, pages_per_step=4)

    ya = ya.at[0, samp_row0:samp_row0 + n_dec].set(ya_s.astype(BF16))
    on = on.at[0, :, samp_row0:samp_row0 + n_dec].set(o_s.transpose(1, 0, 2).astype(BF16))

    wr = jnp.pad(w_router[0], ((0, 0), (0, LANES - N_EXPERTS)))
    br = jnp.concatenate([b_router[0], jnp.full((LANES - N_EXPERTS,), NEG_INF, F32)]).reshape(1, LANES)
    h1, xp, topi, gate, rank, cnt = _merge(
        ya.reshape(rows, D_CONV), on, z2d, h2d,
        w_branch_a[0].astype(BF16), w_branch_b[0].astype(BF16), w_out[0].astype(BF16),
        norm2_g[0].reshape(1, D_MODEL), wr.astype(BF16), br, tm=tq // 2, t_pad=t_pad, t_real=t_real, n_dec=n_dec)

    rmax = 1536
    n_assign = (n_batch * t_real + n_dec) * TOP_K
    real_slots = -(-(n_assign + N_EXPERTS * (SUB_ROWS - 1)) // SUB_ROWS) * SUB_ROWS
    n_items_max = N_EXPERTS + real_slots // rmax
    pos_in_batch = jnp.arange(rows, dtype=I32) % t_pad
    valid = (pos_in_batch < t_real) | ((jnp.arange(rows) < t_pad) & (pos_in_batch < t_real + n_dec))
    pos_scatter, pos_gather, zpos, items, n_slots = _routing_tables(
        cnt[0, :N_EXPERTS].astype(I32), topi[:, :TOP_K], rank[:, :TOP_K], valid,
        rows - n_assign // TOP_K, rmax, n_items_max, real_slots)

    xs = _dispatch(pos_scatter, zpos, xp, n_slots, tm=tq)
    ys = _moe_experts(items, xs, w_mlp1[0], b_mlp1[0], w_mlp2[0], b_mlp2[0], n_slots, rmax, tf=256)
    y_prompt, y_samp = _combine(pos_gather, ys, h1, gate, n_batch, seq, t_pad, n_dec, samp_row0, tm=512)

    return (y_prompt,
            y_samp.reshape(n_dec, 1, D_MODEL),
            k_p.reshape(1, n_batch, t_real, N_HEADS, 2 * HEAD_DIM),
            v_p.reshape(1, n_batch, t_real, N_HEADS, V_DIM),
            ust[:, 6:8].reshape(1, n_batch, CONV_K - 1, D_CONV),
            k_s.reshape(1, n_dec, 1, N_HEADS, 2 * HEAD_DIM),
            v_s.reshape(1, n_dec, 1, N_HEADS, V_DIM),
            jnp.stack([state_conv[0, :, 1], u_s], axis=1).reshape(1, n_dec, CONV_K - 1, D_CONV))
```

```python
import functools
import math

import jax
import jax.numpy as jnp
import numpy as np
from jax import lax
from jax.experimental import pallas as pl
from jax.experimental.pallas import tpu as pltpu

D_MODEL = 2048
N_META = 16
D_CONV = 1024
CONV_K = 3
N_HEADS = 8
HEAD_DIM = 64
V_DIM = 128
D_QK = 1024
D_ATTN = 1024
D_IN = 10240
N_BUCKETS = 32
MAX_DISTANCE = 128
N_EXPERTS = 32
TOP_K = 4
D_FF = 2048
SWIGLU_LIMIT = 7.0
SWIGLU_ALPHA = 1.702
EPS = 1e-6
NEG_INF = -1e30
PAGE_SIZE = 128

F32 = jnp.float32
BF16 = jnp.bfloat16
I32 = jnp.int32
U32 = jnp.uint32

LANES = 128
SUB_ROWS = 128
BIG_ROWS = 1024
MID_ROWS = 512
ATTN_ROW_CHUNK = 64
LOG2E = 1.4426950408889634
DMA_LOOP_UNROLL = 4
DECODE_KEY_CHUNK = 16
D_QK_OFF = 3 * D_CONV
VMEM_LIMIT = 56 * 1024 * 1024


def _cparams(sem, vmem=VMEM_LIMIT):
    return pltpu.CompilerParams(dimension_semantics=sem, vmem_limit_bytes=vmem)


def _in_proj_kernel(xp_hbm, meta_ref, xs_ref, g_ref, w_ref, z_ref, h_ref, xn_ref, sems,
                    *, tm, tiles_per_batch, seq, n_dec):
    i = pl.program_id(0)
    b = i // tiles_per_batch
    t_real = seq + N_META

    @pl.when(pl.program_id(1) == 0)
    def _():
        for t in range(tiles_per_batch):
            @pl.when(i % tiles_per_batch == t)
            def _(t=t):
                lo = t * tm
                p0, p1 = max(lo, N_META), min(lo + tm, t_real)
                n_chunks = 4
                step = -(-(p1 - p0) // (8 * n_chunks)) * 8
                copies = []
                for c in range(n_chunks):
                    r0, r1 = p0 + c * step, min(p0 + (c + 1) * step, p1)
                    copies.append(pltpu.make_async_copy(
                        xp_hbm.at[b, pl.ds(r0 - N_META, r1 - r0)], h_ref.at[pl.ds(r0 - lo, r1 - r0)], sems.at[c]))
                for cp in copies:
                    cp.start()
                if lo < N_META:
                    h_ref[0:N_META, :] = meta_ref[...]
                if lo + tm > t_real:
                    h_ref[t_real - lo:tm, :] = jnp.zeros((lo + tm - t_real, D_MODEL), F32)

                    @pl.when(b == 0)
                    def _():
                        h_ref[t_real - lo:t_real - lo + n_dec, :] = xs_ref[...]
                for cp in copies:
                    cp.wait()

        x = h_ref[...]
        ms = jnp.mean(x * x, axis=-1, keepdims=True)
        xn_ref[...] = (x * lax.rsqrt(ms + EPS) * g_ref[...]).astype(BF16)

    z_ref[...] = jnp.dot(xn_ref[...], w_ref[...].astype(BF16), preferred_element_type=F32)


def _in_proj(x_prompt, meta_tokens, x_sample2d, g, w, t_pad, tm, tn):
    n_batch, seq, _ = x_prompt.shape
    n_dec = x_sample2d.shape[0]
    rows = n_batch * t_pad
    const = lambda shape: pl.BlockSpec(shape, lambda i, j: (0,) * len(shape))
    return pl.pallas_call(
        functools.partial(_in_proj_kernel, tm=tm, tiles_per_batch=t_pad // tm, seq=seq, n_dec=n_dec),
        grid=(rows // tm, D_IN // tn),
        in_specs=[
            pl.BlockSpec(memory_space=pl.ANY),
            const((N_META, D_MODEL)), const((n_dec, D_MODEL)), const((1, D_MODEL)),
            pl.BlockSpec((D_MODEL, tn), lambda i, j: (0, j)),
        ],
        out_specs=[pl.BlockSpec((tm, tn), lambda i, j: (i, j)),
                   pl.BlockSpec((tm, D_MODEL), lambda i, j: (i, 0))],
        out_shape=[jax.ShapeDtypeStruct((rows, D_IN), F32),
                   jax.ShapeDtypeStruct((rows, D_MODEL), F32)],
        scratch_shapes=[pltpu.VMEM((tm, D_MODEL), BF16), pltpu.SemaphoreType.DMA((4,))],
        compiler_params=_cparams(("parallel", "arbitrary")),
        name="in_proj",
    )(x_prompt, meta_tokens, x_sample2d, g.reshape(1, D_MODEL), w)


def _half_norm(x, g2, lo):
    t = x * x
    s_lo = jnp.sum(jnp.where(lo, t, 0.0), axis=-1, keepdims=True)
    s_hi = jnp.sum(jnp.where(lo, 0.0, t), axis=-1, keepdims=True)
    inv = jnp.where(lo, lax.rsqrt(s_lo * (1.0 / HEAD_DIM) + EPS), lax.rsqrt(s_hi * (1.0 / HEAD_DIM) + EPS))
    return x * inv * g2


def _prep_kernel(zb, zc, zx, zq, zk, zv, cw, qg, kg,
                 ya_ref, kout, vout, qs, kb, vb, ust, carry, *, tm, state_tile, state_row):
    i = pl.program_id(1)

    @pl.when(i == 0)
    def _():
        carry[...] = jnp.zeros_like(carry)

    u = zc[0] * zx[0]
    prev = carry[...]
    row = lax.broadcasted_iota(I32, (tm, 1), 0)
    u1 = jnp.where(row == 0, prev[7:8], pltpu.roll(u, 1, 0))
    u2 = jnp.where(row == 0, prev[6:7], jnp.where(row == 1, prev[7:8], pltpu.roll(u, 2, 0)))
    y = cw[0:1] * u2 + cw[1:2] * u1 + cw[2:3] * u
    ya_ref[0] = (zb[0] * y).astype(BF16)
    carry[...] = u[tm - 8:tm]

    @pl.when(i == state_tile)
    def _():
        ust[0] = u[state_row:state_row + 8]

    lo = lax.broadcasted_iota(I32, (1, LANES), 1) < HEAD_DIM
    for h in range(N_HEADS):
        sl = slice(h * LANES, (h + 1) * LANES)
        qn = _half_norm(zq[0, :, sl], qg[...], lo) * (LOG2E * HEAD_DIM ** -0.5)
        qs[0, 0, h] = jnp.where(lo, qn, 0.0).astype(BF16)
        qs[0, 1, h] = jnp.where(lo, 0.0, qn).astype(BF16)
        kn = _half_norm(zk[0, :, sl], kg[...], lo)
        kout[0, :, sl] = kn
        kb[0, h] = kn.astype(BF16)
        vb[0, h] = zv[0, :, sl].astype(BF16)
    vout[0] = zv[0]


def _prep(z3, conv_w, qg2, kg2, n_batch, t_pad, t_real, tm):
    nt = t_pad // tm
    state_tile = (t_real - 2) // tm
    state_row = ((t_real - 2) % tm) // 8 * 8
    sec = lambda s: pl.BlockSpec((1, tm, 1024), lambda b, i, s=s: (b, i, s))
    small = lambda shape: pl.BlockSpec(shape, lambda b, i: (0,) * len(shape))
    return pl.pallas_call(
        functools.partial(_prep_kernel, tm=tm, state_tile=state_tile, state_row=state_row),
        grid=(n_batch, nt),
        in_specs=[sec(0), sec(1), sec(2), sec(3), sec(4), sec(5),
                  small((CONV_K, D_CONV)), small((1, LANES)), small((1, LANES))],
        out_specs=[
            pl.BlockSpec((1, tm, D_CONV), lambda b, i: (b, i, 0)),
            pl.BlockSpec((1, tm, D_QK), lambda b, i: (b, i, 0)),
            pl.BlockSpec((1, tm, D_ATTN), lambda b, i: (b, i, 0)),
            pl.BlockSpec((1, 2, N_HEADS, tm, LANES), lambda b, i: (b, 0, 0, i, 0)),
            pl.BlockSpec((1, N_HEADS, tm, LANES), lambda b, i: (b, 0, i, 0)),
            pl.BlockSpec((1, N_HEADS, tm, LANES), lambda b, i: (b, 0, i, 0)),
            pl.BlockSpec((1, 8, D_CONV), lambda b, i: (b, 0, 0)),
        ],
        out_shape=[
            jax.ShapeDtypeStruct((n_batch, t_pad, D_CONV), BF16),
            jax.ShapeDtypeStruct((n_batch, t_real, D_QK), F32),
            jax.ShapeDtypeStruct((n_batch, t_real, D_ATTN), F32),
            jax.ShapeDtypeStruct((n_batch, 2, N_HEADS, t_pad, LANES), BF16),
            jax.ShapeDtypeStruct((n_batch, N_HEADS, t_pad, LANES), BF16),
            jax.ShapeDtypeStruct((n_batch, N_HEADS, t_pad, LANES), BF16),
            jax.ShapeDtypeStruct((n_batch, 8, D_CONV), F32),
        ],
        scratch_shapes=[pltpu.VMEM((8, D_CONV), F32)],
        compiler_params=_cparams(("parallel", "arbitrary")),
        name="prep",
    )(z3, z3, z3, z3, z3, z3, conv_w, qg2, kg2)


def _lambda_value(lq1, lk1, lq2, lk2, lam_init):
    a = jnp.sum(lq1[...] * lk1[...], axis=-1, keepdims=True)
    b = jnp.sum(lq2[...] * lk2[...], axis=-1, keepdims=True)
    return jnp.exp(a) - jnp.exp(b) + lam_init


def _attn_kernel(qi_tab, ki_tab, q_ref, k_ref, v_ref, bias_ref, lq1, lk1, lq2, lk2, sg,
                 o_ref, m_ref, l_ref, acc_ref, s_ref, p_ref, a_ref, *, tq, tk, lam_init):
    step = pl.program_id(1)
    qi = qi_tab[step]
    ki = ki_tab[step]
    rc = ATTN_ROW_CHUNK

    @pl.when(ki == 0)
    def _():
        m_ref[...] = jnp.full_like(m_ref, NEG_INF)
        l_ref[...] = jnp.zeros_like(l_ref)
        acc_ref[...] = jnp.zeros_like(acc_ref)

    def score(h, slot):
        q = q_ref[0, :, h].reshape(2 * tq, LANES)
        s_ref[slot] = lax.dot_general(q, k_ref[0, h], (((1,), (1,)), ((), ())), preferred_element_type=F32)

    def softmax_update(h, slot, near):
        for c in range(2 * tq // rc):
            rows = pl.ds(c * rc, rc)
            s = s_ref[slot, rows, :]
            if near:
                s = s + bias_ref[0, h, pl.ds((c % (tq // rc)) * rc, rc), :]
            m_prev = m_ref[h, rows, :]
            m_new = jnp.maximum(m_prev, jnp.max(s, axis=-1, keepdims=True))
            alpha = jnp.exp2(m_prev - m_new)
            p = jnp.exp2(s - jnp.concatenate([m_new] * (tk // LANES), axis=1))
            l_ref[h, rows, :] = alpha * l_ref[h, rows, :] + jnp.sum(p, axis=-1, keepdims=True)
            m_ref[h, rows, :] = m_new
            a_ref[slot, rows, :] = alpha
            p_ref[slot, rows, :] = p.astype(BF16)

    def weighted_values(h, slot):
        acc_ref[h] = a_ref[slot] * acc_ref[h] + jnp.dot(p_ref[slot], v_ref[0, h], preferred_element_type=F32)

    def head_pair(hp, near):
        for slot in range(2):
            score(2 * hp + slot, slot)
            softmax_update(2 * hp + slot, slot, near)
            weighted_values(2 * hp + slot, slot)

    @pl.when(ki >= qi - 1)
    def _():
        lax.fori_loop(0, N_HEADS // 2, lambda hp, c: (head_pair(hp, True), c)[1], 0)

    @pl.when(ki < qi - 1)
    def _():
        lax.fori_loop(0, N_HEADS // 2, lambda hp, c: (head_pair(hp, False), c)[1], 0)

    @pl.when(ki == qi)
    def _():
        lam = _lambda_value(lq1, lk1, lq2, lk2, lam_init)

        def fin(h, c):
            acc = acc_ref[h]
            l = l_ref[h]
            o = acc[:tq] / l[:tq] - lam * (acc[tq:] / l[tq:])
            on = o * lax.rsqrt(jnp.mean(o * o, axis=-1, keepdims=True) + EPS) * sg[...]
            o_ref[0, h] = (on * (1.0 - lam_init)).astype(BF16)
            return c

        lax.fori_loop(0, N_HEADS, fin, 0)


def _prompt_attention(qs, kb, vb, bias_tiles, lams, sg, lam_init, t_pad, tq):
    n_batch = qs.shape[0]
    nq = t_pad // tq
    pairs = [(qi, ki) for qi in range(nq) for ki in range(qi + 1)]
    qi_tab = jnp.asarray(np.array([p[0] for p in pairs], np.int32))
    ki_tab = jnp.asarray(np.array([p[1] for p in pairs], np.int32))
    vec = lambda n: pl.BlockSpec((1, n), lambda b, s, qt, kt: (0, 0))
    grid_spec = pltpu.PrefetchScalarGridSpec(
        num_scalar_prefetch=2,
        grid=(n_batch, len(pairs)),
        in_specs=[
            pl.BlockSpec((1, 2, N_HEADS, tq, LANES), lambda b, s, qt, kt: (b, 0, 0, qt[s], 0)),
            pl.BlockSpec((1, N_HEADS, tq, LANES), lambda b, s, qt, kt: (b, 0, kt[s], 0)),
            pl.BlockSpec((1, N_HEADS, tq, LANES), lambda b, s, qt, kt: (b, 0, kt[s], 0)),
            pl.BlockSpec((1, N_HEADS, tq, tq),
                         lambda b, s, qt, kt: (jnp.minimum(qt[s] - kt[s], 1), 0, 0, 0)),
            vec(HEAD_DIM), vec(HEAD_DIM), vec(HEAD_DIM), vec(HEAD_DIM), vec(V_DIM),
        ],
        out_specs=pl.BlockSpec((1, N_HEADS, tq, LANES), lambda b, s, qt, kt: (b, 0, qt[s], 0)),
        scratch_shapes=[
            pltpu.VMEM((N_HEADS, 2 * tq, LANES), F32),
            pltpu.VMEM((N_HEADS, 2 * tq, LANES), F32),
            pltpu.VMEM((N_HEADS, 2 * tq, LANES), F32),
            pltpu.VMEM((2, 2 * tq, tq), F32),
            pltpu.VMEM((2, 2 * tq, tq), BF16),
            pltpu.VMEM((2, 2 * tq, LANES), F32),
        ],
    )
    return pl.pallas_call(
        functools.partial(_attn_kernel, tq=tq, tk=tq, lam_init=lam_init),
        grid_spec=grid_spec,
        out_shape=jax.ShapeDtypeStruct((n_batch, N_HEADS, t_pad, LANES), BF16),
        compiler_params=_cparams(("parallel", "arbitrary")),
        name="prompt_attn",
    )(qi_tab, ki_tab, qs, kb, vb, bias_tiles, *lams, sg)


def _sample_conv_kernel(zb, zc, zx, s0, s1, cw, ya_ref, u_ref):
    u = zc[...] * zx[...]
    y = cw[0:1] * s0[...] + cw[1:2] * s1[...] + cw[2:3] * u
    ya_ref[...] = zb[...] * y
    u_ref[...] = u


def _sample_conv(z2d, s0, s1, conv_w, row_block, n):
    sec = lambda s: pl.BlockSpec((n, D_CONV), lambda i, s=s: (row_block, s))
    full = lambda r: pl.BlockSpec((r, D_CONV), lambda i: (0, 0))
    return pl.pallas_call(
        _sample_conv_kernel,
        grid=(1,),
        in_specs=[sec(0), sec(1), sec(2), full(n), full(n), full(CONV_K)],
        out_specs=[full(n), full(n)],
        out_shape=[jax.ShapeDtypeStruct((n, D_CONV), F32)] * 2,
        name="sample_conv",
    )(z2d, z2d, z2d, s0, s1, conv_w)


def _decode_kernel(pt_ref, zs, *rest, pages_per_step, n_steps, lam_init):
    kpages = rest[:pages_per_step]
    vpages = rest[pages_per_step:2 * pages_per_step]
    (pbias, pbias_max, sbias, rsum, qg, kg, lq1, lk1, lq2, lk2, sg,
     o_ref, kout, vout, q_sc, s_ref, m_ref, l_ref, acc_ref) = rest[2 * pages_per_step:]
    p = pl.program_id(1)
    kc = DECODE_KEY_CHUNK
    q_lo, k_lo, v_lo = D_QK_OFF // LANES, (D_QK_OFF + D_QK) // LANES, (D_QK_OFF + 2 * D_QK) // LANES
    lo = lax.broadcasted_iota(I32, (1, LANES), 1) < HEAD_DIM

    @pl.when(p == 0)
    def _():
        q_sc[...] = _half_norm(zs[0, q_lo:q_lo + N_HEADS], qg[...], lo) * (LOG2E * HEAD_DIM ** -0.5)
        kout[0] = _half_norm(zs[0, k_lo:k_lo + N_HEADS], kg[...], lo)
        vout[0] = zs[0, v_lo:v_lo + N_HEADS]
        m_ref[...] = jnp.full_like(m_ref, NEG_INF)
        l_ref[...] = jnp.zeros_like(l_ref)
        acc_ref[...] = jnp.zeros_like(acc_ref)

    q = q_sc[...]

    def scores(k):
        n = k.shape[0]
        t = (k * q[None]).reshape(n * N_HEADS, LANES).astype(BF16)
        return jnp.dot(t, rsum[...], preferred_element_type=F32).reshape(n, N_HEADS, 2 * LANES)

    m_prev = m_ref[...]
    m_new = m_prev
    for i in range(pages_per_step):
        s = scores(kpages[i][...])
        s_ref[pl.ds(i * PAGE_SIZE, PAGE_SIZE)] = s
        m_new = jnp.maximum(m_new, jnp.max(s, axis=0))

    m_new = m_new + jnp.where(p == n_steps - 1, pbias_max[...], 0.0)

    @pl.when(p == n_steps - 1)
    def _():
        last = pl.ds((pages_per_step - 1) * PAGE_SIZE, PAGE_SIZE)
        s_ref[last] = s_ref[last] + pbias[...]

    alpha = jnp.exp2(m_prev - m_new)

    def accumulate(i):
        def body(c, carry):
            l, a0, a1 = carry
            pc = jnp.exp2(s_ref[pl.ds(i * PAGE_SIZE + c * kc, kc)] - m_new[None])
            vc = vpages[i][pl.ds(c * kc, kc)]
            return (l + jnp.sum(pc, axis=0),
                    a0 + jnp.sum(pc[:, :, :LANES] * vc, axis=0),
                    a1 + jnp.sum(pc[:, :, LANES:] * vc, axis=0))
        return body

    carry = (alpha * l_ref[...], alpha[:, :LANES] * acc_ref[0], alpha[:, LANES:] * acc_ref[1])
    for i in range(pages_per_step):
        carry = lax.fori_loop(0, PAGE_SIZE // kc, accumulate(i), carry)
    l, a0, a1 = carry
    m_ref[...] = m_new
    l_ref[...] = l
    acc_ref[0] = a0
    acc_ref[1] = a1

    @pl.when(p == n_steps - 1)
    def _():
        k_own = _half_norm(zs[0, k_lo:k_lo + N_HEADS], kg[...], lo)
        v_own = zs[0, v_lo:v_lo + N_HEADS]
        s_own = scores(k_own[None])[0] + sbias[...]
        m_fin = jnp.maximum(m_new, s_own)
        a_fin = jnp.exp2(m_new - m_fin)
        p_own = jnp.exp2(s_own - m_fin)
        l_fin = a_fin * l + p_own
        o0 = (a_fin[:, :LANES] * a0 + p_own[:, :LANES] * v_own) / l_fin[:, :LANES]
        o1 = (a_fin[:, LANES:] * a1 + p_own[:, LANES:] * v_own) / l_fin[:, LANES:]
        o = o0 - _lambda_value(lq1, lk1, lq2, lk2, lam_init) * o1
        on = o * lax.rsqrt(jnp.mean(o * o, axis=-1, keepdims=True) + EPS) * sg[...]
        o_ref[0] = on * (1.0 - lam_init)


def _sample_attention(zs, ck, cv, page_table, pbias, sbias, qg2, kg2, lams, sg, lam_init, pages_per_step):
    n_dec = zs.shape[0]
    n_pages = page_table.shape[1]
    n_steps = n_pages // pages_per_step
    pt = page_table.reshape(-1)
    page = lambda i: pl.BlockSpec(
        (None, None, PAGE_SIZE, N_HEADS, LANES),
        lambda r, p, pt, i=i: (0, pt[r * n_pages + p * pages_per_step + i], 0, 0, 0))
    vec = lambda shape: pl.BlockSpec(shape, lambda r, p, pt: (0,) * len(shape))
    out_row = pl.BlockSpec((1, N_HEADS, LANES), lambda r, p, pt: (r, 0, 0))
    rsum = (jnp.arange(LANES)[:, None] // HEAD_DIM == jnp.arange(2 * LANES)[None, :] // LANES).astype(BF16)
    grid_spec = pltpu.PrefetchScalarGridSpec(
        num_scalar_prefetch=1,
        grid=(n_dec, n_steps),
        in_specs=[pl.BlockSpec((1,) + zs.shape[1:], lambda r, p, pt: (r, 0, 0))]
        + [page(i) for i in range(pages_per_step)] + [page(i) for i in range(pages_per_step)]
        + [vec((PAGE_SIZE, N_HEADS, 2 * LANES)), vec((N_HEADS, 2 * LANES)), vec((N_HEADS, 2 * LANES)),
           vec((LANES, 2 * LANES)),
           vec((1, LANES)), vec((1, LANES)),
           vec((1, HEAD_DIM)), vec((1, HEAD_DIM)), vec((1, HEAD_DIM)), vec((1, HEAD_DIM)), vec((1, V_DIM))],
        out_specs=[out_row, out_row, out_row],
        scratch_shapes=[
            pltpu.VMEM((N_HEADS, LANES), F32),
            pltpu.VMEM((pages_per_step * PAGE_SIZE, N_HEADS, 2 * LANES), F32),
            pltpu.VMEM((N_HEADS, 2 * LANES), F32),
            pltpu.VMEM((N_HEADS, 2 * LANES), F32),
            pltpu.VMEM((2, N_HEADS, LANES), F32),
        ],
    )
    return pl.pallas_call(
        functools.partial(_decode_kernel, pages_per_step=pages_per_step, n_steps=n_steps, lam_init=lam_init),
        grid_spec=grid_spec,
        out_shape=[jax.ShapeDtypeStruct((n_dec, N_HEADS, LANES), F32)] * 3,
        compiler_params=_cparams(("parallel", "arbitrary")),
        name="sample_attn",
    )(pt, zs, *([ck] * pages_per_step), *([cv] * pages_per_step),
      pbias, jnp.maximum(jnp.max(pbias, axis=0), 0.0), sbias, rsum, qg2, kg2, *lams, sg)


def _merge_kernel(ya, on, ga, gb, h, wa, wb, wo, n2g, wr, br,
                  h1_ref, xp_ref, topi_ref, gate_ref, rank_ref, cnt_ref, carry,
                  *, tm, tiles_per_batch, t_real, n_dec):
    i = pl.program_id(0)

    @pl.when(i == 0)
    def _():
        carry[...] = jnp.zeros_like(carry)

    y_a = jnp.dot(ya[...], wa[...], preferred_element_type=F32)
    o_cat = jnp.concatenate([on[0, hd] for hd in range(N_HEADS)], axis=1)
    y_b = jnp.dot(o_cat, wb[...], preferred_element_type=F32)
    mix = jax.nn.sigmoid(ga[...]) * y_a + jax.nn.sigmoid(gb[...]) * y_b
    h1 = h[...] + jnp.dot(mix.astype(BF16), wo[...], preferred_element_type=F32)
    h1_ref[...] = h1

    xn = h1 * lax.rsqrt(jnp.mean(h1 * h1, axis=-1, keepdims=True) + EPS) * n2g[...]
    xb = xn.astype(BF16)
    xf = xb.astype(F32)
    bits = pltpu.bitcast(xf, U32)
    half = D_MODEL // 2
    xp_ref[...] = bits[:, :half] | (bits[:, half:] >> 16)

    logits = jnp.dot(xb, wr[...], preferred_element_type=F32) + br[...]

    lane = lax.broadcasted_iota(I32, (tm, LANES), 1)
    work = logits
    vals, idxs, sels = [], [], []
    for _ in range(TOP_K):
        mx = jnp.max(work, axis=-1, keepdims=True)
        idx = jnp.min(jnp.where(work == mx, lane, LANES), axis=-1, keepdims=True)
        sel = lane == idx
        vals.append(mx)
        idxs.append(idx)
        sels.append(sel)
        work = jnp.where(sel, -jnp.inf, work)
    exps = [jnp.exp(v - vals[0]) for v in vals]
    denom = exps[0] + exps[1] + exps[2] + exps[3]

    pos_in_batch = (i % tiles_per_batch) * tm + lax.broadcasted_iota(I32, (tm, 1), 0)
    limit = jnp.where(i // tiles_per_batch == 0, t_real + n_dec, t_real)
    valid = pos_in_batch < limit

    onehot = jnp.zeros((tm, LANES), F32)
    for sel in sels:
        onehot = onehot + jnp.where(sel, 1.0, 0.0)
    onehot = jnp.where(valid, onehot, 0.0)
    rr = lax.broadcasted_iota(I32, (tm, tm), 0)
    cc = lax.broadcasted_iota(I32, (tm, tm), 1)
    lower = jnp.where(rr > cc, 1.0, 0.0).astype(BF16)
    before = jnp.dot(lower, onehot.astype(BF16), preferred_element_type=F32) + carry[...]

    topi = jnp.zeros((tm, LANES), I32)
    gate = jnp.zeros((tm, LANES), F32)
    rank = jnp.zeros((tm, LANES), I32)
    for j in range(TOP_K):
        rj = jnp.sum(jnp.where(sels[j], before, 0.0), axis=-1, keepdims=True).astype(I32)
        topi = jnp.where(lane == j, idxs[j], topi)
        gate = jnp.where(lane == j, exps[j] / denom, gate)
        rank = jnp.where(lane == j, rj, rank)
    topi_ref[...] = topi
    gate_ref[...] = gate
    rank_ref[...] = rank
    carry[...] = carry[...] + jnp.sum(onehot, axis=0, keepdims=True)
    cnt_ref[...] = jnp.broadcast_to(carry[...], cnt_ref.shape)


def _merge(ya2d, on, z2d, h2d, wa, wb, wo, n2g, wr, br, tm, t_pad, t_real, n_dec):
    rows = h2d.shape[0]
    tpb = t_pad // tm
    const = lambda shape: pl.BlockSpec(shape, lambda i: (0,) * len(shape), pipeline_mode=pl.Buffered(1))
    row = lambda w, dt=None: pl.BlockSpec((tm, w), lambda i: (i, 0))
    return pl.pallas_call(
        functools.partial(_merge_kernel, tm=tm, tiles_per_batch=tpb, t_real=t_real, n_dec=n_dec),
        grid=(rows // tm,),
        in_specs=[
            row(D_CONV),
            pl.BlockSpec((1, N_HEADS, tm, LANES), lambda i: (i // tpb, 0, i % tpb, 0)),
            pl.BlockSpec((tm, D_MODEL), lambda i: (i, 3)),
            pl.BlockSpec((tm, D_MODEL), lambda i: (i, 4)),
            row(D_MODEL),
            const((D_CONV, D_MODEL)), const((D_ATTN, D_MODEL)), const((D_MODEL, D_MODEL)),
            const((1, D_MODEL)), const((D_MODEL, LANES)), const((1, LANES)),
        ],
        out_specs=[row(D_MODEL), row(D_MODEL // 2), row(LANES), row(LANES), row(LANES),
                   pl.BlockSpec((8, LANES), lambda i: (0, 0))],
        out_shape=[
            jax.ShapeDtypeStruct((rows, D_MODEL), F32),
            jax.ShapeDtypeStruct((rows, D_MODEL // 2), U32),
            jax.ShapeDtypeStruct((rows, LANES), I32),
            jax.ShapeDtypeStruct((rows, LANES), F32),
            jax.ShapeDtypeStruct((rows, LANES), I32),
            jax.ShapeDtypeStruct((8, LANES), F32),
        ],
        scratch_shapes=[pltpu.VMEM((1, LANES), F32)],
        compiler_params=_cparams(("arbitrary",)),
        name="merge_route",
    )(ya2d, on, z2d, z2d, h2d, wa, wb, wo, n2g, wr, br)


def _dispatch_kernel(pos_ref, zpos_ref, x_ref, xs_hbm, zbuf, sem, zsem, *, tm):
    i = pl.program_id(0)

    @pl.when(i == 0)
    def _():
        zbuf[...] = jnp.zeros_like(zbuf)

        def zstart(e, c):
            pltpu.make_async_copy(zbuf, xs_hbm.at[pl.ds(pl.multiple_of(zpos_ref[e], SUB_ROWS), SUB_ROWS)],
                                  zsem).start()
            return c

        def zwait(e, c):
            pltpu.make_async_copy(zbuf, xs_hbm.at[pl.ds(0, SUB_ROWS)], zsem).wait()
            return c

        lax.fori_loop(0, N_EXPERTS, zstart, 0)
        lax.fori_loop(0, N_EXPERTS, zwait, 0)

    base = i * tm

    def start(t, c):
        for j in range(TOP_K):
            dst = pos_ref[(base + t) * TOP_K + j]
            pltpu.make_async_copy(x_ref.at[pl.ds(t, 1)], xs_hbm.at[pl.ds(dst, 1)], sem).start()
        return c

    def wait(t, c):
        for j in range(TOP_K):
            pltpu.make_async_copy(x_ref.at[pl.ds(0, 1)], xs_hbm.at[pl.ds(0, 1)], sem).wait()
        return c

    lax.fori_loop(0, tm, start, 0, unroll=DMA_LOOP_UNROLL)
    lax.fori_loop(0, tm, wait, 0, unroll=DMA_LOOP_UNROLL)


def _dispatch(pos_flat, zpos, xp, n_slots, tm):
    rows, width = xp.shape
    grid_spec = pltpu.PrefetchScalarGridSpec(
        num_scalar_prefetch=2,
        grid=(rows // tm,),
        in_specs=[pl.BlockSpec((tm, width), lambda i, p, z: (i, 0))],
        out_specs=pl.BlockSpec(memory_space=pl.ANY),
        scratch_shapes=[pltpu.VMEM((SUB_ROWS, width), U32),
                        pltpu.SemaphoreType.DMA(()), pltpu.SemaphoreType.DMA(())],
    )
    return pl.pallas_call(
        functools.partial(_dispatch_kernel, tm=tm),
        grid_spec=grid_spec,
        out_shape=jax.ShapeDtypeStruct((n_slots, width), U32),
        compiler_params=_cparams(("arbitrary",)),
        name="moe_dispatch",
    )(pos_flat, zpos, xp)


def _moe_kernel(item_e, item_start, item_nsub, n_items,
                xs_hbm, w1g, w1u, w2, b1g, b1u, b2,
                ys_hbm, xraw, xa, xb, yacc, sem_in, sem_out, *, nf):
    it = pl.program_id(0)
    f = pl.program_id(1)
    half = D_MODEL // 2

    @pl.when(it < n_items[0])
    def _():
        start = pl.multiple_of(item_start[it], SUB_ROWS)
        nsub = item_nsub[it]

        def in_copy(s):
            r0 = pl.multiple_of(s * SUB_ROWS, SUB_ROWS)
            return pltpu.make_async_copy(xs_hbm.at[pl.ds(start + r0, SUB_ROWS)],
                                         xraw.at[pl.ds(r0, SUB_ROWS)], sem_in.at[s])

        def out_copy(r0, m):
            rows = pl.ds(pl.multiple_of(r0, SUB_ROWS), m)
            return pltpu.make_async_copy(yacc.at[rows], ys_hbm.at[pl.ds(start + r0, m)], sem_out)

        def mlp_rows(r0, m, first, last):
            r0 = pl.multiple_of(r0, SUB_ROWS)
            rows = pl.ds(r0, m)
            if first:
                for k in range(m // SUB_ROWS):
                    sub = pl.ds(r0 + k * SUB_ROWS, SUB_ROWS)
                    in_copy(r0 // SUB_ROWS + k).wait()
                    w = xraw[sub, :]
                    xa[sub, :] = pltpu.bitcast(w & jnp.uint32(0xFFFF0000), F32).astype(BF16)
                    xb[sub, :] = pltpu.bitcast(w << 16, F32).astype(BF16)
            wg = w1g[0].astype(BF16)
            wu = w1u[0].astype(BF16)
            wd = w2[0].astype(BF16)
            a = xa[rows, :]
            b = xb[rows, :]
            hg = (jnp.dot(a, wg[:half], preferred_element_type=F32)
                  + jnp.dot(b, wg[half:], preferred_element_type=F32) + b1g[0])
            hu = (jnp.dot(a, wu[:half], preferred_element_type=F32)
                  + jnp.dot(b, wu[half:], preferred_element_type=F32) + b1u[0])
            g = jnp.minimum(hg, SWIGLU_LIMIT)
            up = jnp.clip(hu, -SWIGLU_LIMIT, SWIGLU_LIMIT)
            act = g * jax.nn.sigmoid(SWIGLU_ALPHA * g) * (up + 1.0)
            y = jnp.dot(act.astype(BF16), wd, preferred_element_type=F32)
            yacc[rows, :] = y + (b2[0] if first else yacc[rows, :])
            if last:
                out_copy(r0, m).start()

        per_big, per_mid = BIG_ROWS // SUB_ROWS, MID_ROWS // SUB_ROWS
        nbig = nsub // per_big
        mid0 = nbig * BIG_ROWS
        nmid = (nsub - nbig * per_big) // per_mid
        tail0 = nbig * per_big + nmid * per_mid

        def sweep(first, last):
            lax.fori_loop(0, nbig, lambda c, u: (mlp_rows(c * BIG_ROWS, BIG_ROWS, first, last), u)[1], 0)
            lax.fori_loop(0, nmid, lambda c, u: (mlp_rows(mid0 + c * MID_ROWS, MID_ROWS, first, last), u)[1], 0)
            lax.fori_loop(tail0, nsub, lambda s, u: (mlp_rows(s * SUB_ROWS, SUB_ROWS, first, last), u)[1], 0)

        @pl.when(f == 0)
        def _():
            lax.fori_loop(0, nsub, lambda s, c: (in_copy(s).start(), c)[1], 0)
            sweep(True, False)

        @pl.when((f > 0) & (f < nf - 1))
        def _():
            sweep(False, False)

        @pl.when(f == nf - 1)
        def _():
            sweep(False, True)
            lax.fori_loop(0, nbig, lambda c, u: (out_copy(c * BIG_ROWS, BIG_ROWS).wait(), u)[1], 0)
            lax.fori_loop(0, nmid, lambda c, u: (out_copy(mid0 + c * MID_ROWS, MID_ROWS).wait(), u)[1], 0)
            lax.fori_loop(tail0, nsub, lambda s, u: (out_copy(s * SUB_ROWS, SUB_ROWS).wait(), u)[1], 0)


def _moe_experts(items, xs, w1, b1, w2, b2, n_slots, rmax, tf):
    item_e, item_start, item_nsub, n_items = items
    ni = item_e.shape[0]
    nf = D_FF // tf
    half = D_MODEL // 2

    def fidx(it, f, n):
        return jnp.where(it < n[0], f, nf - 1)

    grid_spec = pltpu.PrefetchScalarGridSpec(
        num_scalar_prefetch=4,
        grid=(n_items[0], nf),
        in_specs=[
            pl.BlockSpec(memory_space=pl.ANY),
            pl.BlockSpec((1, D_MODEL, tf), lambda it, f, e, s, ns, n: (e[it], 0, fidx(it, f, n))),
            pl.BlockSpec((1, D_MODEL, tf), lambda it, f, e, s, ns, n: (e[it], 0, nf + fidx(it, f, n))),
            pl.BlockSpec((1, tf, D_MODEL), lambda it, f, e, s, ns, n: (e[it], fidx(it, f, n), 0)),
            pl.BlockSpec((1, 1, tf), lambda it, f, e, s, ns, n: (e[it], 0, fidx(it, f, n))),
            pl.BlockSpec((1, 1, tf), lambda it, f, e, s, ns, n: (e[it], 0, nf + fidx(it, f, n))),
            pl.BlockSpec((1, 1, D_MODEL), lambda it, f, e, s, ns, n: (e[it], 0, 0)),
        ],
        out_specs=pl.BlockSpec(memory_space=pl.ANY),
        scratch_shapes=[
            pltpu.VMEM((rmax, half), U32),
            pltpu.VMEM((rmax, half), BF16),
            pltpu.VMEM((rmax, half), BF16),
            pltpu.VMEM((rmax, D_MODEL), F32),
            pltpu.SemaphoreType.DMA((rmax // SUB_ROWS,)), pltpu.SemaphoreType.DMA(()),
        ],
    )
    return pl.pallas_call(
        functools.partial(_moe_kernel, nf=nf),
        grid_spec=grid_spec,
        out_shape=jax.ShapeDtypeStruct((n_slots, D_MODEL), F32),
        compiler_params=_cparams(("arbitrary", "arbitrary")),
        name="moe_experts",
    )(item_e, item_start, item_nsub, n_items, xs, w1, w1, w2,
      b1.reshape(N_EXPERTS, 1, 2 * D_FF), b1.reshape(N_EXPERTS, 1, 2 * D_FF), b2.reshape(N_EXPERTS, 1, D_MODEL))


def _combine_kernel(pos_ref, ys_hbm, h1_hbm, gate_hbm, y_ref, ysamp_ref, ybuf, hbuf, gbuf, sem, hsem,
                    *, tm, tiles_per_batch, t_pad, n_dec, samp_row0):
    i = pl.program_id(0)
    n_tiles = pl.num_programs(0)

    def gather_and_mix(row0, n):
        hcp = pltpu.make_async_copy(h1_hbm.at[pl.ds(row0, n)], hbuf.at[pl.ds(0, n)], hsem)
        gcp = pltpu.make_async_copy(gate_hbm.at[pl.ds(row0, n)], gbuf.at[pl.ds(0, n)], hsem)
        hcp.start()
        gcp.start()

        def start(t, c):
            for j in range(TOP_K):
                src = pos_ref[(row0 + t) * TOP_K + j]
                pltpu.make_async_copy(ys_hbm.at[pl.ds(src, 1)], ybuf.at[j, pl.ds(t, 1)], sem).start()
            return c

        def wait(t, c):
            for j in range(TOP_K):
                pltpu.make_async_copy(ys_hbm.at[pl.ds(0, 1)], ybuf.at[0, pl.ds(0, 1)], sem).wait()
            return c

        lax.fori_loop(0, n, start, 0, unroll=DMA_LOOP_UNROLL)
        hcp.wait()
        gcp.wait()
        lax.fori_loop(0, n, wait, 0, unroll=DMA_LOOP_UNROLL)
        g = gbuf[0:n, :]
        out = hbuf[0:n, :]
        for j in range(TOP_K):
            out = out + g[:, j:j + 1] * ybuf[j, 0:n, :]
        return out

    row0 = (i // tiles_per_batch) * t_pad + N_META + (i % tiles_per_batch) * tm
    y_ref[0] = gather_and_mix(pl.multiple_of(row0, 8), tm)

    @pl.when(i == n_tiles - 1)
    def _():
        ysamp_ref[...] = gather_and_mix(samp_row0, n_dec)


def _combine(pos_flat, ys, h1, gate, n_batch, seq, t_pad, n_dec, samp_row0, tm):
    tpb = seq // tm
    grid_spec = pltpu.PrefetchScalarGridSpec(
        num_scalar_prefetch=1,
        grid=(n_batch * tpb,),
        in_specs=[pl.BlockSpec(memory_space=pl.ANY)] * 3,
        out_specs=[pl.BlockSpec((1, tm, D_MODEL), lambda i, p: (i // tpb, i % tpb, 0)),
                   pl.BlockSpec((n_dec, D_MODEL), lambda i, p: (0, 0))],
        scratch_shapes=[
            pltpu.VMEM((TOP_K, tm, D_MODEL), F32),
            pltpu.VMEM((tm, D_MODEL), F32),
            pltpu.VMEM((tm, LANES), F32),
            pltpu.SemaphoreType.DMA(()), pltpu.SemaphoreType.DMA(()),
        ],
    )
    return pl.pallas_call(
        functools.partial(_combine_kernel, tm=tm, tiles_per_batch=tpb, t_pad=t_pad, n_dec=n_dec,
                          samp_row0=samp_row0),
        grid_spec=grid_spec,
        out_shape=[jax.ShapeDtypeStruct((n_batch, seq, D_MODEL), F32),
                   jax.ShapeDtypeStruct((n_dec, D_MODEL), F32)],
        compiler_params=_cparams(("arbitrary",)),
        name="moe_combine",
    )(pos_flat, ys, h1, gate)


def _t5_bucket(rel):
    n = jnp.maximum(rel, 0)
    max_exact = N_BUCKETS // 2
    nf = jnp.maximum(n, 1).astype(F32)
    large = max_exact + (jnp.log(nf / max_exact) / math.log(MAX_DISTANCE / max_exact)
                         * (N_BUCKETS - max_exact)).astype(I32)
    large = jnp.minimum(large, N_BUCKETS - 1)
    return jnp.where(n < max_exact, n, large)


def _bias_tables(rel_bias, tq):
    shifted = (rel_bias - rel_bias[N_BUCKETS - 1][None]) * LOG2E

    def bias_of(rel, out):
        onehot = jax.nn.one_hot(_t5_bucket(rel), N_BUCKETS, dtype=F32)
        return jnp.einsum("...b,bh->" + out, onehot, shifted, precision=lax.Precision.HIGHEST)

    rel0 = jnp.arange(tq)[:, None] - jnp.arange(tq)[None, :]
    diag = jnp.where((rel0 >= 0)[None], bias_of(rel0, "h..."), NEG_INF)
    tiles = jnp.stack([diag, bias_of(tq + rel0, "h...")])
    rel_last = PAGE_SIZE - jnp.arange(PAGE_SIZE)
    pbias = jnp.broadcast_to(bias_of(rel_last, "...h")[:, :, None], (PAGE_SIZE, N_HEADS, 2 * LANES))
    sbias = jnp.broadcast_to(bias_of(jnp.zeros((), I32), "...h")[:, None], (N_HEADS, 2 * LANES))
    return tiles, pbias, sbias


def _routing_tables(counts, topi, rank, valid, n_trash_rows, rmax, n_items_max, real_slots):
    padded = (counts + SUB_ROWS - 1) // SUB_ROWS * SUB_ROWS
    ends = jnp.cumsum(padded)
    off = ends - padded
    pos = off[topi] + rank
    trash_row = jnp.cumsum(jnp.logical_not(valid).astype(I32)) - 1
    trash = real_slots + trash_row[:, None] * TOP_K + jnp.arange(TOP_K, dtype=I32)[None]
    pos_scatter = jnp.where(valid[:, None], pos, trash).reshape(-1)
    pos_gather = jnp.where(valid[:, None], pos, 0).reshape(-1)
    zero_trash = real_slots + -(-(n_trash_rows * TOP_K) // SUB_ROWS) * SUB_ROWS
    zpos = jnp.where(counts > 0, ends - SUB_ROWS, zero_trash).astype(I32)
    per_e = (padded + rmax - 1) // rmax
    item_end = jnp.cumsum(per_e)
    n_items = item_end[-1]
    t = jnp.arange(n_items_max, dtype=I32)
    tt = jnp.minimum(t, n_items - 1)
    e_of = jnp.minimum(jnp.searchsorted(item_end, tt, side="right"), N_EXPERTS - 1).astype(I32)
    k = tt - (item_end - per_e)[e_of]
    start = off[e_of] + k * rmax
    nrows = jnp.minimum(rmax, padded[e_of] - k * rmax)
    nsub = jnp.where(t < n_items, nrows // SUB_ROWS, 0).astype(I32)
    items = (e_of, start.astype(I32), nsub, n_items.reshape(1).astype(I32))
    return pos_scatter.astype(I32), pos_gather.astype(I32), zpos, items, zero_trash + SUB_ROWS


def kernel(x_prompt, x_sample, cache_k, cache_v, state_conv, page_table, meta_tokens, rel_bias, norm1_g, w_in,
           conv_w, q_norm_g, k_norm_g, lambda_q1, lambda_k1, lambda_q2, lambda_k2, subln_g, w_branch_a,
           w_branch_b, w_out, norm2_g, w_router, b_router, w_mlp1, b_mlp1, w_mlp2, b_mlp2):
    n_batch, seq, _ = x_prompt.shape
    n_dec, t_dec, _ = x_sample.shape
    depth = cache_k.shape[0]
    assert depth == 1 and t_dec == 1 and n_dec == 8
    t_real = seq + N_META
    tq = 384
    t_pad = -(-(t_real + n_dec) // tq) * tq
    rows = n_batch * t_pad
    samp_row0 = t_real
    assert samp_row0 % 8 == 0 and t_pad % 24 == 0
    lam_init = 0.8 - 0.6 * math.exp(-0.3 * 0)

    z2d, h2d = _in_proj(x_prompt, meta_tokens, x_sample.reshape(n_dec, D_MODEL), norm1_g[0], w_in[0],
                        t_pad, tm=t_pad // 3, tn=512)
    z3 = z2d.reshape(n_batch, t_pad, D_IN)

    qg2 = jnp.tile(q_norm_g[0], 2).reshape(1, LANES)
    kg2 = jnp.tile(k_norm_g[0], 2).reshape(1, LANES)
    sg = subln_g[0].reshape(1, V_DIM)
    lams = [v[0].reshape(1, HEAD_DIM) for v in (lambda_q1, lambda_k1, lambda_q2, lambda_k2)]
    bias_tiles, pbias, sbias = _bias_tables(rel_bias, tq)

    ya, k_p, v_p, qs, kb, vb, ust = _prep(z3, conv_w[0], qg2, kg2, n_batch, t_pad, t_real, tm=tq)
    on = _prompt_attention(qs, kb, vb, bias_tiles, lams, sg, lam_init, t_pad, tq)

    row_block = samp_row0 // n_dec
    ya_s, u_s = _sample_conv(z2d, state_conv[0, :, 0], state_conv[0, :, 1], conv_w[0], row_block, n_dec)
    zs = z2d[samp_row0:samp_row0 + n_dec].reshape(n_dec, D_IN // LANES, LANES)
    o_s, k_s, v_s = _sample_attention(zs, cache_k, cache_v, page_table, pbias, sbias, qg2, kg2, lams, sg,
                                      lam_init, pages_per_step=8)

    ya = ya.at[0, samp_row0:samp_row0 + n_dec].set(ya_s.astype(BF16))
    on = on.at[0, :, samp_row0:samp_row0 + n_dec].set(o_s.transpose(1, 0, 2).astype(BF16))

    wr = jnp.pad(w_router[0], ((0, 0), (0, LANES - N_EXPERTS)))
    br = jnp.concatenate([b_router[0], jnp.full((LANES - N_EXPERTS,), NEG_INF, F32)]).reshape(1, LANES)
    h1, xp, topi, gate, rank, cnt = _merge(
        ya.reshape(rows, D_CONV), on, z2d, h2d,
        w_branch_a[0].astype(BF16), w_branch_b[0].astype(BF16), w_out[0].astype(BF16),
        norm2_g[0].reshape(1, D_MODEL), wr.astype(BF16), br, tm=tq, t_pad=t_pad, t_real=t_real, n_dec=n_dec)

    rmax = 1536
    n_assign = (n_batch * t_real + n_dec) * TOP_K
    real_slots = -(-(n_assign + N_EXPERTS * (SUB_ROWS - 1)) // SUB_ROWS) * SUB_ROWS
    n_items_max = N_EXPERTS + real_slots // rmax
    pos_in_batch = jnp.arange(rows, dtype=I32) % t_pad
    valid = (pos_in_batch < t_real) | ((jnp.arange(rows) < t_pad) & (pos_in_batch < t_real + n_dec))
    pos_scatter, pos_gather, zpos, items, n_slots = _routing_tables(
        cnt[0, :N_EXPERTS].astype(I32), topi[:, :TOP_K], rank[:, :TOP_K], valid,
        rows - n_assign // TOP_K, rmax, n_items_max, real_slots)

    xs = _dispatch(pos_scatter, zpos, xp, n_slots, tm=tq)
    ys = _moe_experts(items, xs, w_mlp1[0], b_mlp1[0], w_mlp2[0], b_mlp2[0], n_slots, rmax, tf=256)
    y_prompt, y_samp = _combine(pos_gather, ys, h1, gate, n_batch, seq, t_pad, n_dec, samp_row0, tm=512)

    return (y_prompt,
            y_samp.reshape(n_dec, 1, D_MODEL),
            k_p.reshape(1, n_batch, t_real, N_HEADS, 2 * HEAD_DIM),
            v_p.reshape(1, n_batch, t_real, N_HEADS, V_DIM),
            ust[:, 6:8].reshape(1, n_batch, CONV_K - 1, D_CONV),
            k_s.reshape(1, n_dec, 1, N_HEADS, 2 * HEAD_DIM),
            v_s.reshape(1, n_dec, 1, N_HEADS, V_DIM),
            jnp.stack([state_conv[0, :, 1], u_s], axis=1).reshape(1, n_dec, CONV_K - 1, D_CONV))
```

```python
import functools
import math

import jax
import jax.numpy as jnp
import numpy as np
from jax import lax
from jax.experimental import pallas as pl
from jax.experimental.pallas import tpu as pltpu

D_MODEL = 2048
N_META = 16
D_CONV = 1024
CONV_K = 3
N_HEADS = 8
HEAD_DIM = 64
V_DIM = 128
D_QK = 1024
D_ATTN = 1024
D_IN = 10240
N_BUCKETS = 32
MAX_DISTANCE = 128
N_EXPERTS = 32
TOP_K = 4
D_FF = 2048
SWIGLU_LIMIT = 7.0
SWIGLU_ALPHA = 1.702
EPS = 1e-6
NEG_INF = -1e30
PAGE_SIZE = 128

F32 = jnp.float32
BF16 = jnp.bfloat16
I32 = jnp.int32
U32 = jnp.uint32

LANES = 128
SUB_ROWS = 128
BIG_ROWS = 1024
MID_ROWS = 512
ATTN_ROW_CHUNK = 64
LOG2E = 1.4426950408889634
DMA_LOOP_UNROLL = 4
DECODE_KEY_CHUNK = 16
D_QK_OFF = 3 * D_CONV
VMEM_LIMIT = 56 * 1024 * 1024


def _cparams(sem, vmem=VMEM_LIMIT):
    return pltpu.CompilerParams(dimension_semantics=sem, vmem_limit_bytes=vmem)


def _in_proj_kernel(xp_hbm, meta_ref, xs_ref, g_ref, w_ref, z_ref, h_ref, xn_ref, sems,
                    *, tm, tiles_per_batch, seq, n_dec):
    i = pl.program_id(0)
    b = i // tiles_per_batch
    t_real = seq + N_META

    @pl.when(pl.program_id(1) == 0)
    def _():
        for t in range(tiles_per_batch):
            @pl.when(i % tiles_per_batch == t)
            def _(t=t):
                lo = t * tm
                p0, p1 = max(lo, N_META), min(lo + tm, t_real)
                n_chunks = 4
                step = -(-(p1 - p0) // (8 * n_chunks)) * 8
                copies = []
                for c in range(n_chunks):
                    r0, r1 = p0 + c * step, min(p0 + (c + 1) * step, p1)
                    copies.append(pltpu.make_async_copy(
                        xp_hbm.at[b, pl.ds(r0 - N_META, r1 - r0)], h_ref.at[pl.ds(r0 - lo, r1 - r0)], sems.at[c]))
                for cp in copies:
                    cp.start()
                if lo < N_META:
                    h_ref[0:N_META, :] = meta_ref[...]
                if lo + tm > t_real:
                    h_ref[t_real - lo:tm, :] = jnp.zeros((lo + tm - t_real, D_MODEL), F32)

                    @pl.when(b == 0)
                    def _():
                        h_ref[t_real - lo:t_real - lo + n_dec, :] = xs_ref[...]
                for cp in copies:
                    cp.wait()

        x = h_ref[...]
        ms = jnp.mean(x * x, axis=-1, keepdims=True)
        xn_ref[...] = (x * lax.rsqrt(ms + EPS) * g_ref[...]).astype(BF16)

    z_ref[...] = jnp.dot(xn_ref[...], w_ref[...], preferred_element_type=F32)


def _in_proj(x_prompt, meta_tokens, x_sample2d, g, w, t_pad, tm, tn):
    n_batch, seq, _ = x_prompt.shape
    n_dec = x_sample2d.shape[0]
    rows = n_batch * t_pad
    const = lambda shape: pl.BlockSpec(shape, lambda i, j: (0,) * len(shape))
    return pl.pallas_call(
        functools.partial(_in_proj_kernel, tm=tm, tiles_per_batch=t_pad // tm, seq=seq, n_dec=n_dec),
        grid=(rows // tm, D_IN // tn),
        in_specs=[
            pl.BlockSpec(memory_space=pl.ANY),
            const((N_META, D_MODEL)), const((n_dec, D_MODEL)), const((1, D_MODEL)),
            pl.BlockSpec((D_MODEL, tn), lambda i, j: (0, j)),
        ],
        out_specs=[pl.BlockSpec((tm, tn), lambda i, j: (i, j)),
                   pl.BlockSpec((tm, D_MODEL), lambda i, j: (i, 0))],
        out_shape=[jax.ShapeDtypeStruct((rows, D_IN), F32),
                   jax.ShapeDtypeStruct((rows, D_MODEL), F32)],
        scratch_shapes=[pltpu.VMEM((tm, D_MODEL), BF16), pltpu.SemaphoreType.DMA((4,))],
        compiler_params=_cparams(("parallel", "arbitrary")),
        name="in_proj",
    )(x_prompt, meta_tokens, x_sample2d, g.reshape(1, D_MODEL), w)


def _half_norm(x, g2, lo):
    t = x * x
    s_lo = jnp.sum(jnp.where(lo, t, 0.0), axis=-1, keepdims=True)
    s_hi = jnp.sum(jnp.where(lo, 0.0, t), axis=-1, keepdims=True)
    inv = jnp.where(lo, lax.rsqrt(s_lo * (1.0 / HEAD_DIM) + EPS), lax.rsqrt(s_hi * (1.0 / HEAD_DIM) + EPS))
    return x * inv * g2


def _prep_kernel(zb, zc, zx, zq, zk, zv, cw, qg, kg,
                 ya_ref, kout, vout, qs, kb, vb, ust, carry, *, tm, state_tile, state_row):
    i = pl.program_id(1)

    @pl.when(i == 0)
    def _():
        carry[...] = jnp.zeros_like(carry)

    u = zc[0] * zx[0]
    prev = carry[...]
    row = lax.broadcasted_iota(I32, (tm, 1), 0)
    u1 = jnp.where(row == 0, prev[7:8], pltpu.roll(u, 1, 0))
    u2 = jnp.where(row == 0, prev[6:7], jnp.where(row == 1, prev[7:8], pltpu.roll(u, 2, 0)))
    y = cw[0:1] * u2 + cw[1:2] * u1 + cw[2:3] * u
    ya_ref[0] = (zb[0] * y).astype(BF16)
    carry[...] = u[tm - 8:tm]

    @pl.when(i == state_tile)
    def _():
        ust[0] = u[state_row:state_row + 8]

    lo = lax.broadcasted_iota(I32, (1, LANES), 1) < HEAD_DIM
    for h in range(N_HEADS):
        sl = slice(h * LANES, (h + 1) * LANES)
        qn = _half_norm(zq[0, :, sl], qg[...], lo) * (LOG2E * HEAD_DIM ** -0.5)
        qs[0, 0, h] = jnp.where(lo, qn, 0.0).astype(BF16)
        qs[0, 1, h] = jnp.where(lo, 0.0, qn).astype(BF16)
        kn = _half_norm(zk[0, :, sl], kg[...], lo)
        kout[0, :, sl] = kn
        kb[0, h] = kn.astype(BF16)
        vb[0, h] = zv[0, :, sl].astype(BF16)
    vout[0] = zv[0]


def _prep(z3, conv_w, qg2, kg2, n_batch, t_pad, t_real, tm):
    nt = t_pad // tm
    state_tile = (t_real - 2) // tm
    state_row = ((t_real - 2) % tm) // 8 * 8
    sec = lambda s: pl.BlockSpec((1, tm, 1024), lambda b, i, s=s: (b, i, s))
    small = lambda shape: pl.BlockSpec(shape, lambda b, i: (0,) * len(shape))
    return pl.pallas_call(
        functools.partial(_prep_kernel, tm=tm, state_tile=state_tile, state_row=state_row),
        grid=(n_batch, nt),
        in_specs=[sec(0), sec(1), sec(2), sec(3), sec(4), sec(5),
                  small((CONV_K, D_CONV)), small((1, LANES)), small((1, LANES))],
        out_specs=[
            pl.BlockSpec((1, tm, D_CONV), lambda b, i: (b, i, 0)),
            pl.BlockSpec((1, tm, D_QK), lambda b, i: (b, i, 0)),
            pl.BlockSpec((1, tm, D_ATTN), lambda b, i: (b, i, 0)),
            pl.BlockSpec((1, 2, N_HEADS, tm, LANES), lambda b, i: (b, 0, 0, i, 0)),
            pl.BlockSpec((1, N_HEADS, tm, LANES), lambda b, i: (b, 0, i, 0)),
            pl.BlockSpec((1, N_HEADS, tm, LANES), lambda b, i: (b, 0, i, 0)),
            pl.BlockSpec((1, 8, D_CONV), lambda b, i: (b, 0, 0)),
        ],
        out_shape=[
            jax.ShapeDtypeStruct((n_batch, t_pad, D_CONV), BF16),
            jax.ShapeDtypeStruct((n_batch, t_real, D_QK), F32),
            jax.ShapeDtypeStruct((n_batch, t_real, D_ATTN), F32),
            jax.ShapeDtypeStruct((n_batch, 2, N_HEADS, t_pad, LANES), BF16),
            jax.ShapeDtypeStruct((n_batch, N_HEADS, t_pad, LANES), BF16),
            jax.ShapeDtypeStruct((n_batch, N_HEADS, t_pad, LANES), BF16),
            jax.ShapeDtypeStruct((n_batch, 8, D_CONV), F32),
        ],
        scratch_shapes=[pltpu.VMEM((8, D_CONV), F32)],
        compiler_params=_cparams(("parallel", "arbitrary")),
        name="prep",
    )(z3, z3, z3, z3, z3, z3, conv_w, qg2, kg2)


def _lambda_value(lq1, lk1, lq2, lk2, lam_init):
    a = jnp.sum(lq1[...] * lk1[...], axis=-1, keepdims=True)
    b = jnp.sum(lq2[...] * lk2[...], axis=-1, keepdims=True)
    return jnp.exp(a) - jnp.exp(b) + lam_init


def _attn_kernel(qi_tab, ki_tab, q_ref, k_ref, v_ref, bias_ref, lq1, lk1, lq2, lk2, sg,
                 o_ref, m_ref, l_ref, acc_ref, s_ref, p_ref, a_ref, *, tq, tk, lam_init):
    step = pl.program_id(1)
    qi = qi_tab[step]
    ki = ki_tab[step]
    rc = ATTN_ROW_CHUNK

    @pl.when(ki == 0)
    def _():
        m_ref[...] = jnp.full_like(m_ref, NEG_INF)
        l_ref[...] = jnp.zeros_like(l_ref)
        acc_ref[...] = jnp.zeros_like(acc_ref)

    def score(h, slot):
        q = q_ref[0, :, h].reshape(2 * tq, LANES)
        s_ref[slot] = lax.dot_general(q, k_ref[0, h], (((1,), (1,)), ((), ())), preferred_element_type=F32)

    def softmax_update(h, slot, near):
        for c in range(2 * tq // rc):
            rows = pl.ds(c * rc, rc)
            s = s_ref[slot, rows, :]
            if near:
                s = s + bias_ref[0, h, pl.ds((c % (tq // rc)) * rc, rc), :]
            m_prev = m_ref[h, rows, :]
            m_new = jnp.maximum(m_prev, jnp.max(s, axis=-1, keepdims=True))
            alpha = jnp.exp2(m_prev - m_new)
            p = jnp.exp2(s - jnp.concatenate([m_new] * (tk // LANES), axis=1))
            l_ref[h, rows, :] = alpha * l_ref[h, rows, :] + jnp.sum(p, axis=-1, keepdims=True)
            m_ref[h, rows, :] = m_new
            a_ref[slot, rows, :] = alpha
            p_ref[slot, rows, :] = p.astype(BF16)

    def weighted_values(h, slot):
        acc_ref[h] = a_ref[slot] * acc_ref[h] + jnp.dot(p_ref[slot], v_ref[0, h], preferred_element_type=F32)

    def head_pair(hp, near):
        for slot in range(2):
            score(2 * hp + slot, slot)
            softmax_update(2 * hp + slot, slot, near)
            weighted_values(2 * hp + slot, slot)

    @pl.when(ki >= qi - 1)
    def _():
        lax.fori_loop(0, N_HEADS // 2, lambda hp, c: (head_pair(hp, True), c)[1], 0)

    @pl.when(ki < qi - 1)
    def _():
        lax.fori_loop(0, N_HEADS // 2, lambda hp, c: (head_pair(hp, False), c)[1], 0)

    @pl.when(ki == qi)
    def _():
        lam = _lambda_value(lq1, lk1, lq2, lk2, lam_init)

        def fin(h, c):
            acc = acc_ref[h]
            l = l_ref[h]
            o = acc[:tq] / l[:tq] - lam * (acc[tq:] / l[tq:])
            on = o * lax.rsqrt(jnp.mean(o * o, axis=-1, keepdims=True) + EPS) * sg[...]
            o_ref[0, h] = (on * (1.0 - lam_init)).astype(BF16)
            return c

        lax.fori_loop(0, N_HEADS, fin, 0)


def _prompt_attention(qs, kb, vb, bias_tiles, lams, sg, lam_init, t_pad, tq):
    n_batch = qs.shape[0]
    nq = t_pad // tq
    pairs = [(qi, ki) for qi in range(nq) for ki in range(qi + 1)]
    qi_tab = jnp.asarray(np.array([p[0] for p in pairs], np.int32))
    ki_tab = jnp.asarray(np.array([p[1] for p in pairs], np.int32))
    vec = lambda n: pl.BlockSpec((1, n), lambda b, s, qt, kt: (0, 0))
    grid_spec = pltpu.PrefetchScalarGridSpec(
        num_scalar_prefetch=2,
        grid=(n_batch, len(pairs)),
        in_specs=[
            pl.BlockSpec((1, 2, N_HEADS, tq, LANES), lambda b, s, qt, kt: (b, 0, 0, qt[s], 0)),
            pl.BlockSpec((1, N_HEADS, tq, LANES), lambda b, s, qt, kt: (b, 0, kt[s], 0)),
            pl.BlockSpec((1, N_HEADS, tq, LANES), lambda b, s, qt, kt: (b, 0, kt[s], 0)),
            pl.BlockSpec((1, N_HEADS, tq, tq),
                         lambda b, s, qt, kt: (jnp.minimum(qt[s] - kt[s], 1), 0, 0, 0)),
            vec(HEAD_DIM), vec(HEAD_DIM), vec(HEAD_DIM), vec(HEAD_DIM), vec(V_DIM),
        ],
        out_specs=pl.BlockSpec((1, N_HEADS, tq, LANES), lambda b, s, qt, kt: (b, 0, qt[s], 0)),
        scratch_shapes=[
            pltpu.VMEM((N_HEADS, 2 * tq, LANES), F32),
            pltpu.VMEM((N_HEADS, 2 * tq, LANES), F32),
            pltpu.VMEM((N_HEADS, 2 * tq, LANES), F32),
            pltpu.VMEM((2, 2 * tq, tq), F32),
            pltpu.VMEM((2, 2 * tq, tq), BF16),
            pltpu.VMEM((2, 2 * tq, LANES), F32),
        ],
    )
    return pl.pallas_call(
        functools.partial(_attn_kernel, tq=tq, tk=tq, lam_init=lam_init),
        grid_spec=grid_spec,
        out_shape=jax.ShapeDtypeStruct((n_batch, N_HEADS, t_pad, LANES), BF16),
        compiler_params=_cparams(("parallel", "arbitrary")),
        name="prompt_attn",
    )(qi_tab, ki_tab, qs, kb, vb, bias_tiles, *lams, sg)


def _sample_conv_kernel(zb, zc, zx, s0, s1, cw, ya_ref, u_ref):
    u = zc[...] * zx[...]
    y = cw[0:1] * s0[...] + cw[1:2] * s1[...] + cw[2:3] * u
    ya_ref[...] = zb[...] * y
    u_ref[...] = u


def _sample_conv(z2d, s0, s1, conv_w, row_block, n):
    sec = lambda s: pl.BlockSpec((n, D_CONV), lambda i, s=s: (row_block, s))
    full = lambda r: pl.BlockSpec((r, D_CONV), lambda i: (0, 0))
    return pl.pallas_call(
        _sample_conv_kernel,
        grid=(1,),
        in_specs=[sec(0), sec(1), sec(2), full(n), full(n), full(CONV_K)],
        out_specs=[full(n), full(n)],
        out_shape=[jax.ShapeDtypeStruct((n, D_CONV), F32)] * 2,
        name="sample_conv",
    )(z2d, z2d, z2d, s0, s1, conv_w)


def _decode_kernel(pt_ref, zs, *rest, pages_per_step, n_steps, lam_init):
    kpages = rest[:pages_per_step]
    vpages = rest[pages_per_step:2 * pages_per_step]
    (pbias, pbias_max, sbias, rsum, qg, kg, lq1, lk1, lq2, lk2, sg,
     o_ref, kout, vout, q_sc, s_ref, m_ref, l_ref, acc_ref) = rest[2 * pages_per_step:]
    p = pl.program_id(1)
    kc = DECODE_KEY_CHUNK
    q_lo, k_lo, v_lo = D_QK_OFF // LANES, (D_QK_OFF + D_QK) // LANES, (D_QK_OFF + 2 * D_QK) // LANES
    lo = lax.broadcasted_iota(I32, (1, LANES), 1) < HEAD_DIM

    @pl.when(p == 0)
    def _():
        q_sc[...] = _half_norm(zs[0, q_lo:q_lo + N_HEADS], qg[...], lo) * (LOG2E * HEAD_DIM ** -0.5)
        kout[0] = _half_norm(zs[0, k_lo:k_lo + N_HEADS], kg[...], lo)
        vout[0] = zs[0, v_lo:v_lo + N_HEADS]
        m_ref[...] = jnp.full_like(m_ref, NEG_INF)
        l_ref[...] = jnp.zeros_like(l_ref)
        acc_ref[...] = jnp.zeros_like(acc_ref)

    q = q_sc[...]

    def scores(k):
        n = k.shape[0]
        t = (k * q[None]).reshape(n * N_HEADS, LANES).astype(BF16)
        return jnp.dot(t, rsum[...], preferred_element_type=F32).reshape(n, N_HEADS, 2 * LANES)

    m_prev = m_ref[...]
    m_new = m_prev
    for i in range(pages_per_step):
        s = scores(kpages[i][...])
        s_ref[pl.ds(i * PAGE_SIZE, PAGE_SIZE)] = s
        m_new = jnp.maximum(m_new, jnp.max(s, axis=0))

    m_new = m_new + jnp.where(p == n_steps - 1, pbias_max[...], 0.0)

    @pl.when(p == n_steps - 1)
    def _():
        last = pl.ds((pages_per_step - 1) * PAGE_SIZE, PAGE_SIZE)
        s_ref[last] = s_ref[last] + pbias[...]

    alpha = jnp.exp2(m_prev - m_new)

    def accumulate(i):
        def body(c, carry):
            l, a0, a1 = carry
            pc = jnp.exp2(s_ref[pl.ds(i * PAGE_SIZE + c * kc, kc)] - m_new[None])
            vc = vpages[i][pl.ds(c * kc, kc)]
            return (l + jnp.sum(pc, axis=0),
                    a0 + jnp.sum(pc[:, :, :LANES] * vc, axis=0),
                    a1 + jnp.sum(pc[:, :, LANES:] * vc, axis=0))
        return body

    carry = (alpha * l_ref[...], alpha[:, :LANES] * acc_ref[0], alpha[:, LANES:] * acc_ref[1])
    for i in range(pages_per_step):
        carry = lax.fori_loop(0, PAGE_SIZE // kc, accumulate(i), carry)
    l, a0, a1 = carry
    m_ref[...] = m_new
    l_ref[...] = l
    acc_ref[0] = a0
    acc_ref[1] = a1

    @pl.when(p == n_steps - 1)
    def _():
        k_own = _half_norm(zs[0, k_lo:k_lo + N_HEADS], kg[...], lo)
        v_own = zs[0, v_lo:v_lo + N_HEADS]
        s_own = scores(k_own[None])[0] + sbias[...]
        m_fin = jnp.maximum(m_new, s_own)
        a_fin = jnp.exp2(m_new - m_fin)
        p_own = jnp.exp2(s_own - m_fin)
        l_fin = a_fin * l + p_own
        o0 = (a_fin[:, :LANES] * a0 + p_own[:, :LANES] * v_own) / l_fin[:, :LANES]
        o1 = (a_fin[:, LANES:] * a1 + p_own[:, LANES:] * v_own) / l_fin[:, LANES:]
        o = o0 - _lambda_value(lq1, lk1, lq2, lk2, lam_init) * o1
        on = o * lax.rsqrt(jnp.mean(o * o, axis=-1, keepdims=True) + EPS) * sg[...]
        o_ref[0] = on * (1.0 - lam_init)


def _sample_attention(zs, ck, cv, page_table, pbias, sbias, qg2, kg2, lams, sg, lam_init, pages_per_step):
    n_dec = zs.shape[0]
    n_pages = page_table.shape[1]
    n_steps = n_pages // pages_per_step
    pt = page_table.reshape(-1)
    page = lambda i: pl.BlockSpec(
        (None, None, PAGE_SIZE, N_HEADS, LANES),
        lambda r, p, pt, i=i: (0, pt[r * n_pages + p * pages_per_step + i], 0, 0, 0))
    vec = lambda shape: pl.BlockSpec(shape, lambda r, p, pt: (0,) * len(shape))
    out_row = pl.BlockSpec((1, N_HEADS, LANES), lambda r, p, pt: (r, 0, 0))
    rsum = (jnp.arange(LANES)[:, None] // HEAD_DIM == jnp.arange(2 * LANES)[None, :] // LANES).astype(BF16)
    grid_spec = pltpu.PrefetchScalarGridSpec(
        num_scalar_prefetch=1,
        grid=(n_dec, n_steps),
        in_specs=[pl.BlockSpec((1,) + zs.shape[1:], lambda r, p, pt: (r, 0, 0))]
        + [page(i) for i in range(pages_per_step)] + [page(i) for i in range(pages_per_step)]
        + [vec((PAGE_SIZE, N_HEADS, 2 * LANES)), vec((N_HEADS, 2 * LANES)), vec((N_HEADS, 2 * LANES)),
           vec((LANES, 2 * LANES)),
           vec((1, LANES)), vec((1, LANES)),
           vec((1, HEAD_DIM)), vec((1, HEAD_DIM)), vec((1, HEAD_DIM)), vec((1, HEAD_DIM)), vec((1, V_DIM))],
        out_specs=[out_row, out_row, out_row],
        scratch_shapes=[
            pltpu.VMEM((N_HEADS, LANES), F32),
            pltpu.VMEM((pages_per_step * PAGE_SIZE, N_HEADS, 2 * LANES), F32),
            pltpu.VMEM((N_HEADS, 2 * LANES), F32),
            pltpu.VMEM((N_HEADS, 2 * LANES), F32),
            pltpu.VMEM((2, N_HEADS, LANES), F32),
        ],
    )
    return pl.pallas_call(
        functools.partial(_decode_kernel, pages_per_step=pages_per_step, n_steps=n_steps, lam_init=lam_init),
        grid_spec=grid_spec,
        out_shape=[jax.ShapeDtypeStruct((n_dec, N_HEADS, LANES), F32)] * 3,
        compiler_params=_cparams(("parallel", "arbitrary")),
        name="sample_attn",
    )(pt, zs, *([ck] * pages_per_step), *([cv] * pages_per_step),
      pbias, jnp.maximum(jnp.max(pbias, axis=0), 0.0), sbias, rsum, qg2, kg2, *lams, sg)


def _merge_kernel(ya, on, ga, gb, h, wa, wb, wo, n2g, wr, br,
                  h1_ref, xp_ref, topi_ref, gate_ref, rank_ref, cnt_ref, carry,
                  *, tm, tiles_per_batch, t_real, n_dec):
    i = pl.program_id(0)

    @pl.when(i == 0)
    def _():
        carry[...] = jnp.zeros_like(carry)

    y_a = jnp.dot(ya[...], wa[...], preferred_element_type=F32)
    o_cat = jnp.concatenate([on[0, hd] for hd in range(N_HEADS)], axis=1)
    y_b = jnp.dot(o_cat, wb[...], preferred_element_type=F32)
    mix = jax.nn.sigmoid(ga[...]) * y_a + jax.nn.sigmoid(gb[...]) * y_b
    h1 = h[...] + jnp.dot(mix.astype(BF16), wo[...], preferred_element_type=F32)
    h1_ref[...] = h1

    xn = h1 * lax.rsqrt(jnp.mean(h1 * h1, axis=-1, keepdims=True) + EPS) * n2g[...]
    xb = xn.astype(BF16)
    xf = xb.astype(F32)
    bits = pltpu.bitcast(xf, U32)
    half = D_MODEL // 2
    xp_ref[...] = bits[:, :half] | (bits[:, half:] >> 16)

    logits = jnp.dot(xb, wr[...], preferred_element_type=F32) + br[...]

    lane = lax.broadcasted_iota(I32, (tm, LANES), 1)
    work = logits
    vals, idxs, sels = [], [], []
    for _ in range(TOP_K):
        mx = jnp.max(work, axis=-1, keepdims=True)
        idx = jnp.min(jnp.where(work == mx, lane, LANES), axis=-1, keepdims=True)
        sel = lane == idx
        vals.append(mx)
        idxs.append(idx)
        sels.append(sel)
        work = jnp.where(sel, -jnp.inf, work)
    exps = [jnp.exp(v - vals[0]) for v in vals]
    denom = exps[0] + exps[1] + exps[2] + exps[3]

    pos_in_batch = (i % tiles_per_batch) * tm + lax.broadcasted_iota(I32, (tm, 1), 0)
    limit = jnp.where(i // tiles_per_batch == 0, t_real + n_dec, t_real)
    valid = pos_in_batch < limit

    onehot = jnp.zeros((tm, LANES), F32)
    for sel in sels:
        onehot = onehot + jnp.where(sel, 1.0, 0.0)
    onehot = jnp.where(valid, onehot, 0.0)
    rr = lax.broadcasted_iota(I32, (tm, tm), 0)
    cc = lax.broadcasted_iota(I32, (tm, tm), 1)
    lower = jnp.where(rr > cc, 1.0, 0.0).astype(BF16)
    before = jnp.dot(lower, onehot.astype(BF16), preferred_element_type=F32) + carry[...]

    topi = jnp.zeros((tm, LANES), I32)
    gate = jnp.zeros((tm, LANES), F32)
    rank = jnp.zeros((tm, LANES), I32)
    for j in range(TOP_K):
        rj = jnp.sum(jnp.where(sels[j], before, 0.0), axis=-1, keepdims=True).astype(I32)
        topi = jnp.where(lane == j, idxs[j], topi)
        gate = jnp.where(lane == j, exps[j] / denom, gate)
        rank = jnp.where(lane == j, rj, rank)
    topi_ref[...] = topi
    gate_ref[...] = gate
    rank_ref[...] = rank
    carry[...] = carry[...] + jnp.sum(onehot, axis=0, keepdims=True)
    cnt_ref[...] = jnp.broadcast_to(carry[...], cnt_ref.shape)


def _merge(ya2d, on, z2d, h2d, wa, wb, wo, n2g, wr, br, tm, t_pad, t_real, n_dec):
    rows = h2d.shape[0]
    tpb = t_pad // tm
    const = lambda shape: pl.BlockSpec(shape, lambda i: (0,) * len(shape), pipeline_mode=pl.Buffered(1))
    row = lambda w, dt=None: pl.BlockSpec((tm, w), lambda i: (i, 0))
    return pl.pallas_call(
        functools.partial(_merge_kernel, tm=tm, tiles_per_batch=tpb, t_real=t_real, n_dec=n_dec),
        grid=(rows // tm,),
        in_specs=[
            row(D_CONV),
            pl.BlockSpec((1, N_HEADS, tm, LANES), lambda i: (i // tpb, 0, i % tpb, 0)),
            pl.BlockSpec((tm, D_MODEL), lambda i: (i, 3)),
            pl.BlockSpec((tm, D_MODEL), lambda i: (i, 4)),
            row(D_MODEL),
            const((D_CONV, D_MODEL)), const((D_ATTN, D_MODEL)), const((D_MODEL, D_MODEL)),
            const((1, D_MODEL)), const((D_MODEL, LANES)), const((1, LANES)),
        ],
        out_specs=[row(D_MODEL), row(D_MODEL // 2), row(LANES), row(LANES), row(LANES),
                   pl.BlockSpec((8, LANES), lambda i: (0, 0))],
        out_shape=[
            jax.ShapeDtypeStruct((rows, D_MODEL), F32),
            jax.ShapeDtypeStruct((rows, D_MODEL // 2), U32),
            jax.ShapeDtypeStruct((rows, LANES), I32),
            jax.ShapeDtypeStruct((rows, LANES), F32),
            jax.ShapeDtypeStruct((rows, LANES), I32),
            jax.ShapeDtypeStruct((8, LANES), F32),
        ],
        scratch_shapes=[pltpu.VMEM((1, LANES), F32)],
        compiler_params=_cparams(("arbitrary",)),
        name="merge_route",
    )(ya2d, on, z2d, z2d, h2d, wa, wb, wo, n2g, wr, br)


def _dispatch_kernel(pos_ref, zpos_ref, x_ref, xs_hbm, zbuf, sem, zsem, *, tm):
    i = pl.program_id(0)

    @pl.when(i == 0)
    def _():
        zbuf[...] = jnp.zeros_like(zbuf)

        def zstart(e, c):
            pltpu.make_async_copy(zbuf, xs_hbm.at[pl.ds(pl.multiple_of(zpos_ref[e], SUB_ROWS), SUB_ROWS)],
                                  zsem).start()
            return c

        def zwait(e, c):
            pltpu.make_async_copy(zbuf, xs_hbm.at[pl.ds(0, SUB_ROWS)], zsem).wait()
            return c

        lax.fori_loop(0, N_EXPERTS, zstart, 0)
        lax.fori_loop(0, N_EXPERTS, zwait, 0)

    base = i * tm

    def start(t, c):
        for j in range(TOP_K):
            dst = pos_ref[(base + t) * TOP_K + j]
            pltpu.make_async_copy(x_ref.at[pl.ds(t, 1)], xs_hbm.at[pl.ds(dst, 1)], sem).start()
        return c

    def wait(t, c):
        for j in range(TOP_K):
            pltpu.make_async_copy(x_ref.at[pl.ds(0, 1)], xs_hbm.at[pl.ds(0, 1)], sem).wait()
        return c

    lax.fori_loop(0, tm, start, 0, unroll=DMA_LOOP_UNROLL)
    lax.fori_loop(0, tm, wait, 0, unroll=DMA_LOOP_UNROLL)


def _dispatch(pos_flat, zpos, xp, n_slots, tm):
    rows, width = xp.shape
    grid_spec = pltpu.PrefetchScalarGridSpec(
        num_scalar_prefetch=2,
        grid=(rows // tm,),
        in_specs=[pl.BlockSpec((tm, width), lambda i, p, z: (i, 0))],
        out_specs=pl.BlockSpec(memory_space=pl.ANY),
        scratch_shapes=[pltpu.VMEM((SUB_ROWS, width), U32),
                        pltpu.SemaphoreType.DMA(()), pltpu.SemaphoreType.DMA(())],
    )
    return pl.pallas_call(
        functools.partial(_dispatch_kernel, tm=tm),
        grid_spec=grid_spec,
        out_shape=jax.ShapeDtypeStruct((n_slots, width), U32),
        compiler_params=_cparams(("arbitrary",)),
        name="moe_dispatch",
    )(pos_flat, zpos, xp)


def _moe_kernel(item_e, item_start, item_nsub, n_items,
                xs_hbm, w1g, w1u, w2, b1g, b1u, b2,
                ys_hbm, xraw, xa, xb, yacc, sem_in, sem_out, *, nf):
    it = pl.program_id(0)
    f = pl.program_id(1)
    half = D_MODEL // 2

    @pl.when(it < n_items[0])
    def _():
        start = pl.multiple_of(item_start[it], SUB_ROWS)
        nsub = item_nsub[it]

        def in_copy(s):
            r0 = pl.multiple_of(s * SUB_ROWS, SUB_ROWS)
            return pltpu.make_async_copy(xs_hbm.at[pl.ds(start + r0, SUB_ROWS)],
                                         xraw.at[pl.ds(r0, SUB_ROWS)], sem_in.at[s])

        def out_copy(r0, m):
            rows = pl.ds(pl.multiple_of(r0, SUB_ROWS), m)
            return pltpu.make_async_copy(yacc.at[rows], ys_hbm.at[pl.ds(start + r0, m)], sem_out)

        def mlp_rows(r0, m, first, last):
            r0 = pl.multiple_of(r0, SUB_ROWS)
            rows = pl.ds(r0, m)
            if first:
                for k in range(m // SUB_ROWS):
                    sub = pl.ds(r0 + k * SUB_ROWS, SUB_ROWS)
                    in_copy(r0 // SUB_ROWS + k).wait()
                    w = xraw[sub, :]
                    xa[sub, :] = pltpu.bitcast(w & jnp.uint32(0xFFFF0000), F32).astype(BF16)
                    xb[sub, :] = pltpu.bitcast(w << 16, F32).astype(BF16)
            wg = w1g[0].astype(BF16)
            wu = w1u[0].astype(BF16)
            wd = w2[0].astype(BF16)
            a = xa[rows, :]
            b = xb[rows, :]
            hg = (jnp.dot(a, wg[:half], preferred_element_type=F32)
                  + jnp.dot(b, wg[half:], preferred_element_type=F32) + b1g[0])
            hu = (jnp.dot(a, wu[:half], preferred_element_type=F32)
                  + jnp.dot(b, wu[half:], preferred_element_type=F32) + b1u[0])
            g = jnp.minimum(hg, SWIGLU_LIMIT)
            up = jnp.clip(hu, -SWIGLU_LIMIT, SWIGLU_LIMIT)
            act = g * jax.nn.sigmoid(SWIGLU_ALPHA * g) * (up + 1.0)
            y = jnp.dot(act.astype(BF16), wd, preferred_element_type=F32)
            yacc[rows, :] = y + (b2[0] if first else yacc[rows, :])
            if last:
                out_copy(r0, m).start()

        per_big, per_mid = BIG_ROWS // SUB_ROWS, MID_ROWS // SUB_ROWS
        nbig = nsub // per_big
        mid0 = nbig * BIG_ROWS
        nmid = (nsub - nbig * per_big) // per_mid
        tail0 = nbig * per_big + nmid * per_mid

        def sweep(first, last):
            lax.fori_loop(0, nbig, lambda c, u: (mlp_rows(c * BIG_ROWS, BIG_ROWS, first, last), u)[1], 0)
            lax.fori_loop(0, nmid, lambda c, u: (mlp_rows(mid0 + c * MID_ROWS, MID_ROWS, first, last), u)[1], 0)
            lax.fori_loop(tail0, nsub, lambda s, u: (mlp_rows(s * SUB_ROWS, SUB_ROWS, first, last), u)[1], 0)

        @pl.when(f == 0)
        def _():
            lax.fori_loop(0, nsub, lambda s, c: (in_copy(s).start(), c)[1], 0)
            sweep(True, False)

        @pl.when((f > 0) & (f < nf - 1))
        def _():
            sweep(False, False)

        @pl.when(f == nf - 1)
        def _():
            sweep(False, True)
            lax.fori_loop(0, nbig, lambda c, u: (out_copy(c * BIG_ROWS, BIG_ROWS).wait(), u)[1], 0)
            lax.fori_loop(0, nmid, lambda c, u: (out_copy(mid0 + c * MID_ROWS, MID_ROWS).wait(), u)[1], 0)
            lax.fori_loop(tail0, nsub, lambda s, u: (out_copy(s * SUB_ROWS, SUB_ROWS).wait(), u)[1], 0)


def _moe_experts(items, xs, w1, b1, w2, b2, n_slots, rmax, tf):
    item_e, item_start, item_nsub, n_items = items
    ni = item_e.shape[0]
    nf = D_FF // tf
    half = D_MODEL // 2

    def fidx(it, f, n):
        return jnp.where(it < n[0], f, nf - 1)

    grid_spec = pltpu.PrefetchScalarGridSpec(
        num_scalar_prefetch=4,
        grid=(n_items[0], nf),
        in_specs=[
            pl.BlockSpec(memory_space=pl.ANY),
            pl.BlockSpec((1, D_MODEL, tf), lambda it, f, e, s, ns, n: (e[it], 0, fidx(it, f, n))),
            pl.BlockSpec((1, D_MODEL, tf), lambda it, f, e, s, ns, n: (e[it], 0, nf + fidx(it, f, n))),
            pl.BlockSpec((1, tf, D_MODEL), lambda it, f, e, s, ns, n: (e[it], fidx(it, f, n), 0)),
            pl.BlockSpec((1, 1, tf), lambda it, f, e, s, ns, n: (e[it], 0, fidx(it, f, n))),
            pl.BlockSpec((1, 1, tf), lambda it, f, e, s, ns, n: (e[it], 0, nf + fidx(it, f, n))),
            pl.BlockSpec((1, 1, D_MODEL), lambda it, f, e, s, ns, n: (e[it], 0, 0)),
        ],
        out_specs=pl.BlockSpec(memory_space=pl.ANY),
        scratch_shapes=[
            pltpu.VMEM((rmax, half), U32),
            pltpu.VMEM((rmax, half), BF16),
            pltpu.VMEM((rmax, half), BF16),
            pltpu.VMEM((rmax, D_MODEL), F32),
            pltpu.SemaphoreType.DMA((rmax // SUB_ROWS,)), pltpu.SemaphoreType.DMA(()),
        ],
    )
    return pl.pallas_call(
        functools.partial(_moe_kernel, nf=nf),
        grid_spec=grid_spec,
        out_shape=jax.ShapeDtypeStruct((n_slots, D_MODEL), F32),
        compiler_params=_cparams(("arbitrary", "arbitrary")),
        name="moe_experts",
    )(item_e, item_start, item_nsub, n_items, xs, w1, w1, w2,
      b1.reshape(N_EXPERTS, 1, 2 * D_FF), b1.reshape(N_EXPERTS, 1, 2 * D_FF), b2.reshape(N_EXPERTS, 1, D_MODEL))


def _combine_kernel(pos_ref, ys_hbm, h1_hbm, gate_hbm, y_ref, ysamp_ref, ybuf, hbuf, gbuf, sem, hsem,
                    *, tm, tiles_per_batch, t_pad, n_dec, samp_row0):
    i = pl.program_id(0)
    n_tiles = pl.num_programs(0)

    def gather_and_mix(row0, n):
        hcp = pltpu.make_async_copy(h1_hbm.at[pl.ds(row0, n)], hbuf.at[pl.ds(0, n)], hsem)
        gcp = pltpu.make_async_copy(gate_hbm.at[pl.ds(row0, n)], gbuf.at[pl.ds(0, n)], hsem)
        hcp.start()
        gcp.start()

        def start(t, c):
            for j in range(TOP_K):
                src = pos_ref[(row0 + t) * TOP_K + j]
                pltpu.make_async_copy(ys_hbm.at[pl.ds(src, 1)], ybuf.at[j, pl.ds(t, 1)], sem).start()
            return c

        def wait(t, c):
            for j in range(TOP_K):
                pltpu.make_async_copy(ys_hbm.at[pl.ds(0, 1)], ybuf.at[0, pl.ds(0, 1)], sem).wait()
            return c

        lax.fori_loop(0, n, start, 0, unroll=DMA_LOOP_UNROLL)
        hcp.wait()
        gcp.wait()
        lax.fori_loop(0, n, wait, 0, unroll=DMA_LOOP_UNROLL)
        g = gbuf[0:n, :]
        out = hbuf[0:n, :]
        for j in range(TOP_K):
            out = out + g[:, j:j + 1] * ybuf[j, 0:n, :]
        return out

    row0 = (i // tiles_per_batch) * t_pad + N_META + (i % tiles_per_batch) * tm
    y_ref[0] = gather_and_mix(pl.multiple_of(row0, 8), tm)

    @pl.when(i == n_tiles - 1)
    def _():
        ysamp_ref[...] = gather_and_mix(samp_row0, n_dec)


def _combine(pos_flat, ys, h1, gate, n_batch, seq, t_pad, n_dec, samp_row0, tm):
    tpb = seq // tm
    grid_spec = pltpu.PrefetchScalarGridSpec(
        num_scalar_prefetch=1,
        grid=(n_batch * tpb,),
        in_specs=[pl.BlockSpec(memory_space=pl.ANY)] * 3,
        out_specs=[pl.BlockSpec((1, tm, D_MODEL), lambda i, p: (i // tpb, i % tpb, 0)),
                   pl.BlockSpec((n_dec, D_MODEL), lambda i, p: (0, 0))],
        scratch_shapes=[
            pltpu.VMEM((TOP_K, tm, D_MODEL), F32),
            pltpu.VMEM((tm, D_MODEL), F32),
            pltpu.VMEM((tm, LANES), F32),
            pltpu.SemaphoreType.DMA(()), pltpu.SemaphoreType.DMA(()),
        ],
    )
    return pl.pallas_call(
        functools.partial(_combine_kernel, tm=tm, tiles_per_batch=tpb, t_pad=t_pad, n_dec=n_dec,
                          samp_row0=samp_row0),
        grid_spec=grid_spec,
        out_shape=[jax.ShapeDtypeStruct((n_batch, seq, D_MODEL), F32),
                   jax.ShapeDtypeStruct((n_dec, D_MODEL), F32)],
        compiler_params=_cparams(("arbitrary",)),
        name="moe_combine",
    )(pos_flat, ys, h1, gate)


def _t5_bucket(rel):
    n = jnp.maximum(rel, 0)
    max_exact = N_BUCKETS // 2
    nf = jnp.maximum(n, 1).astype(F32)
    large = max_exact + (jnp.log(nf / max_exact) / math.log(MAX_DISTANCE / max_exact)
                         * (N_BUCKETS - max_exact)).astype(I32)
    large = jnp.minimum(large, N_BUCKETS - 1)
    return jnp.where(n < max_exact, n, large)


def _bias_tables(rel_bias, tq):
    shifted = (rel_bias - rel_bias[N_BUCKETS - 1][None]) * LOG2E

    def bias_of(rel, out):
        onehot = jax.nn.one_hot(_t5_bucket(rel), N_BUCKETS, dtype=F32)
        return jnp.einsum("...b,bh->" + out, onehot, shifted, precision=lax.Precision.HIGHEST)

    rel0 = jnp.arange(tq)[:, None] - jnp.arange(tq)[None, :]
    diag = jnp.where((rel0 >= 0)[None], bias_of(rel0, "h..."), NEG_INF)
    tiles = jnp.stack([diag, bias_of(tq + rel0, "h...")])
    rel_last = PAGE_SIZE - jnp.arange(PAGE_SIZE)
    pbias = jnp.broadcast_to(bias_of(rel_last, "...h")[:, :, None], (PAGE_SIZE, N_HEADS, 2 * LANES))
    sbias = jnp.broadcast_to(bias_of(jnp.zeros((), I32), "...h")[:, None], (N_HEADS, 2 * LANES))
    return tiles, pbias, sbias


def _routing_tables(counts, topi, rank, valid, n_trash_rows, rmax, n_items_max, real_slots):
    padded = (counts + SUB_ROWS - 1) // SUB_ROWS * SUB_ROWS
    ends = jnp.cumsum(padded)
    off = ends - padded
    experts = jnp.arange(N_EXPERTS, dtype=I32)
    pos = rank + jnp.sum(jnp.where(topi[..., None] == experts, off, 0), axis=-1)
    trash_row = jnp.cumsum(jnp.logical_not(valid).astype(I32)) - 1
    trash = real_slots + trash_row[:, None] * TOP_K + jnp.arange(TOP_K, dtype=I32)[None]
    pos_scatter = jnp.where(valid[:, None], pos, trash).reshape(-1)
    pos_gather = jnp.where(valid[:, None], pos, 0).reshape(-1)
    zero_trash = real_slots + -(-(n_trash_rows * TOP_K) // SUB_ROWS) * SUB_ROWS
    zpos = jnp.where(counts > 0, ends - SUB_ROWS, zero_trash).astype(I32)
    per_e = (padded + rmax - 1) // rmax
    item_end = jnp.cumsum(per_e)
    n_items = item_end[-1]
    t = jnp.arange(n_items_max, dtype=I32)
    tt = jnp.minimum(t, n_items - 1)
    e_of = jnp.minimum(jnp.searchsorted(item_end, tt, side="right"), N_EXPERTS - 1).astype(I32)
    k = tt - (item_end - per_e)[e_of]
    start = off[e_of] + k * rmax
    nrows = jnp.minimum(rmax, padded[e_of] - k * rmax)
    nsub = jnp.where(t < n_items, nrows // SUB_ROWS, 0).astype(I32)
    items = (e_of, start.astype(I32), nsub, n_items.reshape(1).astype(I32))
    return pos_scatter.astype(I32), pos_gather.astype(I32), zpos, items, zero_trash + SUB_ROWS


def kernel(x_prompt, x_sample, cache_k, cache_v, state_conv, page_table, meta_tokens, rel_bias, norm1_g, w_in,
           conv_w, q_norm_g, k_norm_g, lambda_q1, lambda_k1, lambda_q2, lambda_k2, subln_g, w_branch_a,
           w_branch_b, w_out, norm2_g, w_router, b_router, w_mlp1, b_mlp1, w_mlp2, b_mlp2):
    n_batch, seq, _ = x_prompt.shape
    n_dec, t_dec, _ = x_sample.shape
    depth = cache_k.shape[0]
    assert depth == 1 and t_dec == 1 and n_dec == 8
    t_real = seq + N_META
    tq = 384
    t_pad = -(-(t_real + n_dec) // tq) * tq
    rows = n_batch * t_pad
    samp_row0 = t_real
    assert samp_row0 % 8 == 0 and t_pad % 24 == 0
    lam_init = 0.8 - 0.6 * math.exp(-0.3 * 0)

    z2d, h2d = _in_proj(x_prompt, meta_tokens, x_sample.reshape(n_dec, D_MODEL), norm1_g[0],
                        w_in[0].astype(BF16), t_pad, tm=t_pad // 3, tn=1024)
    z3 = z2d.reshape(n_batch, t_pad, D_IN)

    qg2 = jnp.tile(q_norm_g[0], 2).reshape(1, LANES)
    kg2 = jnp.tile(k_norm_g[0], 2).reshape(1, LANES)
    sg = subln_g[0].reshape(1, V_DIM)
    lams = [v[0].reshape(1, HEAD_DIM) for v in (lambda_q1, lambda_k1, lambda_q2, lambda_k2)]
    bias_tiles, pbias, sbias = _bias_tables(rel_bias, tq)

    ya, k_p, v_p, qs, kb, vb, ust = _prep(z3, conv_w[0], qg2, kg2, n_batch, t_pad, t_real, tm=tq)
    on = _prompt_attention(qs, kb, vb, bias_tiles, lams, sg, lam_init, t_pad, tq)

    row_block = samp_row0 // n_dec
    ya_s, u_s = _sample_conv(z2d, state_conv[0, :, 0], state_conv[0, :, 1], conv_w[0], row_block, n_dec)
    zs = z2d[samp_row0:samp_row0 + n_dec].reshape(n_dec, D_IN // LANES, LANES)
    o_s, k_s, v_s = _sample_attention(zs, cache_k, cache_v, page_table, pbias, sbias, qg2, kg2, lams, sg,
                                      lam_init, pages_per_step=16)

    ya = ya.at[0, samp_row0:samp_row0 + n_dec].set(ya_s.astype(BF16))
    on = on.at[0, :, samp_row0:samp_row0 + n_dec].set(o_s.transpose(1, 0, 2).astype(BF16))

    wr = jnp.pad(w_router[0], ((0, 0), (0, LANES - N_EXPERTS)))
    br = jnp.concatenate([b_router[0], jnp.full((LANES - N_EXPERTS,), NEG_INF, F32)]).reshape(1, LANES)
    h1, xp, topi, gate, rank, cnt = _merge(
        ya.reshape(rows, D_CONV), on, z2d, h2d,
        w_branch_a[0].astype(BF16), w_branch_b[0].astype(BF16), w_out[0].astype(BF16),
        norm2_g[0].reshape(1, D_MODEL), wr.astype(BF16), br, tm=tq, t_pad=t_pad, t_real=t_real, n_dec=n_dec)

    rmax = 1536
    n_assign = (n_batch * t_real + n_dec) * TOP_K
    real_slots = -(-(n_assign + N_EXPERTS * (SUB_ROWS - 1)) // SUB_ROWS) * SUB_ROWS
    n_items_max = N_EXPERTS + real_slots // rmax
    pos_in_batch = jnp.arange(rows, dtype=I32) % t_pad
    valid = (pos_in_batch < t_real) | ((jnp.arange(rows) < t_pad) & (pos_in_batch < t_real + n_dec))
    pos_scatter, pos_gather, zpos, items, n_slots = _routing_tables(
        cnt[0, :N_EXPERTS].astype(I32), topi[:, :TOP_K], rank[:, :TOP_K], valid,
        rows - n_assign // TOP_K, rmax, n_items_max, real_slots)

    xs = _dispatch(pos_scatter, zpos, xp, n_slots, tm=tq)
    ys = _moe_experts(items, xs, w_mlp1[0], b_mlp1[0], w_mlp2[0], b_mlp2[0], n_slots, rmax, tf=256)
    y_prompt, y_samp = _combine(pos_gather, ys, h1, gate, n_batch, seq, t_pad, n_dec, samp_row0, tm=512)

    return (y_prompt,
            y_samp.reshape(n_dec, 1, D_MODEL),
            k_p.reshape(1, n_batch, t_real, N_HEADS, 2 * HEAD_DIM),
            v_p.reshape(1, n_batch, t_real, N_HEADS, V_DIM),
            ust[:, 6:8].reshape(1, n_batch, CONV_K - 1, D_CONV),
            k_s.reshape(1, n_dec, 1, N_HEADS, 2 * HEAD_DIM),
            v_s.reshape(1, n_dec, 1, N_HEADS, V_DIM),
            jnp.stack([state_conv[0, :, 1], u_s], axis=1).reshape(1, n_dec, CONV_K - 1, D_CONV))
```

```python
import functools
import math

import jax
import jax.numpy as jnp
import numpy as np
from jax import lax
from jax.experimental import pallas as pl
from jax.experimental.pallas import tpu as pltpu

D_MODEL = 2048
N_META = 16
D_CONV = 1024
CONV_K = 3
N_HEADS = 8
HEAD_DIM = 64
V_DIM = 128
D_QK = 1024
D_ATTN = 1024
D_IN = 10240
N_BUCKETS = 32
MAX_DISTANCE = 128
N_EXPERTS = 32
TOP_K = 4
D_FF = 2048
SWIGLU_LIMIT = 7.0
SWIGLU_ALPHA = 1.702
EPS = 1e-6
NEG_INF = -1e30
PAGE_SIZE = 128

F32 = jnp.float32
BF16 = jnp.bfloat16
I32 = jnp.int32
U32 = jnp.uint32

LANES = 128
SUB_ROWS = 128
BIG_ROWS = 1024
W_SPLIT = 4
MID_ROWS = 512
ATTN_ROW_CHUNK = 64
LOG2E = 1.4426950408889634
DMA_LOOP_UNROLL = 4
DECODE_KEY_CHUNK = 16
D_QK_OFF = 3 * D_CONV
VMEM_LIMIT = 56 * 1024 * 1024


def _cparams(sem, vmem=VMEM_LIMIT):
    return pltpu.CompilerParams(dimension_semantics=sem, vmem_limit_bytes=vmem)


def _in_proj_kernel(xp_hbm, meta_ref, xs_ref, g_ref, w_ref, z_ref, h_ref, xn_ref, sems,
                    *, tm, tiles_per_batch, seq, n_dec):
    i = pl.program_id(0)
    b = i // tiles_per_batch
    t_real = seq + N_META

    @pl.when(pl.program_id(1) == 0)
    def _():
        for t in range(tiles_per_batch):
            @pl.when(i % tiles_per_batch == t)
            def _(t=t):
                lo = t * tm
                p0, p1 = max(lo, N_META), min(lo + tm, t_real)
                n_chunks = 4
                step = -(-(p1 - p0) // (8 * n_chunks)) * 8
                copies = []
                for c in range(n_chunks):
                    r0, r1 = p0 + c * step, min(p0 + (c + 1) * step, p1)
                    copies.append(pltpu.make_async_copy(
                        xp_hbm.at[b, pl.ds(r0 - N_META, r1 - r0)], h_ref.at[pl.ds(r0 - lo, r1 - r0)], sems.at[c]))
                for cp in copies:
                    cp.start()
                if lo < N_META:
                    h_ref[0:N_META, :] = meta_ref[...]
                if lo + tm > t_real:
                    h_ref[t_real - lo:tm, :] = jnp.zeros((lo + tm - t_real, D_MODEL), F32)

                    @pl.when(b == 0)
                    def _():
                        h_ref[t_real - lo:t_real - lo + n_dec, :] = xs_ref[...]
                for cp in copies:
                    cp.wait()

        x = h_ref[...]
        ms = jnp.mean(x * x, axis=-1, keepdims=True)
        xn_ref[...] = (x * lax.rsqrt(ms + EPS) * g_ref[...]).astype(BF16)

    z_ref[...] = jnp.dot(xn_ref[...], w_ref[...].astype(BF16), preferred_element_type=F32)


def _in_proj(x_prompt, meta_tokens, x_sample2d, g, w, t_pad, tm, tn):
    n_batch, seq, _ = x_prompt.shape
    n_dec = x_sample2d.shape[0]
    rows = n_batch * t_pad
    const = lambda shape: pl.BlockSpec(shape, lambda i, j: (0,) * len(shape))
    return pl.pallas_call(
        functools.partial(_in_proj_kernel, tm=tm, tiles_per_batch=t_pad // tm, seq=seq, n_dec=n_dec),
        grid=(rows // tm, D_IN // tn),
        in_specs=[
            pl.BlockSpec(memory_space=pl.ANY),
            const((N_META, D_MODEL)), const((n_dec, D_MODEL)), const((1, D_MODEL)),
            pl.BlockSpec((D_MODEL, tn), lambda i, j: (0, j)),
        ],
        out_specs=[pl.BlockSpec((tm, tn), lambda i, j: (i, j)),
                   pl.BlockSpec((tm, D_MODEL), lambda i, j: (i, 0))],
        out_shape=[jax.ShapeDtypeStruct((rows, D_IN), F32),
                   jax.ShapeDtypeStruct((rows, D_MODEL), F32)],
        scratch_shapes=[pltpu.VMEM((tm, D_MODEL), BF16), pltpu.SemaphoreType.DMA((4,))],
        compiler_params=_cparams(("parallel", "arbitrary")),
        name="in_proj",
    )(x_prompt, meta_tokens, x_sample2d, g.reshape(1, D_MODEL), w)


def _half_norm(x, g2, lo):
    t = x * x
    s_lo = jnp.sum(jnp.where(lo, t, 0.0), axis=-1, keepdims=True)
    s_hi = jnp.sum(jnp.where(lo, 0.0, t), axis=-1, keepdims=True)
    inv = jnp.where(lo, lax.rsqrt(s_lo * (1.0 / HEAD_DIM) + EPS), lax.rsqrt(s_hi * (1.0 / HEAD_DIM) + EPS))
    return x * inv * g2


def _prep_kernel(zb, zc, zx, zq, zk, zv, cw, qg, kg,
                 ya_ref, kout, vout, qs, kb, vb, ust, carry, *, tm, state_tile, state_row):
    i = pl.program_id(1)

    @pl.when(i == 0)
    def _():
        carry[...] = jnp.zeros_like(carry)

    u = zc[0] * zx[0]
    prev = carry[...]
    row = lax.broadcasted_iota(I32, (tm, 1), 0)
    u1 = jnp.where(row == 0, prev[7:8], pltpu.roll(u, 1, 0))
    u2 = jnp.where(row == 0, prev[6:7], jnp.where(row == 1, prev[7:8], pltpu.roll(u, 2, 0)))
    y = cw[0:1] * u2 + cw[1:2] * u1 + cw[2:3] * u
    ya_ref[0] = (zb[0] * y).astype(BF16)
    carry[...] = u[tm - 8:tm]

    @pl.when(i == state_tile)
    def _():
        ust[0] = u[state_row:state_row + 8]

    lo = lax.broadcasted_iota(I32, (1, LANES), 1) < HEAD_DIM
    for h in range(N_HEADS):
        sl = slice(h * LANES, (h + 1) * LANES)
        qn = _half_norm(zq[0, :, sl], qg[...], lo) * (LOG2E * HEAD_DIM ** -0.5)
        qs[0, 0, h] = jnp.where(lo, qn, 0.0).astype(BF16)
        qs[0, 1, h] = jnp.where(lo, 0.0, qn).astype(BF16)
        kn = _half_norm(zk[0, :, sl], kg[...], lo)
        kout[0, :, sl] = kn
        kb[0, h] = kn.astype(BF16)
        vb[0, h] = zv[0, :, sl].astype(BF16)
    vout[0] = zv[0]


def _prep(z3, conv_w, qg2, kg2, n_batch, t_pad, t_real, tm):
    nt = t_pad // tm
    state_tile = (t_real - 2) // tm
    state_row = ((t_real - 2) % tm) // 8 * 8
    sec = lambda s: pl.BlockSpec((1, tm, 1024), lambda b, i, s=s: (b, i, s))
    small = lambda shape: pl.BlockSpec(shape, lambda b, i: (0,) * len(shape))
    return pl.pallas_call(
        functools.partial(_prep_kernel, tm=tm, state_tile=state_tile, state_row=state_row),
        grid=(n_batch, nt),
        in_specs=[sec(0), sec(1), sec(2), sec(3), sec(4), sec(5),
                  small((CONV_K, D_CONV)), small((1, LANES)), small((1, LANES))],
        out_specs=[
            pl.BlockSpec((1, tm, D_CONV), lambda b, i: (b, i, 0)),
            pl.BlockSpec((1, tm, D_QK), lambda b, i: (b, i, 0)),
            pl.BlockSpec((1, tm, D_ATTN), lambda b, i: (b, i, 0)),
            pl.BlockSpec((1, 2, N_HEADS, tm, LANES), lambda b, i: (b, 0, 0, i, 0)),
            pl.BlockSpec((1, N_HEADS, tm, LANES), lambda b, i: (b, 0, i, 0)),
            pl.BlockSpec((1, N_HEADS, tm, LANES), lambda b, i: (b, 0, i, 0)),
            pl.BlockSpec((1, 8, D_CONV), lambda b, i: (b, 0, 0)),
        ],
        out_shape=[
            jax.ShapeDtypeStruct((n_batch, t_pad, D_CONV), BF16),
            jax.ShapeDtypeStruct((n_batch, t_real, D_QK), F32),
            jax.ShapeDtypeStruct((n_batch, t_real, D_ATTN), F32),
            jax.ShapeDtypeStruct((n_batch, 2, N_HEADS, t_pad, LANES), BF16),
            jax.ShapeDtypeStruct((n_batch, N_HEADS, t_pad, LANES), BF16),
            jax.ShapeDtypeStruct((n_batch, N_HEADS, t_pad, LANES), BF16),
            jax.ShapeDtypeStruct((n_batch, 8, D_CONV), F32),
        ],
        scratch_shapes=[pltpu.VMEM((8, D_CONV), F32)],
        compiler_params=_cparams(("parallel", "arbitrary")),
        name="prep",
    )(z3, z3, z3, z3, z3, z3, conv_w, qg2, kg2)


def _lambda_value(lq1, lk1, lq2, lk2, lam_init):
    a = jnp.sum(lq1[...] * lk1[...], axis=-1, keepdims=True)
    b = jnp.sum(lq2[...] * lk2[...], axis=-1, keepdims=True)
    return jnp.exp(a) - jnp.exp(b) + lam_init


def _attn_kernel(qi_tab, ki_tab, q_ref, k_ref, v_ref, bias_ref, lq1, lk1, lq2, lk2, sg,
                 o_ref, m_ref, l_ref, acc_ref, s_ref, p_ref, a_ref, *, tq, tk, lam_init):
    step = pl.program_id(1)
    qi = qi_tab[step]
    ki = ki_tab[step]
    rc = ATTN_ROW_CHUNK

    @pl.when(ki == 0)
    def _():
        m_ref[...] = jnp.full_like(m_ref, NEG_INF)
        l_ref[...] = jnp.zeros_like(l_ref)
        acc_ref[...] = jnp.zeros_like(acc_ref)

    def score(h, slot):
        q = q_ref[0, :, h].reshape(2 * tq, LANES)
        s_ref[slot] = lax.dot_general(q, k_ref[0, h], (((1,), (1,)), ((), ())), preferred_element_type=F32)

    def softmax_update(h, slot, near):
        for c in range(2 * tq // rc):
            rows = pl.ds(c * rc, rc)
            s = s_ref[slot, rows, :]
            if near:
                s = s + bias_ref[0, h, pl.ds((c % (tq // rc)) * rc, rc), :]
            m_prev = m_ref[h, rows, :]
            m_new = jnp.maximum(m_prev, jnp.max(s, axis=-1, keepdims=True))
            alpha = jnp.exp2(m_prev - m_new)
            p = jnp.exp2(s - jnp.concatenate([m_new] * (tk // LANES), axis=1))
            l_ref[h, rows, :] = alpha * l_ref[h, rows, :] + jnp.sum(p, axis=-1, keepdims=True)
            m_ref[h, rows, :] = m_new
            a_ref[slot, rows, :] = alpha
            p_ref[slot, rows, :] = p.astype(BF16)

    def weighted_values(h, slot):
        acc_ref[h] = a_ref[slot] * acc_ref[h] + jnp.dot(p_ref[slot], v_ref[0, h], preferred_element_type=F32)

    def head_pair(hp, near):
        for slot in range(2):
            score(2 * hp + slot, slot)
            softmax_update(2 * hp + slot, slot, near)
            weighted_values(2 * hp + slot, slot)

    @pl.when(ki >= qi - 1)
    def _():
        lax.fori_loop(0, N_HEADS // 2, lambda hp, c: (head_pair(hp, True), c)[1], 0)

    @pl.when(ki < qi - 1)
    def _():
        lax.fori_loop(0, N_HEADS // 2, lambda hp, c: (head_pair(hp, False), c)[1], 0)

    @pl.when(ki == qi)
    def _():
        lam = _lambda_value(lq1, lk1, lq2, lk2, lam_init)

        def fin(h, c):
            acc = acc_ref[h]
            l = l_ref[h]
            o = acc[:tq] / l[:tq] - lam * (acc[tq:] / l[tq:])
            on = o * lax.rsqrt(jnp.mean(o * o, axis=-1, keepdims=True) + EPS) * sg[...]
            o_ref[0, h] = (on * (1.0 - lam_init)).astype(BF16)
            return c

        lax.fori_loop(0, N_HEADS, fin, 0)


def _prompt_attention(qs, kb, vb, bias_tiles, lams, sg, lam_init, t_pad, tq):
    n_batch = qs.shape[0]
    nq = t_pad // tq
    pairs = [(qi, ki) for qi in range(nq) for ki in range(qi + 1)]
    qi_tab = jnp.asarray(np.array([p[0] for p in pairs], np.int32))
    ki_tab = jnp.asarray(np.array([p[1] for p in pairs], np.int32))
    vec = lambda n: pl.BlockSpec((1, n), lambda b, s, qt, kt: (0, 0))
    grid_spec = pltpu.PrefetchScalarGridSpec(
        num_scalar_prefetch=2,
        grid=(n_batch, len(pairs)),
        in_specs=[
            pl.BlockSpec((1, 2, N_HEADS, tq, LANES), lambda b, s, qt, kt: (b, 0, 0, qt[s], 0)),
            pl.BlockSpec((1, N_HEADS, tq, LANES), lambda b, s, qt, kt: (b, 0, kt[s], 0)),
            pl.BlockSpec((1, N_HEADS, tq, LANES), lambda b, s, qt, kt: (b, 0, kt[s], 0)),
            pl.BlockSpec((1, N_HEADS, tq, tq),
                         lambda b, s, qt, kt: (jnp.minimum(qt[s] - kt[s], 1), 0, 0, 0)),
            vec(HEAD_DIM), vec(HEAD_DIM), vec(HEAD_DIM), vec(HEAD_DIM), vec(V_DIM),
        ],
        out_specs=pl.BlockSpec((1, N_HEADS, tq, LANES), lambda b, s, qt, kt: (b, 0, qt[s], 0)),
        scratch_shapes=[
            pltpu.VMEM((N_HEADS, 2 * tq, LANES), F32),
            pltpu.VMEM((N_HEADS, 2 * tq, LANES), F32),
            pltpu.VMEM((N_HEADS, 2 * tq, LANES), F32),
            pltpu.VMEM((2, 2 * tq, tq), F32),
            pltpu.VMEM((2, 2 * tq, tq), BF16),
            pltpu.VMEM((2, 2 * tq, LANES), F32),
        ],
    )
    return pl.pallas_call(
        functools.partial(_attn_kernel, tq=tq, tk=tq, lam_init=lam_init),
        grid_spec=grid_spec,
        out_shape=jax.ShapeDtypeStruct((n_batch, N_HEADS, t_pad, LANES), BF16),
        compiler_params=_cparams(("parallel", "arbitrary")),
        name="prompt_attn",
    )(qi_tab, ki_tab, qs, kb, vb, bias_tiles, *lams, sg)


def _sample_conv_kernel(zb, zc, zx, s0, s1, cw, ya_ref, u_ref):
    u = zc[...] * zx[...]
    y = cw[0:1] * s0[...] + cw[1:2] * s1[...] + cw[2:3] * u
    ya_ref[...] = zb[...] * y
    u_ref[...] = u


def _sample_conv(z2d, s0, s1, conv_w, row_block, n):
    sec = lambda s: pl.BlockSpec((n, D_CONV), lambda i, s=s: (row_block, s))
    full = lambda r: pl.BlockSpec((r, D_CONV), lambda i: (0, 0))
    return pl.pallas_call(
        _sample_conv_kernel,
        grid=(1,),
        in_specs=[sec(0), sec(1), sec(2), full(n), full(n), full(CONV_K)],
        out_specs=[full(n), full(n)],
        out_shape=[jax.ShapeDtypeStruct((n, D_CONV), F32)] * 2,
        name="sample_conv",
    )(z2d, z2d, z2d, s0, s1, conv_w)


def _decode_kernel(pt_ref, zs, *rest, pages_per_step, n_steps, lam_init):
    kpages = rest[:pages_per_step]
    vpages = rest[pages_per_step:2 * pages_per_step]
    (pbias, pbias_max, sbias, rsum, qg, kg, lq1, lk1, lq2, lk2, sg,
     o_ref, kout, vout, q_sc, s_ref, m_ref, l_ref, acc_ref) = rest[2 * pages_per_step:]
    p = pl.program_id(1)
    kc = DECODE_KEY_CHUNK
    q_lo, k_lo, v_lo = D_QK_OFF // LANES, (D_QK_OFF + D_QK) // LANES, (D_QK_OFF + 2 * D_QK) // LANES
    lo = lax.broadcasted_iota(I32, (1, LANES), 1) < HEAD_DIM

    @pl.when(p == 0)
    def _():
        q_sc[...] = _half_norm(zs[0, q_lo:q_lo + N_HEADS], qg[...], lo) * (LOG2E * HEAD_DIM ** -0.5)
        kout[0] = _half_norm(zs[0, k_lo:k_lo + N_HEADS], kg[...], lo)
        vout[0] = zs[0, v_lo:v_lo + N_HEADS]
        m_ref[...] = jnp.full_like(m_ref, NEG_INF)
        l_ref[...] = jnp.zeros_like(l_ref)
        acc_ref[...] = jnp.zeros_like(acc_ref)

    q = q_sc[...]

    def scores(k):
        n = k.shape[0]
        t = (k * q[None]).reshape(n * N_HEADS, LANES).astype(BF16)
        return jnp.dot(t, rsum[...], preferred_element_type=F32).reshape(n, N_HEADS, 2 * LANES)

    m_prev = m_ref[...]
    m_new = m_prev
    for i in range(pages_per_step):
        s = scores(kpages[i][...])
        s_ref[pl.ds(i * PAGE_SIZE, PAGE_SIZE)] = s
        m_new = jnp.maximum(m_new, jnp.max(s, axis=0))

    m_new = m_new + jnp.where(p == n_steps - 1, pbias_max[...], 0.0)

    @pl.when(p == n_steps - 1)
    def _():
        last = pl.ds((pages_per_step - 1) * PAGE_SIZE, PAGE_SIZE)
        s_ref[last] = s_ref[last] + pbias[...]

    alpha = jnp.exp2(m_prev - m_new)

    def accumulate(i):
        def body(c, carry):
            l, a0, a1 = carry
            pc = jnp.exp2(s_ref[pl.ds(i * PAGE_SIZE + c * kc, kc)] - m_new[None])
            vc = vpages[i][pl.ds(c * kc, kc)]
            return (l + jnp.sum(pc, axis=0),
                    a0 + jnp.sum(pc[:, :, :LANES] * vc, axis=0),
                    a1 + jnp.sum(pc[:, :, LANES:] * vc, axis=0))
        return body

    carry = (alpha * l_ref[...], alpha[:, :LANES] * acc_ref[0], alpha[:, LANES:] * acc_ref[1])
    for i in range(pages_per_step):
        carry = lax.fori_loop(0, PAGE_SIZE // kc, accumulate(i), carry)
    l, a0, a1 = carry
    m_ref[...] = m_new
    l_ref[...] = l
    acc_ref[0] = a0
    acc_ref[1] = a1

    @pl.when(p == n_steps - 1)
    def _():
        k_own = _half_norm(zs[0, k_lo:k_lo + N_HEADS], kg[...], lo)
        v_own = zs[0, v_lo:v_lo + N_HEADS]
        s_own = scores(k_own[None])[0] + sbias[...]
        m_fin = jnp.maximum(m_new, s_own)
        a_fin = jnp.exp2(m_new - m_fin)
        p_own = jnp.exp2(s_own - m_fin)
        l_fin = a_fin * l + p_own
        o0 = (a_fin[:, :LANES] * a0 + p_own[:, :LANES] * v_own) / l_fin[:, :LANES]
        o1 = (a_fin[:, LANES:] * a1 + p_own[:, LANES:] * v_own) / l_fin[:, LANES:]
        o = o0 - _lambda_value(lq1, lk1, lq2, lk2, lam_init) * o1
        on = o * lax.rsqrt(jnp.mean(o * o, axis=-1, keepdims=True) + EPS) * sg[...]
        o_ref[0] = on * (1.0 - lam_init)


def _sample_attention(zs, ck, cv, page_table, pbias, sbias, qg2, kg2, lams, sg, lam_init, pages_per_step):
    n_dec = zs.shape[0]
    n_pages = page_table.shape[1]
    n_steps = n_pages // pages_per_step
    pt = page_table.reshape(-1)
    page = lambda i: pl.BlockSpec(
        (None, None, PAGE_SIZE, N_HEADS, LANES),
        lambda r, p, pt, i=i: (0, pt[r * n_pages + p * pages_per_step + i], 0, 0, 0))
    vec = lambda shape: pl.BlockSpec(shape, lambda r, p, pt: (0,) * len(shape))
    out_row = pl.BlockSpec((1, N_HEADS, LANES), lambda r, p, pt: (r, 0, 0))
    rsum = (jnp.arange(LANES)[:, None] // HEAD_DIM == jnp.arange(2 * LANES)[None, :] // LANES).astype(BF16)
    grid_spec = pltpu.PrefetchScalarGridSpec(
        num_scalar_prefetch=1,
        grid=(n_dec, n_steps),
        in_specs=[pl.BlockSpec((1,) + zs.shape[1:], lambda r, p, pt: (r, 0, 0))]
        + [page(i) for i in range(pages_per_step)] + [page(i) for i in range(pages_per_step)]
        + [vec((PAGE_SIZE, N_HEADS, 2 * LANES)), vec((N_HEADS, 2 * LANES)), vec((N_HEADS, 2 * LANES)),
           vec((LANES, 2 * LANES)),
           vec((1, LANES)), vec((1, LANES)),
           vec((1, HEAD_DIM)), vec((1, HEAD_DIM)), vec((1, HEAD_DIM)), vec((1, HEAD_DIM)), vec((1, V_DIM))],
        out_specs=[out_row, out_row, out_row],
        scratch_shapes=[
            pltpu.VMEM((N_HEADS, LANES), F32),
            pltpu.VMEM((pages_per_step * PAGE_SIZE, N_HEADS, 2 * LANES), F32),
            pltpu.VMEM((N_HEADS, 2 * LANES), F32),
            pltpu.VMEM((N_HEADS, 2 * LANES), F32),
            pltpu.VMEM((2, N_HEADS, LANES), F32),
        ],
    )
    return pl.pallas_call(
        functools.partial(_decode_kernel, pages_per_step=pages_per_step, n_steps=n_steps, lam_init=lam_init),
        grid_spec=grid_spec,
        out_shape=[jax.ShapeDtypeStruct((n_dec, N_HEADS, LANES), F32)] * 3,
        compiler_params=_cparams(("parallel", "arbitrary")),
        name="sample_attn",
    )(pt, zs, *([ck] * pages_per_step), *([cv] * pages_per_step),
      pbias, jnp.maximum(jnp.max(pbias, axis=0), 0.0), sbias, rsum, qg2, kg2, *lams, sg)


def _merge_kernel(ya, on, ga, gb, h, wa, wb, wo, n2g, wr, br,
                  h1_ref, xp_ref, topi_ref, gate_ref, rank_ref, cnt_ref, carry,
                  *, tm, tiles_per_batch, t_real, n_dec):
    i = pl.program_id(0)

    @pl.when(i == 0)
    def _():
        carry[...] = jnp.zeros_like(carry)

    y_a = jnp.dot(ya[...], wa[...], preferred_element_type=F32)
    o_cat = jnp.concatenate([on[0, hd] for hd in range(N_HEADS)], axis=1)
    y_b = jnp.dot(o_cat, wb[...], preferred_element_type=F32)
    mix = jax.nn.sigmoid(ga[...]) * y_a + jax.nn.sigmoid(gb[...]) * y_b
    h1 = h[...] + jnp.dot(mix.astype(BF16), wo[...], preferred_element_type=F32)
    h1_ref[...] = h1

    xn = h1 * lax.rsqrt(jnp.mean(h1 * h1, axis=-1, keepdims=True) + EPS) * n2g[...]
    xb = xn.astype(BF16)
    xf = xb.astype(F32)
    bits = pltpu.bitcast(xf, U32)
    half = D_MODEL // 2
    xp_ref[...] = bits[:, :half] | (bits[:, half:] >> 16)

    logits = jnp.dot(xb, wr[...], preferred_element_type=F32) + br[...]

    lane = lax.broadcasted_iota(I32, (tm, LANES), 1)
    work = logits
    vals, idxs, sels = [], [], []
    for _ in range(TOP_K):
        mx = jnp.max(work, axis=-1, keepdims=True)
        idx = jnp.min(jnp.where(work == mx, lane, LANES), axis=-1, keepdims=True)
        sel = lane == idx
        vals.append(mx)
        idxs.append(idx)
        sels.append(sel)
        work = jnp.where(sel, -jnp.inf, work)
    exps = [jnp.exp(v - vals[0]) for v in vals]
    denom = exps[0] + exps[1] + exps[2] + exps[3]

    pos_in_batch = (i % tiles_per_batch) * tm + lax.broadcasted_iota(I32, (tm, 1), 0)
    limit = jnp.where(i // tiles_per_batch == 0, t_real + n_dec, t_real)
    valid = pos_in_batch < limit

    onehot = jnp.zeros((tm, LANES), F32)
    for sel in sels:
        onehot = onehot + jnp.where(sel, 1.0, 0.0)
    onehot = jnp.where(valid, onehot, 0.0)
    rr = lax.broadcasted_iota(I32, (tm, tm), 0)
    cc = lax.broadcasted_iota(I32, (tm, tm), 1)
    lower = jnp.where(rr > cc, 1.0, 0.0).astype(BF16)
    before = jnp.dot(lower, onehot.astype(BF16), preferred_element_type=F32) + carry[...]

    topi = jnp.zeros((tm, LANES), I32)
    gate = jnp.zeros((tm, LANES), F32)
    rank = jnp.zeros((tm, LANES), I32)
    for j in range(TOP_K):
        rj = jnp.sum(jnp.where(sels[j], before, 0.0), axis=-1, keepdims=True).astype(I32)
        topi = jnp.where(lane == j, idxs[j], topi)
        gate = jnp.where(lane == j, exps[j] / denom, gate)
        rank = jnp.where(lane == j, rj, rank)
    topi_ref[...] = topi
    gate_ref[...] = gate
    rank_ref[...] = rank
    carry[...] = carry[...] + jnp.sum(onehot, axis=0, keepdims=True)
    cnt_ref[...] = jnp.broadcast_to(carry[...], cnt_ref.shape)


def _merge(ya2d, on, z2d, h2d, wa, wb, wo, n2g, wr, br, tm, t_pad, t_real, n_dec):
    rows = h2d.shape[0]
    tpb = t_pad // tm
    const = lambda shape: pl.BlockSpec(shape, lambda i: (0,) * len(shape), pipeline_mode=pl.Buffered(1))
    row = lambda w, dt=None: pl.BlockSpec((tm, w), lambda i: (i, 0))
    return pl.pallas_call(
        functools.partial(_merge_kernel, tm=tm, tiles_per_batch=tpb, t_real=t_real, n_dec=n_dec),
        grid=(rows // tm,),
        in_specs=[
            row(D_CONV),
            pl.BlockSpec((1, N_HEADS, tm, LANES), lambda i: (i // tpb, 0, i % tpb, 0)),
            pl.BlockSpec((tm, D_MODEL), lambda i: (i, 3)),
            pl.BlockSpec((tm, D_MODEL), lambda i: (i, 4)),
            row(D_MODEL),
            const((D_CONV, D_MODEL)), const((D_ATTN, D_MODEL)), const((D_MODEL, D_MODEL)),
            const((1, D_MODEL)), const((D_MODEL, LANES)), const((1, LANES)),
        ],
        out_specs=[row(D_MODEL), row(D_MODEL // 2), row(LANES), row(LANES), row(LANES),
                   pl.BlockSpec((8, LANES), lambda i: (0, 0))],
        out_shape=[
            jax.ShapeDtypeStruct((rows, D_MODEL), F32),
            jax.ShapeDtypeStruct((rows, D_MODEL // 2), U32),
            jax.ShapeDtypeStruct((rows, LANES), I32),
            jax.ShapeDtypeStruct((rows, LANES), F32),
            jax.ShapeDtypeStruct((rows, LANES), I32),
            jax.ShapeDtypeStruct((8, LANES), F32),
        ],
        scratch_shapes=[pltpu.VMEM((1, LANES), F32)],
        compiler_params=_cparams(("arbitrary",)),
        name="merge_route",
    )(ya2d, on, z2d, z2d, h2d, wa, wb, wo, n2g, wr, br)


def _dispatch_kernel(pos_ref, zpos_ref, x_ref, xs_hbm, zbuf, sem, zsem, *, tm):
    i = pl.program_id(0)

    @pl.when(i == 0)
    def _():
        zbuf[...] = jnp.zeros_like(zbuf)

        def zstart(e, c):
            pltpu.make_async_copy(zbuf, xs_hbm.at[pl.ds(pl.multiple_of(zpos_ref[e], SUB_ROWS), SUB_ROWS)],
                                  zsem).start()
            return c

        def zwait(e, c):
            pltpu.make_async_copy(zbuf, xs_hbm.at[pl.ds(0, SUB_ROWS)], zsem).wait()
            return c

        lax.fori_loop(0, N_EXPERTS, zstart, 0)
        lax.fori_loop(0, N_EXPERTS, zwait, 0)

    base = i * tm

    def start(t, c):
        for j in range(TOP_K):
            dst = pos_ref[(base + t) * TOP_K + j]
            pltpu.make_async_copy(x_ref.at[pl.ds(t, 1)], xs_hbm.at[pl.ds(dst, 1)], sem).start()
        return c

    def wait(t, c):
        for j in range(TOP_K):
            pltpu.make_async_copy(x_ref.at[pl.ds(0, 1)], xs_hbm.at[pl.ds(0, 1)], sem).wait()
        return c

    lax.fori_loop(0, tm, start, 0, unroll=DMA_LOOP_UNROLL)
    lax.fori_loop(0, tm, wait, 0, unroll=DMA_LOOP_UNROLL)


def _dispatch(pos_flat, zpos, xp, n_slots, tm):
    rows, width = xp.shape
    grid_spec = pltpu.PrefetchScalarGridSpec(
        num_scalar_prefetch=2,
        grid=(rows // tm,),
        in_specs=[pl.BlockSpec((tm, width), lambda i, p, z: (i, 0))],
        out_specs=pl.BlockSpec(memory_space=pl.ANY),
        scratch_shapes=[pltpu.VMEM((SUB_ROWS, width), U32),
                        pltpu.SemaphoreType.DMA(()), pltpu.SemaphoreType.DMA(())],
    )
    return pl.pallas_call(
        functools.partial(_dispatch_kernel, tm=tm),
        grid_spec=grid_spec,
        out_shape=jax.ShapeDtypeStruct((n_slots, width), U32),
        compiler_params=_cparams(("arbitrary",)),
        name="moe_dispatch",
    )(pos_flat, zpos, xp)


def _moe_kernel(item_e, item_start, item_nsub, n_items, xs_hbm, *rest, nf):
    w1g = rest[:W_SPLIT]
    w1u = rest[W_SPLIT:2 * W_SPLIT]
    w2 = rest[2 * W_SPLIT:3 * W_SPLIT]
    b1g, b1u, b2, ys_hbm, xraw, xa, xb, yacc, sem_in, sem_out = rest[3 * W_SPLIT:]
    it = pl.program_id(0)
    f = pl.program_id(1)
    kw = D_MODEL // W_SPLIT
    per_x = W_SPLIT // 2

    @pl.when(it < n_items[0])
    def _():
        start = pl.multiple_of(item_start[it], SUB_ROWS)
        nsub = item_nsub[it]

        def in_copy(s):
            r0 = pl.multiple_of(s * SUB_ROWS, SUB_ROWS)
            return pltpu.make_async_copy(xs_hbm.at[pl.ds(start + r0, SUB_ROWS)],
                                         xraw.at[pl.ds(r0, SUB_ROWS)], sem_in.at[s])

        def out_copy(r0, m):
            rows = pl.ds(pl.multiple_of(r0, SUB_ROWS), m)
            return pltpu.make_async_copy(yacc.at[rows], ys_hbm.at[pl.ds(start + r0, m)], sem_out)

        def mlp_rows(r0, m, first, last):
            r0 = pl.multiple_of(r0, SUB_ROWS)
            rows = pl.ds(r0, m)
            if first:
                for k in range(m // SUB_ROWS):
                    sub = pl.ds(r0 + k * SUB_ROWS, SUB_ROWS)
                    in_copy(r0 // SUB_ROWS + k).wait()
                    w = xraw[sub, :]
                    xa[sub, :] = pltpu.bitcast(w & jnp.uint32(0xFFFF0000), F32).astype(BF16)
                    xb[sub, :] = pltpu.bitcast(w << 16, F32).astype(BF16)
            xs_parts = [(xa if c < per_x else xb)[rows, (c % per_x) * kw:(c % per_x + 1) * kw]
                        for c in range(W_SPLIT)]
            hg = b1g[0]
            hu = b1u[0]
            for c in range(W_SPLIT):
                hg = hg + jnp.dot(xs_parts[c], w1g[c][0].astype(BF16), preferred_element_type=F32)
                hu = hu + jnp.dot(xs_parts[c], w1u[c][0].astype(BF16), preferred_element_type=F32)
            g = jnp.minimum(hg, SWIGLU_LIMIT)
            up = jnp.clip(hu, -SWIGLU_LIMIT, SWIGLU_LIMIT)
            act = (g * jax.nn.sigmoid(SWIGLU_ALPHA * g) * (up + 1.0)).astype(BF16)
            for c in range(W_SPLIT):
                cols = slice(c * kw, (c + 1) * kw)
                y = jnp.dot(act, w2[c][0].astype(BF16), preferred_element_type=F32)
                yacc[rows, cols] = y + (b2[0, :, cols] if first else yacc[rows, cols])
            if last:
                out_copy(r0, m).start()

        per_big, per_mid = BIG_ROWS // SUB_ROWS, MID_ROWS // SUB_ROWS
        nbig = nsub // per_big
        mid0 = nbig * BIG_ROWS
        nmid = (nsub - nbig * per_big) // per_mid
        tail0 = nbig * per_big + nmid * per_mid

        def sweep(first, last):
            lax.fori_loop(0, nbig, lambda c, u: (mlp_rows(c * BIG_ROWS, BIG_ROWS, first, last), u)[1], 0)
            lax.fori_loop(0, nmid, lambda c, u: (mlp_rows(mid0 + c * MID_ROWS, MID_ROWS, first, last), u)[1], 0)
            lax.fori_loop(tail0, nsub, lambda s, u: (mlp_rows(s * SUB_ROWS, SUB_ROWS, first, last), u)[1], 0)

        @pl.when(f == 0)
        def _():
            lax.fori_loop(0, nsub, lambda s, c: (in_copy(s).start(), c)[1], 0)
            sweep(True, False)

        @pl.when((f > 0) & (f < nf - 1))
        def _():
            sweep(False, False)

        @pl.when(f == nf - 1)
        def _():
            sweep(False, True)
            lax.fori_loop(0, nbig, lambda c, u: (out_copy(c * BIG_ROWS, BIG_ROWS).wait(), u)[1], 0)
            lax.fori_loop(0, nmid, lambda c, u: (out_copy(mid0 + c * MID_ROWS, MID_ROWS).wait(), u)[1], 0)
            lax.fori_loop(tail0, nsub, lambda s, u: (out_copy(s * SUB_ROWS, SUB_ROWS).wait(), u)[1], 0)


def _moe_experts(items, xs, w1, b1, w2, b2, n_slots, rmax, tf):
    item_e, item_start, item_nsub, n_items = items
    ni = item_e.shape[0]
    nf = D_FF // tf
    half = D_MODEL // 2
    kw = D_MODEL // W_SPLIT

    def fidx(it, f, n):
        return jnp.where(it < n[0], f, nf - 1)

    grid_spec = pltpu.PrefetchScalarGridSpec(
        num_scalar_prefetch=4,
        grid=(n_items[0], nf),
        in_specs=[pl.BlockSpec(memory_space=pl.ANY)]
        + [pl.BlockSpec((1, kw, tf), lambda it, f, e, s, ns, n, c=c: (e[it], c, fidx(it, f, n)))
           for c in range(W_SPLIT)]
        + [pl.BlockSpec((1, kw, tf), lambda it, f, e, s, ns, n, c=c: (e[it], c, nf + fidx(it, f, n)))
           for c in range(W_SPLIT)]
        + [pl.BlockSpec((1, tf, kw), lambda it, f, e, s, ns, n, c=c: (e[it], fidx(it, f, n), c))
           for c in range(W_SPLIT)]
        + [
            pl.BlockSpec((1, 1, tf), lambda it, f, e, s, ns, n: (e[it], 0, fidx(it, f, n))),
            pl.BlockSpec((1, 1, tf), lambda it, f, e, s, ns, n: (e[it], 0, nf + fidx(it, f, n))),
            pl.BlockSpec((1, 1, D_MODEL), lambda it, f, e, s, ns, n: (e[it], 0, 0)),
        ],
        out_specs=pl.BlockSpec(memory_space=pl.ANY),
        scratch_shapes=[
            pltpu.VMEM((rmax, half), U32),
            pltpu.VMEM((rmax, half), BF16),
            pltpu.VMEM((rmax, half), BF16),
            pltpu.VMEM((rmax, D_MODEL), F32),
            pltpu.SemaphoreType.DMA((rmax // SUB_ROWS,)), pltpu.SemaphoreType.DMA(()),
        ],
    )
    return pl.pallas_call(
        functools.partial(_moe_kernel, nf=nf),
        grid_spec=grid_spec,
        out_shape=jax.ShapeDtypeStruct((n_slots, D_MODEL), F32),
        compiler_params=_cparams(("arbitrary", "arbitrary")),
        name="moe_experts",
    )(item_e, item_start, item_nsub, n_items, xs, *([w1] * (2 * W_SPLIT)), *([w2] * W_SPLIT),
      b1.reshape(N_EXPERTS, 1, 2 * D_FF), b1.reshape(N_EXPERTS, 1, 2 * D_FF), b2.reshape(N_EXPERTS, 1, D_MODEL))


def _combine_kernel(pos_ref, ys_hbm, h1_hbm, gate_hbm, y_ref, ysamp_ref, ybuf, hbuf, gbuf, sem, hsem,
                    *, tm, tiles_per_batch, t_pad, n_dec, samp_row0):
    i = pl.program_id(0)
    n_tiles = pl.num_programs(0)

    def gather_and_mix(row0, n):
        hcp = pltpu.make_async_copy(h1_hbm.at[pl.ds(row0, n)], hbuf.at[pl.ds(0, n)], hsem)
        gcp = pltpu.make_async_copy(gate_hbm.at[pl.ds(row0, n)], gbuf.at[pl.ds(0, n)], hsem)
        hcp.start()
        gcp.start()

        def start(t, c):
            for j in range(TOP_K):
                src = pos_ref[(row0 + t) * TOP_K + j]
                pltpu.make_async_copy(ys_hbm.at[pl.ds(src, 1)], ybuf.at[j, pl.ds(t, 1)], sem).start()
            return c

        def wait(t, c):
            for j in range(TOP_K):
                pltpu.make_async_copy(ys_hbm.at[pl.ds(0, 1)], ybuf.at[0, pl.ds(0, 1)], sem).wait()
            return c

        lax.fori_loop(0, n, start, 0, unroll=DMA_LOOP_UNROLL)
        hcp.wait()
        gcp.wait()
        lax.fori_loop(0, n, wait, 0, unroll=DMA_LOOP_UNROLL)
        g = gbuf[0:n, :]
        out = hbuf[0:n, :]
        for j in range(TOP_K):
            out = out + g[:, j:j + 1] * ybuf[j, 0:n, :]
        return out

    row0 = (i // tiles_per_batch) * t_pad + N_META + (i % tiles_per_batch) * tm
    y_ref[0] = gather_and_mix(pl.multiple_of(row0, 8), tm)

    @pl.when(i == n_tiles - 1)
    def _():
        ysamp_ref[...] = gather_and_mix(samp_row0, n_dec)


def _combine(pos_flat, ys, h1, gate, n_batch, seq, t_pad, n_dec, samp_row0, tm):
    tpb = seq // tm
    grid_spec = pltpu.PrefetchScalarGridSpec(
        num_scalar_prefetch=1,
        grid=(n_batch * tpb,),
        in_specs=[pl.BlockSpec(memory_space=pl.ANY)] * 3,
        out_specs=[pl.BlockSpec((1, tm, D_MODEL), lambda i, p: (i // tpb, i % tpb, 0)),
                   pl.BlockSpec((n_dec, D_MODEL), lambda i, p: (0, 0))],
        scratch_shapes=[
            pltpu.VMEM((TOP_K, tm, D_MODEL), F32),
            pltpu.VMEM((tm, D_MODEL), F32),
            pltpu.VMEM((tm, LANES), F32),
            pltpu.SemaphoreType.DMA(()), pltpu.SemaphoreType.DMA(()),
        ],
    )
    return pl.pallas_call(
        functools.partial(_combine_kernel, tm=tm, tiles_per_batch=tpb, t_pad=t_pad, n_dec=n_dec,
                          samp_row0=samp_row0),
        grid_spec=grid_spec,
        out_shape=[jax.ShapeDtypeStruct((n_batch, seq, D_MODEL), F32),
                   jax.ShapeDtypeStruct((n_dec, D_MODEL), F32)],
        compiler_params=_cparams(("arbitrary",)),
        name="moe_combine",
    )(pos_flat, ys, h1, gate)


def _t5_bucket(rel):
    n = jnp.maximum(rel, 0)
    max_exact = N_BUCKETS // 2
    nf = jnp.maximum(n, 1).astype(F32)
    large = max_exact + (jnp.log(nf / max_exact) / math.log(MAX_DISTANCE / max_exact)
                         * (N_BUCKETS - max_exact)).astype(I32)
    large = jnp.minimum(large, N_BUCKETS - 1)
    return jnp.where(n < max_exact, n, large)


def _bias_tables(rel_bias, tq):
    shifted = (rel_bias - rel_bias[N_BUCKETS - 1][None]) * LOG2E

    def bias_of(rel, out):
        onehot = jax.nn.one_hot(_t5_bucket(rel), N_BUCKETS, dtype=F32)
        return jnp.einsum("...b,bh->" + out, onehot, shifted, precision=lax.Precision.HIGHEST)

    rel0 = jnp.arange(tq)[:, None] - jnp.arange(tq)[None, :]
    diag = jnp.where((rel0 >= 0)[None], bias_of(rel0, "h..."), NEG_INF)
    tiles = jnp.stack([diag, bias_of(tq + rel0, "h...")])
    rel_last = PAGE_SIZE - jnp.arange(PAGE_SIZE)
    pbias = jnp.broadcast_to(bias_of(rel_last, "...h")[:, :, None], (PAGE_SIZE, N_HEADS, 2 * LANES))
    sbias = jnp.broadcast_to(bias_of(jnp.zeros((), I32), "...h")[:, None], (N_HEADS, 2 * LANES))
    return tiles, pbias, sbias


def _routing_tables(counts, topi, rank, valid, n_trash_rows, rmax, n_items_max, real_slots):
    padded = (counts + SUB_ROWS - 1) // SUB_ROWS * SUB_ROWS
    ends = jnp.cumsum(padded)
    off = ends - padded
    experts = jnp.arange(N_EXPERTS, dtype=I32)
    pos = rank + jnp.sum(jnp.where(topi[..., None] == experts, off, 0), axis=-1)
    trash_row = jnp.cumsum(jnp.logical_not(valid).astype(I32)) - 1
    trash = real_slots + trash_row[:, None] * TOP_K + jnp.arange(TOP_K, dtype=I32)[None]
    pos_scatter = jnp.where(valid[:, None], pos, trash).reshape(-1)
    pos_gather = jnp.where(valid[:, None], pos, 0).reshape(-1)
    zero_trash = real_slots + -(-(n_trash_rows * TOP_K) // SUB_ROWS) * SUB_ROWS
    zpos = jnp.where(counts > 0, ends - SUB_ROWS, zero_trash).astype(I32)
    per_e = (padded + rmax - 1) // rmax
    item_end = jnp.cumsum(per_e)
    n_items = item_end[-1]
    t = jnp.arange(n_items_max, dtype=I32)
    tt = jnp.minimum(t, n_items - 1)
    e_of = jnp.minimum(jnp.searchsorted(item_end, tt, side="right"), N_EXPERTS - 1).astype(I32)
    k = tt - (item_end - per_e)[e_of]
    start = off[e_of] + k * rmax
    nrows = jnp.minimum(rmax, padded[e_of] - k * rmax)
    nsub = jnp.where(t < n_items, nrows // SUB_ROWS, 0).astype(I32)
    items = (e_of, start.astype(I32), nsub, n_items.reshape(1).astype(I32))
    return pos_scatter.astype(I32), pos_gather.astype(I32), zpos, items, zero_trash + SUB_ROWS


def kernel(x_prompt, x_sample, cache_k, cache_v, state_conv, page_table, meta_tokens, rel_bias, norm1_g, w_in,
           conv_w, q_norm_g, k_norm_g, lambda_q1, lambda_k1, lambda_q2, lambda_k2, subln_g, w_branch_a,
           w_branch_b, w_out, norm2_g, w_router, b_router, w_mlp1, b_mlp1, w_mlp2, b_mlp2):
    n_batch, seq, _ = x_prompt.shape
    n_dec, t_dec, _ = x_sample.shape
    depth = cache_k.shape[0]
    assert depth == 1 and t_dec == 1 and n_dec == 8
    t_real = seq + N_META
    tq = 384
    t_pad = -(-(t_real + n_dec) // tq) * tq
    rows = n_batch * t_pad
    samp_row0 = t_real
    assert samp_row0 % 8 == 0 and t_pad % 24 == 0
    lam_init = 0.8 - 0.6 * math.exp(-0.3 * 0)

    z2d, h2d = _in_proj(x_prompt, meta_tokens, x_sample.reshape(n_dec, D_MODEL), norm1_g[0], w_in[0],
                        t_pad, tm=t_pad // 3, tn=512)
    z3 = z2d.reshape(n_batch, t_pad, D_IN)

    qg2 = jnp.tile(q_norm_g[0], 2).reshape(1, LANES)
    kg2 = jnp.tile(k_norm_g[0], 2).reshape(1, LANES)
    sg = subln_g[0].reshape(1, V_DIM)
    lams = [v[0].reshape(1, HEAD_DIM) for v in (lambda_q1, lambda_k1, lambda_q2, lambda_k2)]
    bias_tiles, pbias, sbias = _bias_tables(rel_bias, tq)

    ya, k_p, v_p, qs, kb, vb, ust = _prep(z3, conv_w[0], qg2, kg2, n_batch, t_pad, t_real, tm=tq)
    on = _prompt_attention(qs, kb, vb, bias_tiles, lams, sg, lam_init, t_pad, tq)

    row_block = samp_row0 // n_dec
    ya_s, u_s = _sample_conv(z2d, state_conv[0, :, 0], state_conv[0, :, 1], conv_w[0], row_block, n_dec)
    zs = z2d[samp_row0:samp_row0 + n_dec].reshape(n_dec, D_IN // LANES, LANES)
    o_s, k_s, v_s = _sample_attention(zs, cache_k, cache_v, page_table, pbias, sbias, qg2, kg2, lams, sg,
                                      lam_init, pages_per_step=16)

    ya = ya.at[0, samp_row0:samp_row0 + n_dec].set(ya_s.astype(BF16))
    on = on.at[0, :, samp_row0:samp_row0 + n_dec].set(o_s.transpose(1, 0, 2).astype(BF16))

    wr = jnp.pad(w_router[0], ((0, 0), (0, LANES - N_EXPERTS)))
    br = jnp.concatenate([b_router[0], jnp.full((LANES - N_EXPERTS,), NEG_INF, F32)]).reshape(1, LANES)
    h1, xp, topi, gate, rank, cnt = _merge(
        ya.reshape(rows, D_CONV), on, z2d, h2d,
        w_branch_a[0].astype(BF16), w_branch_b[0].astype(BF16), w_out[0].astype(BF16),
        norm2_g[0].reshape(1, D_MODEL), wr.astype(BF16), br, tm=tq, t_pad=t_pad, t_real=t_real, n_dec=n_dec)

    rmax = 1536
    n_assign = (n_batch * t_real + n_dec) * TOP_K
    real_slots = -(-(n_assign + N_EXPERTS * (SUB_ROWS - 1)) // SUB_ROWS) * SUB_ROWS
    n_items_max = N_EXPERTS + real_slots // rmax
    pos_in_batch = jnp.arange(rows, dtype=I32) % t_pad
    valid = (pos_in_batch < t_real) | ((jnp.arange(rows) < t_pad) & (pos_in_batch < t_real + n_dec))
    pos_scatter, pos_gather, zpos, items, n_slots = _routing_tables(
        cnt[0, :N_EXPERTS].astype(I32), topi[:, :TOP_K], rank[:, :TOP_K], valid,
        rows - n_assign // TOP_K, rmax, n_items_max, real_slots)

    xs = _dispatch(pos_scatter, zpos, xp, n_slots, tm=tq)
    ys = _moe_experts(items, xs, w_mlp1[0], b_mlp1[0], w_mlp2[0], b_mlp2[0], n_slots, rmax, tf=256)
    y_prompt, y_samp = _combine(pos_gather, ys, h1, gate, n_batch, seq, t_pad, n_dec, samp_row0, tm=512)

    return (y_prompt,
            y_samp.reshape(n_dec, 1, D_MODEL),
            k_p.reshape(1, n_batch, t_real, N_HEADS, 2 * HEAD_DIM),
            v_p.reshape(1, n_batch, t_real, N_HEADS, V_DIM),
            ust[:, 6:8].reshape(1, n_batch, CONV_K - 1, D_CONV),
            k_s.reshape(1, n_dec, 1, N_HEADS, 2 * HEAD_DIM),
            v_s.reshape(1, n_dec, 1, N_HEADS, V_DIM),
            jnp.stack([state_conv[0, :, 1], u_s], axis=1).reshape(1, n_dec, CONV_K - 1, D_CONV))
```

```python
import functools
import math

import jax
import jax.numpy as jnp
import numpy as np
from jax import lax
from jax.experimental import pallas as pl
from jax.experimental.pallas import tpu as pltpu

D_MODEL = 2048
N_META = 16
D_CONV = 1024
CONV_K = 3
N_HEADS = 8
HEAD_DIM = 64
V_DIM = 128
D_QK = 1024
D_ATTN = 1024
D_IN = 10240
N_BUCKETS = 32
MAX_DISTANCE = 128
N_EXPERTS = 32
TOP_K = 4
D_FF = 2048
SWIGLU_LIMIT = 7.0
SWIGLU_ALPHA = 1.702
EPS = 1e-6
NEG_INF = -1e30
PAGE_SIZE = 128

F32 = jnp.float32
BF16 = jnp.bfloat16
I32 = jnp.int32
U32 = jnp.uint32

LANES = 128
SUB_ROWS = 128
BIG_ROWS = 1024
MID_ROWS = 512
ATTN_ROW_CHUNK = 64
LOG2E = 1.4426950408889634
DMA_LOOP_UNROLL = 4
DECODE_KEY_CHUNK = 16
D_QK_OFF = 3 * D_CONV
VMEM_LIMIT = 56 * 1024 * 1024


def _cparams(sem, vmem=VMEM_LIMIT):
    return pltpu.CompilerParams(dimension_semantics=sem, vmem_limit_bytes=vmem)


def _in_proj_kernel(xp_hbm, meta_ref, xs_ref, g_ref, w_ref, z_ref, h_ref, xn_ref, sems,
                    *, tm, tiles_per_batch, seq, n_dec):
    i = pl.program_id(0)
    b = i // tiles_per_batch
    t_real = seq + N_META

    @pl.when(pl.program_id(1) == 0)
    def _():
        for t in range(tiles_per_batch):
            @pl.when(i % tiles_per_batch == t)
            def _(t=t):
                lo = t * tm
                p0, p1 = max(lo, N_META), min(lo + tm, t_real)
                n_chunks = 4
                step = -(-(p1 - p0) // (8 * n_chunks)) * 8
                copies = []
                for c in range(n_chunks):
                    r0, r1 = p0 + c * step, min(p0 + (c + 1) * step, p1)
                    copies.append(pltpu.make_async_copy(
                        xp_hbm.at[b, pl.ds(r0 - N_META, r1 - r0)], h_ref.at[pl.ds(r0 - lo, r1 - r0)], sems.at[c]))
                for cp in copies:
                    cp.start()
                if lo < N_META:
                    h_ref[0:N_META, :] = meta_ref[...]
                if lo + tm > t_real:
                    h_ref[t_real - lo:tm, :] = jnp.zeros((lo + tm - t_real, D_MODEL), F32)

                    @pl.when(b == 0)
                    def _():
                        h_ref[t_real - lo:t_real - lo + n_dec, :] = xs_ref[...]
                for cp in copies:
                    cp.wait()

        x = h_ref[...]
        ms = jnp.mean(x * x, axis=-1, keepdims=True)
        xn_ref[...] = (x * lax.rsqrt(ms + EPS) * g_ref[...]).astype(BF16)

    z_ref[...] = jnp.dot(xn_ref[...], w_ref[...].astype(BF16), preferred_element_type=F32)


def _in_proj(x_prompt, meta_tokens, x_sample2d, g, w, t_pad, tm, tn):
    n_batch, seq, _ = x_prompt.shape
    n_dec = x_sample2d.shape[0]
    rows = n_batch * t_pad
    const = lambda shape: pl.BlockSpec(shape, lambda i, j: (0,) * len(shape))
    return pl.pallas_call(
        functools.partial(_in_proj_kernel, tm=tm, tiles_per_batch=t_pad // tm, seq=seq, n_dec=n_dec),
        grid=(rows // tm, D_IN // tn),
        in_specs=[
            pl.BlockSpec(memory_space=pl.ANY),
            const((N_META, D_MODEL)), const((n_dec, D_MODEL)), const((1, D_MODEL)),
            pl.BlockSpec((D_MODEL, tn), lambda i, j: (0, j)),
        ],
        out_specs=[pl.BlockSpec((tm, tn), lambda i, j: (i, j)),
                   pl.BlockSpec((tm, D_MODEL), lambda i, j: (i, 0))],
        out_shape=[jax.ShapeDtypeStruct((rows, D_IN), F32),
                   jax.ShapeDtypeStruct((rows, D_MODEL), F32)],
        scratch_shapes=[pltpu.VMEM((tm, D_MODEL), BF16), pltpu.SemaphoreType.DMA((4,))],
        compiler_params=_cparams(("parallel", "arbitrary")),
        name="in_proj",
    )(x_prompt, meta_tokens, x_sample2d, g.reshape(1, D_MODEL), w)


def _half_norm(x, g2, lo):
    t = x * x
    s_lo = jnp.sum(jnp.where(lo, t, 0.0), axis=-1, keepdims=True)
    s_hi = jnp.sum(jnp.where(lo, 0.0, t), axis=-1, keepdims=True)
    inv = jnp.where(lo, lax.rsqrt(s_lo * (1.0 / HEAD_DIM) + EPS), lax.rsqrt(s_hi * (1.0 / HEAD_DIM) + EPS))
    return x * inv * g2


def _prep_kernel(zb, zc, zx, zq, zk, zv, cw, qg, kg,
                 ya_ref, kout, vout, qs, kb, vb, ust, carry, *, tm, state_tile, state_row):
    i = pl.program_id(1)

    @pl.when(i == 0)
    def _():
        carry[...] = jnp.zeros_like(carry)

    u = zc[0] * zx[0]
    prev = carry[...]
    row = lax.broadcasted_iota(I32, (tm, 1), 0)
    u1 = jnp.where(row == 0, prev[7:8], pltpu.roll(u, 1, 0))
    u2 = jnp.where(row == 0, prev[6:7], jnp.where(row == 1, prev[7:8], pltpu.roll(u, 2, 0)))
    y = cw[0:1] * u2 + cw[1:2] * u1 + cw[2:3] * u
    ya_ref[0] = (zb[0] * y).astype(BF16)
    carry[...] = u[tm - 8:tm]

    @pl.when(i == state_tile)
    def _():
        ust[0] = u[state_row:state_row + 8]

    lo = lax.broadcasted_iota(I32, (1, LANES), 1) < HEAD_DIM
    for h in range(N_HEADS):
        sl = slice(h * LANES, (h + 1) * LANES)
        qn = _half_norm(zq[0, :, sl], qg[...], lo) * (LOG2E * HEAD_DIM ** -0.5)
        qs[0, 0, h] = jnp.where(lo, qn, 0.0).astype(BF16)
        qs[0, 1, h] = jnp.where(lo, 0.0, qn).astype(BF16)
        kn = _half_norm(zk[0, :, sl], kg[...], lo)
        kout[0, :, sl] = kn
        kb[0, h] = kn.astype(BF16)
        vb[0, h] = zv[0, :, sl].astype(BF16)
    vout[0] = zv[0]


def _prep(z3, conv_w, qg2, kg2, n_batch, t_pad, t_real, tm):
    nt = t_pad // tm
    state_tile = (t_real - 2) // tm
    state_row = ((t_real - 2) % tm) // 8 * 8
    sec = lambda s: pl.BlockSpec((1, tm, 1024), lambda b, i, s=s: (b, i, s))
    small = lambda shape: pl.BlockSpec(shape, lambda b, i: (0,) * len(shape))
    return pl.pallas_call(
        functools.partial(_prep_kernel, tm=tm, state_tile=state_tile, state_row=state_row),
        grid=(n_batch, nt),
        in_specs=[sec(0), sec(1), sec(2), sec(3), sec(4), sec(5),
                  small((CONV_K, D_CONV)), small((1, LANES)), small((1, LANES))],
        out_specs=[
            pl.BlockSpec((1, tm, D_CONV), lambda b, i: (b, i, 0)),
            pl.BlockSpec((1, tm, D_QK), lambda b, i: (b, i, 0)),
            pl.BlockSpec((1, tm, D_ATTN), lambda b, i: (b, i, 0)),
            pl.BlockSpec((1, 2, N_HEADS, tm, LANES), lambda b, i: (b, 0, 0, i, 0)),
            pl.BlockSpec((1, N_HEADS, tm, LANES), lambda b, i: (b, 0, i, 0)),
            pl.BlockSpec((1, N_HEADS, tm, LANES), lambda b, i: (b, 0, i, 0)),
            pl.BlockSpec((1, 8, D_CONV), lambda b, i: (b, 0, 0)),
        ],
        out_shape=[
            jax.ShapeDtypeStruct((n_batch, t_pad, D_CONV), BF16),
            jax.ShapeDtypeStruct((n_batch, t_real, D_QK), F32),
            jax.ShapeDtypeStruct((n_batch, t_real, D_ATTN), F32),
            jax.ShapeDtypeStruct((n_batch, 2, N_HEADS, t_pad, LANES), BF16),
            jax.ShapeDtypeStruct((n_batch, N_HEADS, t_pad, LANES), BF16),
            jax.ShapeDtypeStruct((n_batch, N_HEADS, t_pad, LANES), BF16),
            jax.ShapeDtypeStruct((n_batch, 8, D_CONV), F32),
        ],
        scratch_shapes=[pltpu.VMEM((8, D_CONV), F32)],
        compiler_params=_cparams(("parallel", "arbitrary")),
        name="prep",
    )(z3, z3, z3, z3, z3, z3, conv_w, qg2, kg2)


def _lambda_value(lq1, lk1, lq2, lk2, lam_init):
    a = jnp.sum(lq1[...] * lk1[...], axis=-1, keepdims=True)
    b = jnp.sum(lq2[...] * lk2[...], axis=-1, keepdims=True)
    return jnp.exp(a) - jnp.exp(b) + lam_init


def _attn_kernel(qi_tab, ki_tab, q_ref, k_ref, v_ref, bias_ref, lq1, lk1, lq2, lk2, sg,
                 o_ref, m_ref, l_ref, acc_ref, s_ref, p_ref, a_ref, *, tq, tk, lam_init):
    step = pl.program_id(1)
    qi = qi_tab[step]
    ki = ki_tab[step]
    rc = ATTN_ROW_CHUNK

    @pl.when(ki == 0)
    def _():
        m_ref[...] = jnp.full_like(m_ref, NEG_INF)
        l_ref[...] = jnp.zeros_like(l_ref)
        acc_ref[...] = jnp.zeros_like(acc_ref)

    def score(h, slot):
        q = q_ref[0, :, h].reshape(2 * tq, LANES)
        s_ref[slot] = lax.dot_general(q, k_ref[0, h], (((1,), (1,)), ((), ())), preferred_element_type=F32)

    def softmax_update(h, slot, near):
        for c in range(2 * tq // rc):
            rows = pl.ds(c * rc, rc)
            s = s_ref[slot, rows, :]
            if near:
                s = s + bias_ref[0, h, pl.ds((c % (tq // rc)) * rc, rc), :]
            m_prev = m_ref[h, rows, :]
            m_new = jnp.maximum(m_prev, jnp.max(s, axis=-1, keepdims=True))
            alpha = jnp.exp2(m_prev - m_new)
            p = jnp.exp2(s - jnp.concatenate([m_new] * (tk // LANES), axis=1))
            l_ref[h, rows, :] = alpha * l_ref[h, rows, :] + jnp.sum(p, axis=-1, keepdims=True)
            m_ref[h, rows, :] = m_new
            a_ref[slot, rows, :] = alpha
            p_ref[slot, rows, :] = p.astype(BF16)

    def weighted_values(h, slot):
        acc_ref[h] = a_ref[slot] * acc_ref[h] + jnp.dot(p_ref[slot], v_ref[0, h], preferred_element_type=F32)

    def head_pair(hp, near):
        for slot in range(2):
            score(2 * hp + slot, slot)
            softmax_update(2 * hp + slot, slot, near)
            weighted_values(2 * hp + slot, slot)

    @pl.when(ki >= qi - 1)
    def _():
        lax.fori_loop(0, N_HEADS // 2, lambda hp, c: (head_pair(hp, True), c)[1], 0)

    @pl.when(ki < qi - 1)
    def _():
        lax.fori_loop(0, N_HEADS // 2, lambda hp, c: (head_pair(hp, False), c)[1], 0)

    @pl.when(ki == qi)
    def _():
        lam = _lambda_value(lq1, lk1, lq2, lk2, lam_init)

        def fin(h, c):
            acc = acc_ref[h]
            l = l_ref[h]
            o = acc[:tq] / l[:tq] - lam * (acc[tq:] / l[tq:])
            on = o * lax.rsqrt(jnp.mean(o * o, axis=-1, keepdims=True) + EPS) * sg[...]
            o_ref[0, h] = (on * (1.0 - lam_init)).astype(BF16)
            return c

        lax.fori_loop(0, N_HEADS, fin, 0)


def _prompt_attention(qs, kb, vb, bias_tiles, lams, sg, lam_init, t_pad, tq):
    n_batch = qs.shape[0]
    nq = t_pad // tq
    pairs = [(qi, ki) for qi in range(nq) for ki in range(qi + 1)]
    qi_tab = jnp.asarray(np.array([p[0] for p in pairs], np.int32))
    ki_tab = jnp.asarray(np.array([p[1] for p in pairs], np.int32))
    vec = lambda n: pl.BlockSpec((1, n), lambda b, s, qt, kt: (0, 0))
    grid_spec = pltpu.PrefetchScalarGridSpec(
        num_scalar_prefetch=2,
        grid=(n_batch, len(pairs)),
        in_specs=[
            pl.BlockSpec((1, 2, N_HEADS, tq, LANES), lambda b, s, qt, kt: (b, 0, 0, qt[s], 0)),
            pl.BlockSpec((1, N_HEADS, tq, LANES), lambda b, s, qt, kt: (b, 0, kt[s], 0)),
            pl.BlockSpec((1, N_HEADS, tq, LANES), lambda b, s, qt, kt: (b, 0, kt[s], 0)),
            pl.BlockSpec((1, N_HEADS, tq, tq),
                         lambda b, s, qt, kt: (jnp.minimum(qt[s] - kt[s], 1), 0, 0, 0)),
            vec(HEAD_DIM), vec(HEAD_DIM), vec(HEAD_DIM), vec(HEAD_DIM), vec(V_DIM),
        ],
        out_specs=pl.BlockSpec((1, N_HEADS, tq, LANES), lambda b, s, qt, kt: (b, 0, qt[s], 0)),
        scratch_shapes=[
            pltpu.VMEM((N_HEADS, 2 * tq, LANES), F32),
            pltpu.VMEM((N_HEADS, 2 * tq, LANES), F32),
            pltpu.VMEM((N_HEADS, 2 * tq, LANES), F32),
            pltpu.VMEM((2, 2 * tq, tq), F32),
            pltpu.VMEM((2, 2 * tq, tq), BF16),
            pltpu.VMEM((2, 2 * tq, LANES), F32),
        ],
    )
    return pl.pallas_call(
        functools.partial(_attn_kernel, tq=tq, tk=tq, lam_init=lam_init),
        grid_spec=grid_spec,
        out_shape=jax.ShapeDtypeStruct((n_batch, N_HEADS, t_pad, LANES), BF16),
        compiler_params=_cparams(("parallel", "arbitrary")),
        name="prompt_attn",
    )(qi_tab, ki_tab, qs, kb, vb, bias_tiles, *lams, sg)


def _sample_conv_kernel(zb, zc, zx, s0, s1, cw, ya_ref, u_ref):
    u = zc[...] * zx[...]
    y = cw[0:1] * s0[...] + cw[1:2] * s1[...] + cw[2:3] * u
    ya_ref[...] = zb[...] * y
    u_ref[...] = u


def _sample_conv(z2d, s0, s1, conv_w, row_block, n):
    sec = lambda s: pl.BlockSpec((n, D_CONV), lambda i, s=s: (row_block, s))
    full = lambda r: pl.BlockSpec((r, D_CONV), lambda i: (0, 0))
    return pl.pallas_call(
        _sample_conv_kernel,
        grid=(1,),
        in_specs=[sec(0), sec(1), sec(2), full(n), full(n), full(CONV_K)],
        out_specs=[full(n), full(n)],
        out_shape=[jax.ShapeDtypeStruct((n, D_CONV), F32)] * 2,
        name="sample_conv",
    )(z2d, z2d, z2d, s0, s1, conv_w)


def _decode_kernel(pt_ref, zs, *rest, pages_per_step, n_steps, lam_init):
    kpages = rest[:pages_per_step]
    vpages = rest[pages_per_step:2 * pages_per_step]
    (pbias, pbias_max, sbias, rsum, qg, kg, lq1, lk1, lq2, lk2, sg,
     o_ref, kout, vout, q_sc, s_ref, m_ref, l_ref, acc_ref) = rest[2 * pages_per_step:]
    p = pl.program_id(1)
    kc = DECODE_KEY_CHUNK
    q_lo, k_lo, v_lo = D_QK_OFF // LANES, (D_QK_OFF + D_QK) // LANES, (D_QK_OFF + 2 * D_QK) // LANES
    lo = lax.broadcasted_iota(I32, (1, LANES), 1) < HEAD_DIM

    @pl.when(p == 0)
    def _():
        q_sc[...] = _half_norm(zs[0, q_lo:q_lo + N_HEADS], qg[...], lo) * (LOG2E * HEAD_DIM ** -0.5)
        kout[0] = _half_norm(zs[0, k_lo:k_lo + N_HEADS], kg[...], lo)
        vout[0] = zs[0, v_lo:v_lo + N_HEADS]
        m_ref[...] = jnp.full_like(m_ref, NEG_INF)
        l_ref[...] = jnp.zeros_like(l_ref)
        acc_ref[...] = jnp.zeros_like(acc_ref)

    q = q_sc[...]

    def scores(k):
        n = k.shape[0]
        t = (k * q[None]).reshape(n * N_HEADS, LANES).astype(BF16)
        return jnp.dot(t, rsum[...], preferred_element_type=F32).reshape(n, N_HEADS, 2 * LANES)

    m_prev = m_ref[...]
    m_new = m_prev
    for i in range(pages_per_step):
        s = scores(kpages[i][...])
        s_ref[pl.ds(i * PAGE_SIZE, PAGE_SIZE)] = s
        m_new = jnp.maximum(m_new, jnp.max(s, axis=0))

    m_new = m_new + jnp.where(p == n_steps - 1, pbias_max[...], 0.0)

    @pl.when(p == n_steps - 1)
    def _():
        last = pl.ds((pages_per_step - 1) * PAGE_SIZE, PAGE_SIZE)
        s_ref[last] = s_ref[last] + pbias[...]

    alpha = jnp.exp2(m_prev - m_new)

    def accumulate(i):
        def body(c, carry):
            l, a0, a1 = carry
            pc = jnp.exp2(s_ref[pl.ds(i * PAGE_SIZE + c * kc, kc)] - m_new[None])
            vc = vpages[i][pl.ds(c * kc, kc)]
            return (l + jnp.sum(pc, axis=0),
                    a0 + jnp.sum(pc[:, :, :LANES] * vc, axis=0),
                    a1 + jnp.sum(pc[:, :, LANES:] * vc, axis=0))
        return body

    carry = (alpha * l_ref[...], alpha[:, :LANES] * acc_ref[0], alpha[:, LANES:] * acc_ref[1])
    for i in range(pages_per_step):
        carry = lax.fori_loop(0, PAGE_SIZE // kc, accumulate(i), carry)
    l, a0, a1 = carry
    m_ref[...] = m_new
    l_ref[...] = l
    acc_ref[0] = a0
    acc_ref[1] = a1

    @pl.when(p == n_steps - 1)
    def _():
        k_own = _half_norm(zs[0, k_lo:k_lo + N_HEADS], kg[...], lo)
        v_own = zs[0, v_lo:v_lo + N_HEADS]
        s_own = scores(k_own[None])[0] + sbias[...]
        m_fin = jnp.maximum(m_new, s_own)
        a_fin = jnp.exp2(m_new - m_fin)
        p_own = jnp.exp2(s_own - m_fin)
        l_fin = a_fin * l + p_own
        o0 = (a_fin[:, :LANES] * a0 + p_own[:, :LANES] * v_own) / l_fin[:, :LANES]
        o1 = (a_fin[:, LANES:] * a1 + p_own[:, LANES:] * v_own) / l_fin[:, LANES:]
        o = o0 - _lambda_value(lq1, lk1, lq2, lk2, lam_init) * o1
        on = o * lax.rsqrt(jnp.mean(o * o, axis=-1, keepdims=True) + EPS) * sg[...]
        o_ref[0] = on * (1.0 - lam_init)


def _sample_attention(zs, ck, cv, page_table, pbias, sbias, qg2, kg2, lams, sg, lam_init, pages_per_step):
    n_dec = zs.shape[0]
    n_pages = page_table.shape[1]
    n_steps = n_pages // pages_per_step
    pt = page_table.reshape(-1)
    page = lambda i: pl.BlockSpec(
        (None, None, PAGE_SIZE, N_HEADS, LANES),
        lambda r, p, pt, i=i: (0, pt[r * n_pages + p * pages_per_step + i], 0, 0, 0))
    vec = lambda shape: pl.BlockSpec(shape, lambda r, p, pt: (0,) * len(shape))
    out_row = pl.BlockSpec((1, N_HEADS, LANES), lambda r, p, pt: (r, 0, 0))
    rsum = (jnp.arange(LANES)[:, None] // HEAD_DIM == jnp.arange(2 * LANES)[None, :] // LANES).astype(BF16)
    grid_spec = pltpu.PrefetchScalarGridSpec(
        num_scalar_prefetch=1,
        grid=(n_dec, n_steps),
        in_specs=[pl.BlockSpec((1,) + zs.shape[1:], lambda r, p, pt: (r, 0, 0))]
        + [page(i) for i in range(pages_per_step)] + [page(i) for i in range(pages_per_step)]
        + [vec((PAGE_SIZE, N_HEADS, 2 * LANES)), vec((N_HEADS, 2 * LANES)), vec((N_HEADS, 2 * LANES)),
           vec((LANES, 2 * LANES)),
           vec((1, LANES)), vec((1, LANES)),
           vec((1, HEAD_DIM)), vec((1, HEAD_DIM)), vec((1, HEAD_DIM)), vec((1, HEAD_DIM)), vec((1, V_DIM))],
        out_specs=[out_row, out_row, out_row],
        scratch_shapes=[
            pltpu.VMEM((N_HEADS, LANES), F32),
            pltpu.VMEM((pages_per_step * PAGE_SIZE, N_HEADS, 2 * LANES), F32),
            pltpu.VMEM((N_HEADS, 2 * LANES), F32),
            pltpu.VMEM((N_HEADS, 2 * LANES), F32),
            pltpu.VMEM((2, N_HEADS, LANES), F32),
        ],
    )
    return pl.pallas_call(
        functools.partial(_decode_kernel, pages_per_step=pages_per_step, n_steps=n_steps, lam_init=lam_init),
        grid_spec=grid_spec,
        out_shape=[jax.ShapeDtypeStruct((n_dec, N_HEADS, LANES), F32)] * 3,
        compiler_params=_cparams(("parallel", "arbitrary")),
        name="sample_attn",
    )(pt, zs, *([ck] * pages_per_step), *([cv] * pages_per_step),
      pbias, jnp.maximum(jnp.max(pbias, axis=0), 0.0), sbias, rsum, qg2, kg2, *lams, sg)


def _merge_kernel(ya, on, ga, gb, h, wa, wb, wo, n2g, wr, br,
                  h1_ref, xp_ref, topi_ref, gate_ref, rank_ref, cnt_ref, carry,
                  *, tm, tiles_per_batch, t_real, n_dec):
    i = pl.program_id(0)

    @pl.when(i == 0)
    def _():
        carry[...] = jnp.zeros_like(carry)

    y_a = jnp.dot(ya[...], wa[...], preferred_element_type=F32)
    o_cat = jnp.concatenate([on[0, hd] for hd in range(N_HEADS)], axis=1)
    y_b = jnp.dot(o_cat, wb[...], preferred_element_type=F32)
    mix = jax.nn.sigmoid(ga[...]) * y_a + jax.nn.sigmoid(gb[...]) * y_b
    h1 = h[...] + jnp.dot(mix.astype(BF16), wo[...], preferred_element_type=F32)
    h1_ref[...] = h1

    xn = h1 * lax.rsqrt(jnp.mean(h1 * h1, axis=-1, keepdims=True) + EPS) * n2g[...]
    xb = xn.astype(BF16)
    xf = xb.astype(F32)
    bits = pltpu.bitcast(xf, U32)
    half = D_MODEL // 2
    xp_ref[...] = bits[:, :half] | (bits[:, half:] >> 16)

    logits = jnp.dot(xb, wr[...], preferred_element_type=F32) + br[...]

    lane = lax.broadcasted_iota(I32, (tm, LANES), 1)
    work = logits
    vals, idxs, sels = [], [], []
    for _ in range(TOP_K):
        mx = jnp.max(work, axis=-1, keepdims=True)
        idx = jnp.min(jnp.where(work == mx, lane, LANES), axis=-1, keepdims=True)
        sel = lane == idx
        vals.append(mx)
        idxs.append(idx)
        sels.append(sel)
        work = jnp.where(sel, -jnp.inf, work)
    exps = [jnp.exp(v - vals[0]) for v in vals]
    denom = exps[0] + exps[1] + exps[2] + exps[3]

    pos_in_batch = (i % tiles_per_batch) * tm + lax.broadcasted_iota(I32, (tm, 1), 0)
    limit = jnp.where(i // tiles_per_batch == 0, t_real + n_dec, t_real)
    valid = pos_in_batch < limit

    onehot = jnp.zeros((tm, LANES), F32)
    for sel in sels:
        onehot = onehot + jnp.where(sel, 1.0, 0.0)
    onehot = jnp.where(valid, onehot, 0.0)
    rr = lax.broadcasted_iota(I32, (tm, tm), 0)
    cc = lax.broadcasted_iota(I32, (tm, tm), 1)
    lower = jnp.where(rr > cc, 1.0, 0.0).astype(BF16)
    before = jnp.dot(lower, onehot.astype(BF16), preferred_element_type=F32) + carry[...]

    topi = jnp.zeros((tm, LANES), I32)
    gate = jnp.zeros((tm, LANES), F32)
    rank = jnp.zeros((tm, LANES), I32)
    for j in range(TOP_K):
        rj = jnp.sum(jnp.where(sels[j], before, 0.0), axis=-1, keepdims=True).astype(I32)
        topi = jnp.where(lane == j, idxs[j], topi)
        gate = jnp.where(lane == j, exps[j] / denom, gate)
        rank = jnp.where(lane == j, rj, rank)
    topi_ref[...] = topi
    gate_ref[...] = gate
    rank_ref[...] = rank
    carry[...] = carry[...] + jnp.sum(onehot, axis=0, keepdims=True)
    cnt_ref[...] = jnp.broadcast_to(carry[...], cnt_ref.shape)


def _merge(ya2d, on, z2d, h2d, wa, wb, wo, n2g, wr, br, tm, t_pad, t_real, n_dec):
    rows = h2d.shape[0]
    tpb = t_pad // tm
    const = lambda shape: pl.BlockSpec(shape, lambda i: (0,) * len(shape), pipeline_mode=pl.Buffered(1))
    row = lambda w, dt=None: pl.BlockSpec((tm, w), lambda i: (i, 0))
    return pl.pallas_call(
        functools.partial(_merge_kernel, tm=tm, tiles_per_batch=tpb, t_real=t_real, n_dec=n_dec),
        grid=(rows // tm,),
        in_specs=[
            row(D_CONV),
            pl.BlockSpec((1, N_HEADS, tm, LANES), lambda i: (i // tpb, 0, i % tpb, 0)),
            pl.BlockSpec((tm, D_MODEL), lambda i: (i, 3)),
            pl.BlockSpec((tm, D_MODEL), lambda i: (i, 4)),
            row(D_MODEL),
            const((D_CONV, D_MODEL)), const((D_ATTN, D_MODEL)), const((D_MODEL, D_MODEL)),
            const((1, D_MODEL)), const((D_MODEL, LANES)), const((1, LANES)),
        ],
        out_specs=[row(D_MODEL), row(D_MODEL // 2), row(LANES), row(LANES), row(LANES),
                   pl.BlockSpec((8, LANES), lambda i: (0, 0))],
        out_shape=[
            jax.ShapeDtypeStruct((rows, D_MODEL), F32),
            jax.ShapeDtypeStruct((rows, D_MODEL // 2), U32),
            jax.ShapeDtypeStruct((rows, LANES), I32),
            jax.ShapeDtypeStruct((rows, LANES), F32),
            jax.ShapeDtypeStruct((rows, LANES), I32),
            jax.ShapeDtypeStruct((8, LANES), F32),
        ],
        scratch_shapes=[pltpu.VMEM((1, LANES), F32)],
        compiler_params=_cparams(("arbitrary",)),
        name="merge_route",
    )(ya2d, on, z2d, z2d, h2d, wa, wb, wo, n2g, wr, br)


def _dispatch_kernel(pos_ref, zpos_ref, x_ref, xs_hbm, zbuf, sem, zsem, *, tm):
    i = pl.program_id(0)

    @pl.when(i == 0)
    def _():
        zbuf[...] = jnp.zeros_like(zbuf)

        def zstart(e, c):
            pltpu.make_async_copy(zbuf, xs_hbm.at[pl.ds(pl.multiple_of(zpos_ref[e], SUB_ROWS), SUB_ROWS)],
                                  zsem).start()
            return c

        def zwait(e, c):
            pltpu.make_async_copy(zbuf, xs_hbm.at[pl.ds(0, SUB_ROWS)], zsem).wait()
            return c

        lax.fori_loop(0, N_EXPERTS, zstart, 0)
        lax.fori_loop(0, N_EXPERTS, zwait, 0)

    base = i * tm

    def start(t, c):
        for j in range(TOP_K):
            dst = pos_ref[(base + t) * TOP_K + j]
            pltpu.make_async_copy(x_ref.at[pl.ds(t, 1)], xs_hbm.at[pl.ds(dst, 1)], sem).start()
        return c

    def wait(t, c):
        for j in range(TOP_K):
            pltpu.make_async_copy(x_ref.at[pl.ds(0, 1)], xs_hbm.at[pl.ds(0, 1)], sem).wait()
        return c

    lax.fori_loop(0, tm, start, 0, unroll=DMA_LOOP_UNROLL)
    lax.fori_loop(0, tm, wait, 0, unroll=DMA_LOOP_UNROLL)


def _dispatch(pos_flat, zpos, xp, n_slots, tm):
    rows, width = xp.shape
    grid_spec = pltpu.PrefetchScalarGridSpec(
        num_scalar_prefetch=2,
        grid=(rows // tm,),
        in_specs=[pl.BlockSpec((tm, width), lambda i, p, z: (i, 0))],
        out_specs=pl.BlockSpec(memory_space=pl.ANY),
        scratch_shapes=[pltpu.VMEM((SUB_ROWS, width), U32),
                        pltpu.SemaphoreType.DMA(()), pltpu.SemaphoreType.DMA(())],
    )
    return pl.pallas_call(
        functools.partial(_dispatch_kernel, tm=tm),
        grid_spec=grid_spec,
        out_shape=jax.ShapeDtypeStruct((n_slots, width), U32),
        compiler_params=_cparams(("arbitrary",)),
        name="moe_dispatch",
    )(pos_flat, zpos, xp)


def _moe_kernel(item_e, item_start, item_nsub, n_items,
                xs_hbm, w1g, w1u, w2, b1g, b1u, b2,
                ys_hbm, xraw, xa, xb, yacc, sem_in, sem_out, *, nf):
    it = pl.program_id(0)
    f = pl.program_id(1)
    half = D_MODEL // 2

    @pl.when(it < n_items[0])
    def _():
        start = pl.multiple_of(item_start[it], SUB_ROWS)
        nsub = item_nsub[it]

        def in_copy(s):
            r0 = pl.multiple_of(s * SUB_ROWS, SUB_ROWS)
            return pltpu.make_async_copy(xs_hbm.at[pl.ds(start + r0, SUB_ROWS)],
                                         xraw.at[pl.ds(r0, SUB_ROWS)], sem_in.at[s])

        def out_copy(r0, m):
            rows = pl.ds(pl.multiple_of(r0, SUB_ROWS), m)
            return pltpu.make_async_copy(yacc.at[rows], ys_hbm.at[pl.ds(start + r0, m)], sem_out)

        def mlp_rows(r0, m, first, last):
            r0 = pl.multiple_of(r0, SUB_ROWS)
            rows = pl.ds(r0, m)
            if first:
                for k in range(m // SUB_ROWS):
                    sub = pl.ds(r0 + k * SUB_ROWS, SUB_ROWS)
                    in_copy(r0 // SUB_ROWS + k).wait()
                    w = xraw[sub, :]
                    xa[sub, :] = pltpu.bitcast(w & jnp.uint32(0xFFFF0000), F32).astype(BF16)
                    xb[sub, :] = pltpu.bitcast(w << 16, F32).astype(BF16)
            wg = w1g[0].astype(BF16)
            wu = w1u[0].astype(BF16)
            wd = w2[0].astype(BF16)
            a = xa[rows, :]
            b = xb[rows, :]
            hg = (jnp.dot(a, wg[:half], preferred_element_type=F32)
                  + jnp.dot(b, wg[half:], preferred_element_type=F32) + b1g[0])
            hu = (jnp.dot(a, wu[:half], preferred_element_type=F32)
                  + jnp.dot(b, wu[half:], preferred_element_type=F32) + b1u[0])
            g = jnp.minimum(hg, SWIGLU_LIMIT)
            up = jnp.clip(hu, -SWIGLU_LIMIT, SWIGLU_LIMIT)
            act = g * jax.nn.sigmoid(SWIGLU_ALPHA * g) * (up + 1.0)
            y = jnp.dot(act.astype(BF16), wd, preferred_element_type=F32)
            yacc[rows, :] = y + (b2[0] if first else yacc[rows, :])
            if last:
                out_copy(r0, m).start()

        per_big, per_mid = BIG_ROWS // SUB_ROWS, MID_ROWS // SUB_ROWS
        nbig = nsub // per_big
        mid0 = nbig * BIG_ROWS
        nmid = (nsub - nbig * per_big) // per_mid
        tail0 = nbig * per_big + nmid * per_mid

        def sweep(first, last):
            lax.fori_loop(0, nbig, lambda c, u: (mlp_rows(c * BIG_ROWS, BIG_ROWS, first, last), u)[1], 0)
            lax.fori_loop(0, nmid, lambda c, u: (mlp_rows(mid0 + c * MID_ROWS, MID_ROWS, first, last), u)[1], 0)
            lax.fori_loop(tail0, nsub, lambda s, u: (mlp_rows(s * SUB_ROWS, SUB_ROWS, first, last), u)[1], 0)

        @pl.when(f == 0)
        def _():
            lax.fori_loop(0, nsub, lambda s, c: (in_copy(s).start(), c)[1], 0)
            sweep(True, False)

        @pl.when((f > 0) & (f < nf - 1))
        def _():
            sweep(False, False)

        @pl.when(f == nf - 1)
        def _():
            sweep(False, True)
            lax.fori_loop(0, nbig, lambda c, u: (out_copy(c * BIG_ROWS, BIG_ROWS).wait(), u)[1], 0)
            lax.fori_loop(0, nmid, lambda c, u: (out_copy(mid0 + c * MID_ROWS, MID_ROWS).wait(), u)[1], 0)
            lax.fori_loop(tail0, nsub, lambda s, u: (out_copy(s * SUB_ROWS, SUB_ROWS).wait(), u)[1], 0)


def _moe_experts(items, xs, w1, b1, w2, b2, n_slots, rmax, tf):
    item_e, item_start, item_nsub, n_items = items
    ni = item_e.shape[0]
    nf = D_FF // tf
    half = D_MODEL // 2

    def fidx(it, f, n):
        return jnp.where(it < n[0], f, nf - 1)

    grid_spec = pltpu.PrefetchScalarGridSpec(
        num_scalar_prefetch=4,
        grid=(n_items[0], nf),
        in_specs=[
            pl.BlockSpec(memory_space=pl.ANY),
            pl.BlockSpec((1, D_MODEL, tf), lambda it, f, e, s, ns, n: (e[it], 0, fidx(it, f, n))),
            pl.BlockSpec((1, D_MODEL, tf), lambda it, f, e, s, ns, n: (e[it], 0, nf + fidx(it, f, n))),
            pl.BlockSpec((1, tf, D_MODEL), lambda it, f, e, s, ns, n: (e[it], fidx(it, f, n), 0)),
            pl.BlockSpec((1, 1, tf), lambda it, f, e, s, ns, n: (e[it], 0, fidx(it, f, n))),
            pl.BlockSpec((1, 1, tf), lambda it, f, e, s, ns, n: (e[it], 0, nf + fidx(it, f, n))),
            pl.BlockSpec((1, 1, D_MODEL), lambda it, f, e, s, ns, n: (e[it], 0, 0)),
        ],
        out_specs=pl.BlockSpec(memory_space=pl.ANY),
        scratch_shapes=[
            pltpu.VMEM((rmax, half), U32),
            pltpu.VMEM((rmax, half), BF16),
            pltpu.VMEM((rmax, half), BF16),
            pltpu.VMEM((rmax, D_MODEL), F32),
            pltpu.SemaphoreType.DMA((rmax // SUB_ROWS,)), pltpu.SemaphoreType.DMA(()),
        ],
    )
    return pl.pallas_call(
        functools.partial(_moe_kernel, nf=nf),
        grid_spec=grid_spec,
        out_shape=jax.ShapeDtypeStruct((n_slots, D_MODEL), F32),
        compiler_params=_cparams(("arbitrary", "arbitrary")),
        name="moe_experts",
    )(item_e, item_start, item_nsub, n_items, xs, w1, w1, w2,
      b1.reshape(N_EXPERTS, 1, 2 * D_FF), b1.reshape(N_EXPERTS, 1, 2 * D_FF), b2.reshape(N_EXPERTS, 1, D_MODEL))


def _combine_kernel(pos_ref, ys_hbm, h1_hbm, gate_hbm, y_ref, ysamp_ref, ybuf, hbuf, gbuf, sem, hsem,
                    *, tm, tiles_per_batch, t_pad, n_dec, samp_row0):
    i = pl.program_id(0)
    n_tiles = pl.num_programs(0)

    def gather_and_mix(row0, n):
        hcp = pltpu.make_async_copy(h1_hbm.at[pl.ds(row0, n)], hbuf.at[pl.ds(0, n)], hsem)
        gcp = pltpu.make_async_copy(gate_hbm.at[pl.ds(row0, n)], gbuf.at[pl.ds(0, n)], hsem)
        hcp.start()
        gcp.start()

        def start(t, c):
            for j in range(TOP_K):
                src = pos_ref[(row0 + t) * TOP_K + j]
                pltpu.make_async_copy(ys_hbm.at[pl.ds(src, 1)], ybuf.at[j, pl.ds(t, 1)], sem).start()
            return c

        def wait(t, c):
            for j in range(TOP_K):
                pltpu.make_async_copy(ys_hbm.at[pl.ds(0, 1)], ybuf.at[0, pl.ds(0, 1)], sem).wait()
            return c

        lax.fori_loop(0, n, start, 0, unroll=DMA_LOOP_UNROLL)
        hcp.wait()
        gcp.wait()
        lax.fori_loop(0, n, wait, 0, unroll=DMA_LOOP_UNROLL)
        g = gbuf[0:n, :]
        out = hbuf[0:n, :]
        for j in range(TOP_K):
            out = out + g[:, j:j + 1] * ybuf[j, 0:n, :]
        return out

    row0 = (i // tiles_per_batch) * t_pad + N_META + (i % tiles_per_batch) * tm
    y_ref[0] = gather_and_mix(pl.multiple_of(row0, 8), tm)

    @pl.when(i == n_tiles - 1)
    def _():
        ysamp_ref[...] = gather_and_mix(samp_row0, n_dec)


def _combine(pos_flat, ys, h1, gate, n_batch, seq, t_pad, n_dec, samp_row0, tm):
    tpb = seq // tm
    grid_spec = pltpu.PrefetchScalarGridSpec(
        num_scalar_prefetch=1,
        grid=(n_batch * tpb,),
        in_specs=[pl.BlockSpec(memory_space=pl.ANY)] * 3,
        out_specs=[pl.BlockSpec((1, tm, D_MODEL), lambda i, p: (i // tpb, i % tpb, 0)),
                   pl.BlockSpec((n_dec, D_MODEL), lambda i, p: (0, 0))],
        scratch_shapes=[
            pltpu.VMEM((TOP_K, tm, D_MODEL), F32),
            pltpu.VMEM((tm, D_MODEL), F32),
            pltpu.VMEM((tm, LANES), F32),
            pltpu.SemaphoreType.DMA(()), pltpu.SemaphoreType.DMA(()),
        ],
    )
    return pl.pallas_call(
        functools.partial(_combine_kernel, tm=tm, tiles_per_batch=tpb, t_pad=t_pad, n_dec=n_dec,
                          samp_row0=samp_row0),
        grid_spec=grid_spec,
        out_shape=[jax.ShapeDtypeStruct((n_batch, seq, D_MODEL), F32),
                   jax.ShapeDtypeStruct((n_dec, D_MODEL), F32)],
        compiler_params=_cparams(("arbitrary",)),
        name="moe_combine",
    )(pos_flat, ys, h1, gate)


def _t5_bucket(rel):
    n = jnp.maximum(rel, 0)
    max_exact = N_BUCKETS // 2
    nf = jnp.maximum(n, 1).astype(F32)
    large = max_exact + (jnp.log(nf / max_exact) / math.log(MAX_DISTANCE / max_exact)
                         * (N_BUCKETS - max_exact)).astype(I32)
    large = jnp.minimum(large, N_BUCKETS - 1)
    return jnp.where(n < max_exact, n, large)


def _bias_tables(rel_bias, tq):
    shifted = (rel_bias - rel_bias[N_BUCKETS - 1][None]) * LOG2E

    def bias_of(rel, out):
        onehot = jax.nn.one_hot(_t5_bucket(rel), N_BUCKETS, dtype=F32)
        return jnp.einsum("...b,bh->" + out, onehot, shifted, precision=lax.Precision.HIGHEST)

    rel0 = jnp.arange(tq)[:, None] - jnp.arange(tq)[None, :]
    diag = jnp.where((rel0 >= 0)[None], bias_of(rel0, "h..."), NEG_INF)
    tiles = jnp.stack([diag, bias_of(tq + rel0, "h...")])
    rel_last = PAGE_SIZE - jnp.arange(PAGE_SIZE)
    pbias = jnp.broadcast_to(bias_of(rel_last, "...h")[:, :, None], (PAGE_SIZE, N_HEADS, 2 * LANES))
    sbias = jnp.broadcast_to(bias_of(jnp.zeros((), I32), "...h")[:, None], (N_HEADS, 2 * LANES))
    return tiles, pbias, sbias


def _routing_tables(counts, topi, rank, valid, n_trash_rows, rmax, n_items_max, real_slots):
    padded = (counts + SUB_ROWS - 1) // SUB_ROWS * SUB_ROWS
    ends = jnp.cumsum(padded)
    off = ends - padded
    experts = jnp.arange(N_EXPERTS, dtype=I32)
    pos = rank + jnp.sum(jnp.where(topi[..., None] == experts, off, 0), axis=-1)
    trash_row = jnp.cumsum(jnp.logical_not(valid).astype(I32)) - 1
    trash = real_slots + trash_row[:, None] * TOP_K + jnp.arange(TOP_K, dtype=I32)[None]
    pos_scatter = jnp.where(valid[:, None], pos, trash).reshape(-1)
    pos_gather = jnp.where(valid[:, None], pos, 0).reshape(-1)
    zero_trash = real_slots + -(-(n_trash_rows * TOP_K) // SUB_ROWS) * SUB_ROWS
    zpos = jnp.where(counts > 0, ends - SUB_ROWS, zero_trash).astype(I32)
    per_e = (padded + rmax - 1) // rmax
    item_end = jnp.cumsum(per_e)
    n_items = item_end[-1]
    t = jnp.arange(n_items_max, dtype=I32)
    tt = jnp.minimum(t, n_items - 1)
    e_of = jnp.minimum(jnp.searchsorted(item_end, tt, side="right"), N_EXPERTS - 1).astype(I32)
    k = tt - (item_end - per_e)[e_of]
    start = off[e_of] + k * rmax
    nrows = jnp.minimum(rmax, padded[e_of] - k * rmax)
    nsub = jnp.where(t < n_items, nrows // SUB_ROWS, 0).astype(I32)
    items = (e_of, start.astype(I32), nsub, n_items.reshape(1).astype(I32))
    return pos_scatter.astype(I32), pos_gather.astype(I32), zpos, items, zero_trash + SUB_ROWS


def kernel(x_prompt, x_sample, cache_k, cache_v, state_conv, page_table, meta_tokens, rel_bias, norm1_g, w_in,
           conv_w, q_norm_g, k_norm_g, lambda_q1, lambda_k1, lambda_q2, lambda_k2, subln_g, w_branch_a,
           w_branch_b, w_out, norm2_g, w_router, b_router, w_mlp1, b_mlp1, w_mlp2, b_mlp2):
    n_batch, seq, _ = x_prompt.shape
    n_dec, t_dec, _ = x_sample.shape
    depth = cache_k.shape[0]
    assert depth == 1 and t_dec == 1 and n_dec == 8
    t_real = seq + N_META
    tq = 384
    t_pad = -(-(t_real + n_dec) // tq) * tq
    rows = n_batch * t_pad
    samp_row0 = t_real
    assert samp_row0 % 8 == 0 and t_pad % 24 == 0
    lam_init = 0.8 - 0.6 * math.exp(-0.3 * 0)

    z2d, h2d = _in_proj(x_prompt, meta_tokens, x_sample.reshape(n_dec, D_MODEL), norm1_g[0], w_in[0],
                        t_pad, tm=t_pad // 3, tn=512)
    z3 = z2d.reshape(n_batch, t_pad, D_IN)

    qg2 = jnp.tile(q_norm_g[0], 2).reshape(1, LANES)
    kg2 = jnp.tile(k_norm_g[0], 2).reshape(1, LANES)
    sg = subln_g[0].reshape(1, V_DIM)
    lams = [v[0].reshape(1, HEAD_DIM) for v in (lambda_q1, lambda_k1, lambda_q2, lambda_k2)]
    bias_tiles, pbias, sbias = _bias_tables(rel_bias, tq)

    ya, k_p, v_p, qs, kb, vb, ust = _prep(z3, conv_w[0], qg2, kg2, n_batch, t_pad, t_real, tm=tq)
    on = _prompt_attention(qs, kb, vb, bias_tiles, lams, sg, lam_init, t_pad, tq)

    row_block = samp_row0 // n_dec
    ya_s, u_s = _sample_conv(z2d, state_conv[0, :, 0], state_conv[0, :, 1], conv_w[0], row_block, n_dec)
    zs = z2d[samp_row0:samp_row0 + n_dec].reshape(n_dec, D_IN // LANES, LANES)
    o_s, k_s, v_s = _sample_attention(zs, cache_k, cache_v, page_table, pbias, sbias, qg2, kg2, lams, sg,
                                      lam_init, pages_per_step=16)

    ya = ya.at[0, samp_row0:samp_row0 + n_dec].set(ya_s.astype(BF16))
    on = on.at[0, :, samp_row0:samp_row0 + n_dec].set(o_s.transpose(1, 0, 2).astype(BF16))

    wr = jnp.pad(w_router[0], ((0, 0), (0, LANES - N_EXPERTS)))
    br = jnp.concatenate([b_router[0], jnp.full((LANES - N_EXPERTS,), NEG_INF, F32)]).reshape(1, LANES)
    h1, xp, topi, gate, rank, cnt = _merge(
        ya.reshape(rows, D_CONV), on, z2d, h2d,
        w_branch_a[0].astype(BF16), w_branch_b[0].astype(BF16), w_out[0].astype(BF16),
        norm2_g[0].reshape(1, D_MODEL), wr.astype(BF16), br, tm=tq, t_pad=t_pad, t_real=t_real, n_dec=n_dec)

    rmax = 1536
    n_assign = (n_batch * t_real + n_dec) * TOP_K
    real_slots = -(-(n_assign + N_EXPERTS * (SUB_ROWS - 1)) // SUB_ROWS) * SUB_ROWS
    n_items_max = N_EXPERTS + real_slots // rmax
    pos_in_batch = jnp.arange(rows, dtype=I32) % t_pad
    valid = (pos_in_batch < t_real) | ((jnp.arange(rows) < t_pad) & (pos_in_batch < t_real + n_dec))
    pos_scatter, pos_gather, zpos, items, n_slots = _routing_tables(
        cnt[0, :N_EXPERTS].astype(I32), topi[:, :TOP_K], rank[:, :TOP_K], valid,
        rows - n_assign // TOP_K, rmax, n_items_max, real_slots)

    xs = _dispatch(pos_scatter, zpos, xp, n_slots, tm=tq)
    ys = _moe_experts(items, xs, w_mlp1[0], b_mlp1[0], w_mlp2[0], b_mlp2[0], n_slots, rmax, tf=256)
    y_prompt, y_samp = _combine(pos_gather, ys, h1, gate, n_batch, seq, t_pad, n_dec, samp_row0, tm=512)

    return (y_prompt,
            y_samp.reshape(n_dec, 1, D_MODEL),
            k_p.reshape(1, n_batch, t_real, N_HEADS, 2 * HEAD_DIM),
            v_p.reshape(1, n_batch, t_real, N_HEADS, V_DIM),
            ust[:, 6:8].reshape(1, n_batch, CONV_K - 1, D_CONV),
            k_s.reshape(1, n_dec, 1, N_HEADS, 2 * HEAD_DIM),
            v_s.reshape(1, n_dec, 1, N_HEADS, V_DIM),
            jnp.stack([state_conv[0, :, 1], u_s], axis=1).reshape(1, n_dec, CONV_K - 1, D_CONV))
```

```python
import functools
import math

import jax
import jax.numpy as jnp
import numpy as np
from jax import lax
from jax.experimental import pallas as pl
from jax.experimental.pallas import tpu as pltpu

D_MODEL = 2048
N_META = 16
D_CONV = 1024
CONV_K = 3
N_HEADS = 8
HEAD_DIM = 64
V_DIM = 128
D_QK = 1024
D_ATTN = 1024
D_IN = 10240
N_BUCKETS = 32
MAX_DISTANCE = 128
N_EXPERTS = 32
TOP_K = 4
D_FF = 2048
SWIGLU_LIMIT = 7.0
SWIGLU_ALPHA = 1.702
EPS = 1e-6
NEG_INF = -1e30
PAGE_SIZE = 128

F32 = jnp.float32
BF16 = jnp.bfloat16
I32 = jnp.int32
U32 = jnp.uint32

LANES = 128
SUB_ROWS = 128
BIG_ROWS = 1024
MID_ROWS = 512
ATTN_ROW_CHUNK = 64
LOG2E = 1.4426950408889634
DMA_LOOP_UNROLL = 4
DECODE_KEY_CHUNK = 16
D_QK_OFF = 3 * D_CONV
VMEM_LIMIT = 56 * 1024 * 1024


def _cparams(sem, vmem=VMEM_LIMIT):
    return pltpu.CompilerParams(dimension_semantics=sem, vmem_limit_bytes=vmem)


def _in_proj_kernel(xp_hbm, meta_ref, xs_ref, g_ref, w_ref, z_ref, h_ref, xn_ref, sems,
                    *, tm, tiles_per_batch, seq, n_dec):
    i = pl.program_id(0)
    b = i // tiles_per_batch
    t_real = seq + N_META

    @pl.when(pl.program_id(1) == 0)
    def _():
        for t in range(tiles_per_batch):
            @pl.when(i % tiles_per_batch == t)
            def _(t=t):
                lo = t * tm
                p0, p1 = max(lo, N_META), min(lo + tm, t_real)
                n_chunks = 4
                step = -(-(p1 - p0) // (8 * n_chunks)) * 8
                copies = []
                for c in range(n_chunks):
                    r0, r1 = p0 + c * step, min(p0 + (c + 1) * step, p1)
                    copies.append(pltpu.make_async_copy(
                        xp_hbm.at[b, pl.ds(r0 - N_META, r1 - r0)], h_ref.at[pl.ds(r0 - lo, r1 - r0)], sems.at[c]))
                for cp in copies:
                    cp.start()
                if lo < N_META:
                    h_ref[0:N_META, :] = meta_ref[...]
                if lo + tm > t_real:
                    h_ref[t_real - lo:tm, :] = jnp.zeros((lo + tm - t_real, D_MODEL), F32)

                    @pl.when(b == 0)
                    def _():
                        h_ref[t_real - lo:t_real - lo + n_dec, :] = xs_ref[...]
                for cp in copies:
                    cp.wait()

        x = h_ref[...]
        ms = jnp.mean(x * x, axis=-1, keepdims=True)
        xn_ref[...] = (x * lax.rsqrt(ms + EPS) * g_ref[...]).astype(BF16)

    z_ref[...] = jnp.dot(xn_ref[...], w_ref[...].astype(BF16), preferred_element_type=F32)


def _in_proj(x_prompt, meta_tokens, x_sample2d, g, w, t_pad, tm, tn):
    n_batch, seq, _ = x_prompt.shape
    n_dec = x_sample2d.shape[0]
    rows = n_batch * t_pad
    const = lambda shape: pl.BlockSpec(shape, lambda i, j: (0,) * len(shape))
    return pl.pallas_call(
        functools.partial(_in_proj_kernel, tm=tm, tiles_per_batch=t_pad // tm, seq=seq, n_dec=n_dec),
        grid=(rows // tm, D_IN // tn),
        in_specs=[
            pl.BlockSpec(memory_space=pl.ANY),
            const((N_META, D_MODEL)), const((n_dec, D_MODEL)), const((1, D_MODEL)),
            pl.BlockSpec((D_MODEL, tn), lambda i, j: (0, j)),
        ],
        out_specs=[pl.BlockSpec((tm, tn), lambda i, j: (i, j)),
                   pl.BlockSpec((tm, D_MODEL), lambda i, j: (i, 0))],
        out_shape=[jax.ShapeDtypeStruct((rows, D_IN), F32),
                   jax.ShapeDtypeStruct((rows, D_MODEL), F32)],
        scratch_shapes=[pltpu.VMEM((tm, D_MODEL), BF16), pltpu.SemaphoreType.DMA((4,))],
        compiler_params=_cparams(("parallel", "arbitrary")),
        name="in_proj",
    )(x_prompt, meta_tokens, x_sample2d, g.reshape(1, D_MODEL), w)


def _half_norm(x, g2, lo):
    t = x * x
    s_lo = jnp.sum(jnp.where(lo, t, 0.0), axis=-1, keepdims=True)
    s_hi = jnp.sum(jnp.where(lo, 0.0, t), axis=-1, keepdims=True)
    inv = jnp.where(lo, lax.rsqrt(s_lo * (1.0 / HEAD_DIM) + EPS), lax.rsqrt(s_hi * (1.0 / HEAD_DIM) + EPS))
    return x * inv * g2


def _prep_kernel(zb, zc, zx, zq, zk, zv, cw, qg, kg,
                 ya_ref, kout, vout, qs, kb, vb, ust, carry, *, tm, state_tile, state_row):
    i = pl.program_id(1)

    @pl.when(i == 0)
    def _():
        carry[...] = jnp.zeros_like(carry)

    u = zc[0] * zx[0]
    prev = carry[...]
    row = lax.broadcasted_iota(I32, (tm, 1), 0)
    u1 = jnp.where(row == 0, prev[7:8], pltpu.roll(u, 1, 0))
    u2 = jnp.where(row == 0, prev[6:7], jnp.where(row == 1, prev[7:8], pltpu.roll(u, 2, 0)))
    y = cw[0:1] * u2 + cw[1:2] * u1 + cw[2:3] * u
    ya_ref[0] = (zb[0] * y).astype(BF16)
    carry[...] = u[tm - 8:tm]

    @pl.when(i == state_tile)
    def _():
        ust[0] = u[state_row:state_row + 8]

    lo = lax.broadcasted_iota(I32, (1, LANES), 1) < HEAD_DIM
    for h in range(N_HEADS):
        sl = slice(h * LANES, (h + 1) * LANES)
        qn = _half_norm(zq[0, :, sl], qg[...], lo) * (LOG2E * HEAD_DIM ** -0.5)
        qs[0, 0, h] = jnp.where(lo, qn, 0.0).astype(BF16)
        qs[0, 1, h] = jnp.where(lo, 0.0, qn).astype(BF16)
        kn = _half_norm(zk[0, :, sl], kg[...], lo)
        kout[0, :, sl] = kn
        kb[0, h] = kn.astype(BF16)
        vb[0, h] = zv[0, :, sl].astype(BF16)
    vout[0] = zv[0]


def _prep(z3, conv_w, qg2, kg2, n_batch, t_pad, t_real, tm):
    nt = t_pad // tm
    state_tile = (t_real - 2) // tm
    state_row = ((t_real - 2) % tm) // 8 * 8
    sec = lambda s: pl.BlockSpec((1, tm, 1024), lambda b, i, s=s: (b, i, s))
    small = lambda shape: pl.BlockSpec(shape, lambda b, i: (0,) * len(shape))
    return pl.pallas_call(
        functools.partial(_prep_kernel, tm=tm, state_tile=state_tile, state_row=state_row),
        grid=(n_batch, nt),
        in_specs=[sec(0), sec(1), sec(2), sec(3), sec(4), sec(5),
                  small((CONV_K, D_CONV)), small((1, LANES)), small((1, LANES))],
        out_specs=[
            pl.BlockSpec((1, tm, D_CONV), lambda b, i: (b, i, 0)),
            pl.BlockSpec((1, tm, D_QK), lambda b, i: (b, i, 0)),
            pl.BlockSpec((1, tm, D_ATTN), lambda b, i: (b, i, 0)),
            pl.BlockSpec((1, 2, N_HEADS, tm, LANES), lambda b, i: (b, 0, 0, i, 0)),
            pl.BlockSpec((1, N_HEADS, tm, LANES), lambda b, i: (b, 0, i, 0)),
            pl.BlockSpec((1, N_HEADS, tm, LANES), lambda b, i: (b, 0, i, 0)),
            pl.BlockSpec((1, 8, D_CONV), lambda b, i: (b, 0, 0)),
        ],
        out_shape=[
            jax.ShapeDtypeStruct((n_batch, t_pad, D_CONV), BF16),
            jax.ShapeDtypeStruct((n_batch, t_real, D_QK), F32),
            jax.ShapeDtypeStruct((n_batch, t_real, D_ATTN), F32),
            jax.ShapeDtypeStruct((n_batch, 2, N_HEADS, t_pad, LANES), BF16),
            jax.ShapeDtypeStruct((n_batch, N_HEADS, t_pad, LANES), BF16),
            jax.ShapeDtypeStruct((n_batch, N_HEADS, t_pad, LANES), BF16),
            jax.ShapeDtypeStruct((n_batch, 8, D_CONV), F32),
        ],
        scratch_shapes=[pltpu.VMEM((8, D_CONV), F32)],
        compiler_params=_cparams(("parallel", "arbitrary")),
        name="prep",
    )(z3, z3, z3, z3, z3, z3, conv_w, qg2, kg2)


def _lambda_value(lq1, lk1, lq2, lk2, lam_init):
    a = jnp.sum(lq1[...] * lk1[...], axis=-1, keepdims=True)
    b = jnp.sum(lq2[...] * lk2[...], axis=-1, keepdims=True)
    return jnp.exp(a) - jnp.exp(b) + lam_init


def _attn_kernel(qi_tab, ki_tab, q_ref, k_ref, v_ref, bias_ref, lq1, lk1, lq2, lk2, sg,
                 o_ref, m_ref, l_ref, acc_ref, s_ref, p_ref, a_ref, *, tq, tk, lam_init):
    step = pl.program_id(1)
    qi = qi_tab[step]
    ki = ki_tab[step]
    rc = ATTN_ROW_CHUNK

    @pl.when(ki == 0)
    def _():
        m_ref[...] = jnp.full_like(m_ref, NEG_INF)
        l_ref[...] = jnp.zeros_like(l_ref)
        acc_ref[...] = jnp.zeros_like(acc_ref)

    def score(h, slot):
        q = q_ref[0, :, h].reshape(2 * tq, LANES)
        s_ref[slot] = lax.dot_general(q, k_ref[0, h], (((1,), (1,)), ((), ())), preferred_element_type=F32)

    def softmax_update(h, slot, near):
        for c in range(2 * tq // rc):
            rows = pl.ds(c * rc, rc)
            s = s_ref[slot, rows, :]
            if near:
                s = s + bias_ref[0, h, pl.ds((c % (tq // rc)) * rc, rc), :]
            m_prev = m_ref[h, rows, :]
            m_new = jnp.maximum(m_prev, jnp.max(s, axis=-1, keepdims=True))
            alpha = jnp.exp2(m_prev - m_new)
            p = jnp.exp2(s - jnp.concatenate([m_new] * (tk // LANES), axis=1))
            l_ref[h, rows, :] = alpha * l_ref[h, rows, :] + jnp.sum(p, axis=-1, keepdims=True)
            m_ref[h, rows, :] = m_new
            a_ref[slot, rows, :] = alpha
            p_ref[slot, rows, :] = p.astype(BF16)

    def weighted_values(h, slot):
        acc_ref[h] = a_ref[slot] * acc_ref[h] + jnp.dot(p_ref[slot], v_ref[0, h], preferred_element_type=F32)

    def head_pair(hp, near):
        for slot in range(2):
            score(2 * hp + slot, slot)
            softmax_update(2 * hp + slot, slot, near)
            weighted_values(2 * hp + slot, slot)

    @pl.when(ki >= qi - 1)
    def _():
        lax.fori_loop(0, N_HEADS // 2, lambda hp, c: (head_pair(hp, True), c)[1], 0)

    @pl.when(ki < qi - 1)
    def _():
        lax.fori_loop(0, N_HEADS // 2, lambda hp, c: (head_pair(hp, False), c)[1], 0)

    @pl.when(ki == qi)
    def _():
        lam = _lambda_value(lq1, lk1, lq2, lk2, lam_init)

        def fin(h, c):
            acc = acc_ref[h]
            l = l_ref[h]
            o = acc[:tq] / l[:tq] - lam * (acc[tq:] / l[tq:])
            on = o * lax.rsqrt(jnp.mean(o * o, axis=-1, keepdims=True) + EPS) * sg[...]
            o_ref[0, h] = (on * (1.0 - lam_init)).astype(BF16)
            return c

        lax.fori_loop(0, N_HEADS, fin, 0)


def _prompt_attention(qs, kb, vb, bias_tiles, lams, sg, lam_init, t_pad, tq):
    n_batch = qs.shape[0]
    nq = t_pad // tq
    pairs = [(qi, ki) for qi in range(nq) for ki in range(qi + 1)]
    qi_tab = jnp.asarray(np.array([p[0] for p in pairs], np.int32))
    ki_tab = jnp.asarray(np.array([p[1] for p in pairs], np.int32))
    vec = lambda n: pl.BlockSpec((1, n), lambda b, s, qt, kt: (0, 0))
    grid_spec = pltpu.PrefetchScalarGridSpec(
        num_scalar_prefetch=2,
        grid=(n_batch, len(pairs)),
        in_specs=[
            pl.BlockSpec((1, 2, N_HEADS, tq, LANES), lambda b, s, qt, kt: (b, 0, 0, qt[s], 0)),
            pl.BlockSpec((1, N_HEADS, tq, LANES), lambda b, s, qt, kt: (b, 0, kt[s], 0)),
            pl.BlockSpec((1, N_HEADS, tq, LANES), lambda b, s, qt, kt: (b, 0, kt[s], 0)),
            pl.BlockSpec((1, N_HEADS, tq, tq),
                         lambda b, s, qt, kt: (jnp.minimum(qt[s] - kt[s], 1), 0, 0, 0)),
            vec(HEAD_DIM), vec(HEAD_DIM), vec(HEAD_DIM), vec(HEAD_DIM), vec(V_DIM),
        ],
        out_specs=pl.BlockSpec((1, N_HEADS, tq, LANES), lambda b, s, qt, kt: (b, 0, qt[s], 0)),
        scratch_shapes=[
            pltpu.VMEM((N_HEADS, 2 * tq, LANES), F32),
            pltpu.VMEM((N_HEADS, 2 * tq, LANES), F32),
            pltpu.VMEM((N_HEADS, 2 * tq, LANES), F32),
            pltpu.VMEM((2, 2 * tq, tq), F32),
            pltpu.VMEM((2, 2 * tq, tq), BF16),
            pltpu.VMEM((2, 2 * tq, LANES), F32),
        ],
    )
    return pl.pallas_call(
        functools.partial(_attn_kernel, tq=tq, tk=tq, lam_init=lam_init),
        grid_spec=grid_spec,
        out_shape=jax.ShapeDtypeStruct((n_batch, N_HEADS, t_pad, LANES), BF16),
        compiler_params=_cparams(("parallel", "arbitrary")),
        name="prompt_attn",
    )(qi_tab, ki_tab, qs, kb, vb, bias_tiles, *lams, sg)


def _sample_conv_kernel(zb, zc, zx, s0, s1, cw, ya_ref, u_ref):
    u = zc[...] * zx[...]
    y = cw[0:1] * s0[...] + cw[1:2] * s1[...] + cw[2:3] * u
    ya_ref[...] = zb[...] * y
    u_ref[...] = u


def _sample_conv(z2d, s0, s1, conv_w, row_block, n):
    sec = lambda s: pl.BlockSpec((n, D_CONV), lambda i, s=s: (row_block, s))
    full = lambda r: pl.BlockSpec((r, D_CONV), lambda i: (0, 0))
    return pl.pallas_call(
        _sample_conv_kernel,
        grid=(1,),
        in_specs=[sec(0), sec(1), sec(2), full(n), full(n), full(CONV_K)],
        out_specs=[full(n), full(n)],
        out_shape=[jax.ShapeDtypeStruct((n, D_CONV), F32)] * 2,
        name="sample_conv",
    )(z2d, z2d, z2d, s0, s1, conv_w)


def _decode_kernel(pt_ref, zs, *rest, pages_per_step, n_steps, lam_init):
    kpages = rest[:pages_per_step]
    vpages = rest[pages_per_step:2 * pages_per_step]
    (pbias, pbias_max, sbias, rsum, qg, kg, lq1, lk1, lq2, lk2, sg,
     o_ref, kout, vout, q_sc, s_ref, m_ref, l_ref, acc_ref) = rest[2 * pages_per_step:]
    p = pl.program_id(1)
    kc = DECODE_KEY_CHUNK
    q_lo, k_lo, v_lo = D_QK_OFF // LANES, (D_QK_OFF + D_QK) // LANES, (D_QK_OFF + 2 * D_QK) // LANES
    lo = lax.broadcasted_iota(I32, (1, LANES), 1) < HEAD_DIM

    @pl.when(p == 0)
    def _():
        q_sc[...] = _half_norm(zs[0, q_lo:q_lo + N_HEADS], qg[...], lo) * (LOG2E * HEAD_DIM ** -0.5)
        kout[0] = _half_norm(zs[0, k_lo:k_lo + N_HEADS], kg[...], lo)
        vout[0] = zs[0, v_lo:v_lo + N_HEADS]
        m_ref[...] = jnp.full_like(m_ref, NEG_INF)
        l_ref[...] = jnp.zeros_like(l_ref)
        acc_ref[...] = jnp.zeros_like(acc_ref)

    q = q_sc[...]

    def scores(k):
        n = k.shape[0]
        t = (k * q[None]).reshape(n * N_HEADS, LANES).astype(BF16)
        return jnp.dot(t, rsum[...], preferred_element_type=F32).reshape(n, N_HEADS, 2 * LANES)

    m_prev = m_ref[...]
    m_new = m_prev
    for i in range(pages_per_step):
        s = scores(kpages[i][...])
        s_ref[pl.ds(i * PAGE_SIZE, PAGE_SIZE)] = s
        m_new = jnp.maximum(m_new, jnp.max(s, axis=0))

    m_new = m_new + jnp.where(p == n_steps - 1, pbias_max[...], 0.0)

    @pl.when(p == n_steps - 1)
    def _():
        last = pl.ds((pages_per_step - 1) * PAGE_SIZE, PAGE_SIZE)
        s_ref[last] = s_ref[last] + pbias[...]

    alpha = jnp.exp2(m_prev - m_new)

    def accumulate(i):
        def body(c, carry):
            l, a0, a1 = carry
            pc = jnp.exp2(s_ref[pl.ds(i * PAGE_SIZE + c * kc, kc)] - m_new[None])
            vc = vpages[i][pl.ds(c * kc, kc)]
            return (l + jnp.sum(pc, axis=0),
                    a0 + jnp.sum(pc[:, :, :LANES] * vc, axis=0),
                    a1 + jnp.sum(pc[:, :, LANES:] * vc, axis=0))
        return body

    carry = (alpha * l_ref[...], alpha[:, :LANES] * acc_ref[0], alpha[:, LANES:] * acc_ref[1])
    for i in range(pages_per_step):
        carry = lax.fori_loop(0, PAGE_SIZE // kc, accumulate(i), carry)
    l, a0, a1 = carry
    m_ref[...] = m_new
    l_ref[...] = l
    acc_ref[0] = a0
    acc_ref[1] = a1

    @pl.when(p == n_steps - 1)
    def _():
        k_own = _half_norm(zs[0, k_lo:k_lo + N_HEADS], kg[...], lo)
        v_own = zs[0, v_lo:v_lo + N_HEADS]
        s_own = scores(k_own[None])[0] + sbias[...]
        m_fin = jnp.maximum(m_new, s_own)
        a_fin = jnp.exp2(m_new - m_fin)
        p_own = jnp.exp2(s_own - m_fin)
        l_fin = a_fin * l + p_own
        o0 = (a_fin[:, :LANES] * a0 + p_own[:, :LANES] * v_own) / l_fin[:, :LANES]
        o1 = (a_fin[:, LANES:] * a1 + p_own[:, LANES:] * v_own) / l_fin[:, LANES:]
        o = o0 - _lambda_value(lq1, lk1, lq2, lk2, lam_init) * o1
        on = o * lax.rsqrt(jnp.mean(o * o, axis=-1, keepdims=True) + EPS) * sg[...]
        o_ref[0] = on * (1.0 - lam_init)


def _sample_attention(zs, ck, cv, page_table, pbias, sbias, qg2, kg2, lams, sg, lam_init, pages_per_step):
    n_dec = zs.shape[0]
    n_pages = page_table.shape[1]
    n_steps = n_pages // pages_per_step
    pt = page_table.reshape(-1)
    page = lambda i: pl.BlockSpec(
        (None, None, PAGE_SIZE, N_HEADS, LANES),
        lambda r, p, pt, i=i: (0, pt[r * n_pages + p * pages_per_step + i], 0, 0, 0))
    vec = lambda shape: pl.BlockSpec(shape, lambda r, p, pt: (0,) * len(shape))
    out_row = pl.BlockSpec((1, N_HEADS, LANES), lambda r, p, pt: (r, 0, 0))
    rsum = (jnp.arange(LANES)[:, None] // HEAD_DIM == jnp.arange(2 * LANES)[None, :] // LANES).astype(BF16)
    grid_spec = pltpu.PrefetchScalarGridSpec(
        num_scalar_prefetch=1,
        grid=(n_dec, n_steps),
        in_specs=[pl.BlockSpec((1,) + zs.shape[1:], lambda r, p, pt: (r, 0, 0))]
        + [page(i) for i in range(pages_per_step)] + [page(i) for i in range(pages_per_step)]
        + [vec((PAGE_SIZE, N_HEADS, 2 * LANES)), vec((N_HEADS, 2 * LANES)), vec((N_HEADS, 2 * LANES)),
           vec((LANES, 2 * LANES)),
           vec((1, LANES)), vec((1, LANES)),
           vec((1, HEAD_DIM)), vec((1, HEAD_DIM)), vec((1, HEAD_DIM)), vec((1, HEAD_DIM)), vec((1, V_DIM))],
        out_specs=[out_row, out_row, out_row],
        scratch_shapes=[
            pltpu.VMEM((N_HEADS, LANES), F32),
            pltpu.VMEM((pages_per_step * PAGE_SIZE, N_HEADS, 2 * LANES), F32),
            pltpu.VMEM((N_HEADS, 2 * LANES), F32),
            pltpu.VMEM((N_HEADS, 2 * LANES), F32),
            pltpu.VMEM((2, N_HEADS, LANES), F32),
        ],
    )
    return pl.pallas_call(
        functools.partial(_decode_kernel, pages_per_step=pages_per_step, n_steps=n_steps, lam_init=lam_init),
        grid_spec=grid_spec,
        out_shape=[jax.ShapeDtypeStruct((n_dec, N_HEADS, LANES), F32)] * 3,
        compiler_params=_cparams(("parallel", "arbitrary")),
        name="sample_attn",
    )(pt, zs, *([ck] * pages_per_step), *([cv] * pages_per_step),
      pbias, jnp.maximum(jnp.max(pbias, axis=0), 0.0), sbias, rsum, qg2, kg2, *lams, sg)


def _merge_kernel(ya, on, ga, gb, h, wa, wb, wo, n2g, wr, br,
                  h1_ref, xp_ref, topi_ref, gate_ref, rank_ref, cnt_ref, carry,
                  *, tm, tiles_per_batch, t_real, n_dec):
    i = pl.program_id(0)

    @pl.when(i == 0)
    def _():
        carry[...] = jnp.zeros_like(carry)

    y_a = jnp.dot(ya[...], wa[...], preferred_element_type=F32)
    o_cat = jnp.concatenate([on[0, hd] for hd in range(N_HEADS)], axis=1)
    y_b = jnp.dot(o_cat, wb[...], preferred_element_type=F32)
    mix = jax.nn.sigmoid(ga[...]) * y_a + jax.nn.sigmoid(gb[...]) * y_b
    h1 = h[...] + jnp.dot(mix.astype(BF16), wo[...], preferred_element_type=F32)
    h1_ref[...] = h1

    xn = h1 * lax.rsqrt(jnp.mean(h1 * h1, axis=-1, keepdims=True) + EPS) * n2g[...]
    xb = xn.astype(BF16)
    xf = xb.astype(F32)
    bits = pltpu.bitcast(xf, U32)
    half = D_MODEL // 2
    xp_ref[...] = bits[:, :half] | (bits[:, half:] >> 16)

    logits = jnp.dot(xb, wr[...], preferred_element_type=F32) + br[...]

    lane = lax.broadcasted_iota(I32, (tm, LANES), 1)
    work = logits
    vals, idxs, sels = [], [], []
    for _ in range(TOP_K):
        mx = jnp.max(work, axis=-1, keepdims=True)
        idx = jnp.min(jnp.where(work == mx, lane, LANES), axis=-1, keepdims=True)
        sel = lane == idx
        vals.append(mx)
        idxs.append(idx)
        sels.append(sel)
        work = jnp.where(sel, -jnp.inf, work)
    exps = [jnp.exp(v - vals[0]) for v in vals]
    denom = exps[0] + exps[1] + exps[2] + exps[3]

    pos_in_batch = (i % tiles_per_batch) * tm + lax.broadcasted_iota(I32, (tm, 1), 0)
    limit = jnp.where(i // tiles_per_batch == 0, t_real + n_dec, t_real)
    valid = pos_in_batch < limit

    onehot = jnp.zeros((tm, LANES), F32)
    for sel in sels:
        onehot = onehot + jnp.where(sel, 1.0, 0.0)
    onehot = jnp.where(valid, onehot, 0.0)
    rr = lax.broadcasted_iota(I32, (tm, tm), 0)
    cc = lax.broadcasted_iota(I32, (tm, tm), 1)
    lower = jnp.where(rr > cc, 1.0, 0.0).astype(BF16)
    before = jnp.dot(lower, onehot.astype(BF16), preferred_element_type=F32) + carry[...]

    topi = jnp.zeros((tm, LANES), I32)
    gate = jnp.zeros((tm, LANES), F32)
    rank = jnp.zeros((tm, LANES), I32)
    for j in range(TOP_K):
        rj = jnp.sum(jnp.where(sels[j], before, 0.0), axis=-1, keepdims=True).astype(I32)
        topi = jnp.where(lane == j, idxs[j], topi)
        gate = jnp.where(lane == j, exps[j] / denom, gate)
        rank = jnp.where(lane == j, rj, rank)
    topi_ref[...] = topi
    gate_ref[...] = gate
    rank_ref[...] = rank
    carry[...] = carry[...] + jnp.sum(onehot, axis=0, keepdims=True)
    cnt_ref[...] = jnp.broadcast_to(carry[...], cnt_ref.shape)


def _merge(ya2d, on, z2d, h2d, wa, wb, wo, n2g, wr, br, tm, t_pad, t_real, n_dec):
    rows = h2d.shape[0]
    tpb = t_pad // tm
    const = lambda shape: pl.BlockSpec(shape, lambda i: (0,) * len(shape), pipeline_mode=pl.Buffered(1))
    row = lambda w, dt=None: pl.BlockSpec((tm, w), lambda i: (i, 0))
    return pl.pallas_call(
        functools.partial(_merge_kernel, tm=tm, tiles_per_batch=tpb, t_real=t_real, n_dec=n_dec),
        grid=(rows // tm,),
        in_specs=[
            row(D_CONV),
            pl.BlockSpec((1, N_HEADS, tm, LANES), lambda i: (i // tpb, 0, i % tpb, 0)),
            pl.BlockSpec((tm, D_MODEL), lambda i: (i, 3)),
            pl.BlockSpec((tm, D_MODEL), lambda i: (i, 4)),
            row(D_MODEL),
            const((D_CONV, D_MODEL)), const((D_ATTN, D_MODEL)), const((D_MODEL, D_MODEL)),
            const((1, D_MODEL)), const((D_MODEL, LANES)), const((1, LANES)),
        ],
        out_specs=[row(D_MODEL), row(D_MODEL // 2), row(LANES), row(LANES), row(LANES),
                   pl.BlockSpec((8, LANES), lambda i: (0, 0))],
        out_shape=[
            jax.ShapeDtypeStruct((rows, D_MODEL), F32),
            jax.ShapeDtypeStruct((rows, D_MODEL // 2), U32),
            jax.ShapeDtypeStruct((rows, LANES), I32),
            jax.ShapeDtypeStruct((rows, LANES), F32),
            jax.ShapeDtypeStruct((rows, LANES), I32),
            jax.ShapeDtypeStruct((8, LANES), F32),
        ],
        scratch_shapes=[pltpu.VMEM((1, LANES), F32)],
        compiler_params=_cparams(("arbitrary",)),
        name="merge_route",
    )(ya2d, on, z2d, z2d, h2d, wa, wb, wo, n2g, wr, br)


def _dispatch_kernel(pos_ref, zpos_ref, x_ref, xs_hbm, zbuf, sem, zsem, *, tm):
    i = pl.program_id(0)

    @pl.when(i == 0)
    def _():
        zbuf[...] = jnp.zeros_like(zbuf)

        def zstart(e, c):
            pltpu.make_async_copy(zbuf, xs_hbm.at[pl.ds(pl.multiple_of(zpos_ref[e], SUB_ROWS), SUB_ROWS)],
                                  zsem).start()
            return c

        def zwait(e, c):
            pltpu.make_async_copy(zbuf, xs_hbm.at[pl.ds(0, SUB_ROWS)], zsem).wait()
            return c

        lax.fori_loop(0, N_EXPERTS, zstart, 0)
        lax.fori_loop(0, N_EXPERTS, zwait, 0)

    base = i * tm

    def start(t, c):
        for j in range(TOP_K):
            dst = pos_ref[(base + t) * TOP_K + j]
            pltpu.make_async_copy(x_ref.at[pl.ds(t, 1)], xs_hbm.at[pl.ds(dst, 1)], sem).start(priority=j % 2)
        return c

    def wait(t, c):
        for j in range(TOP_K):
            pltpu.make_async_copy(x_ref.at[pl.ds(0, 1)], xs_hbm.at[pl.ds(0, 1)], sem).wait()
        return c

    lax.fori_loop(0, tm, start, 0, unroll=DMA_LOOP_UNROLL)
    lax.fori_loop(0, tm, wait, 0, unroll=DMA_LOOP_UNROLL)


def _dispatch(pos_flat, zpos, xp, n_slots, tm):
    rows, width = xp.shape
    grid_spec = pltpu.PrefetchScalarGridSpec(
        num_scalar_prefetch=2,
        grid=(rows // tm,),
        in_specs=[pl.BlockSpec((tm, width), lambda i, p, z: (i, 0))],
        out_specs=pl.BlockSpec(memory_space=pl.ANY),
        scratch_shapes=[pltpu.VMEM((SUB_ROWS, width), U32),
                        pltpu.SemaphoreType.DMA(()), pltpu.SemaphoreType.DMA(())],
    )
    return pl.pallas_call(
        functools.partial(_dispatch_kernel, tm=tm),
        grid_spec=grid_spec,
        out_shape=jax.ShapeDtypeStruct((n_slots, width), U32),
        compiler_params=_cparams(("arbitrary",)),
        name="moe_dispatch",
    )(pos_flat, zpos, xp)


def _moe_kernel(item_e, item_start, item_nsub, n_items,
                xs_hbm, w1g, w1u, w2, b1g, b1u, b2,
                ys_hbm, xraw, xa, xb, yacc, sem_in, sem_out, *, nf):
    it = pl.program_id(0)
    f = pl.program_id(1)
    half = D_MODEL // 2

    @pl.when(it < n_items[0])
    def _():
        start = pl.multiple_of(item_start[it], SUB_ROWS)
        nsub = item_nsub[it]

        def in_copy(s):
            r0 = pl.multiple_of(s * SUB_ROWS, SUB_ROWS)
            return pltpu.make_async_copy(xs_hbm.at[pl.ds(start + r0, SUB_ROWS)],
                                         xraw.at[pl.ds(r0, SUB_ROWS)], sem_in.at[s])

        def out_copy(r0, m):
            rows = pl.ds(pl.multiple_of(r0, SUB_ROWS), m)
            return pltpu.make_async_copy(yacc.at[rows], ys_hbm.at[pl.ds(start + r0, m)], sem_out)

        def mlp_rows(r0, m, first, last):
            r0 = pl.multiple_of(r0, SUB_ROWS)
            rows = pl.ds(r0, m)
            if first:
                for k in range(m // SUB_ROWS):
                    sub = pl.ds(r0 + k * SUB_ROWS, SUB_ROWS)
                    in_copy(r0 // SUB_ROWS + k).wait()
                    w = xraw[sub, :]
                    xa[sub, :] = pltpu.bitcast(w & jnp.uint32(0xFFFF0000), F32).astype(BF16)
                    xb[sub, :] = pltpu.bitcast(w << 16, F32).astype(BF16)
            wg = w1g[0].astype(BF16)
            wu = w1u[0].astype(BF16)
            wd = w2[0].astype(BF16)
            a = xa[rows, :]
            b = xb[rows, :]
            hg = (jnp.dot(a, wg[:half], preferred_element_type=F32)
                  + jnp.dot(b, wg[half:], preferred_element_type=F32) + b1g[0])
            hu = (jnp.dot(a, wu[:half], preferred_element_type=F32)
                  + jnp.dot(b, wu[half:], preferred_element_type=F32) + b1u[0])
            g = jnp.minimum(hg, SWIGLU_LIMIT)
            up = jnp.clip(hu, -SWIGLU_LIMIT, SWIGLU_LIMIT)
            act = g * jax.nn.sigmoid(SWIGLU_ALPHA * g) * (up + 1.0)
            y = jnp.dot(act.astype(BF16), wd, preferred_element_type=F32)
            yacc[rows, :] = y + (b2[0] if first else yacc[rows, :])
            if last:
                out_copy(r0, m).start()

        per_big, per_mid = BIG_ROWS // SUB_ROWS, MID_ROWS // SUB_ROWS
        nbig = nsub // per_big
        mid0 = nbig * BIG_ROWS
        nmid = (nsub - nbig * per_big) // per_mid
        tail0 = nbig * per_big + nmid * per_mid

        def sweep(first, last):
            lax.fori_loop(0, nbig, lambda c, u: (mlp_rows(c * BIG_ROWS, BIG_ROWS, first, last), u)[1], 0)
            lax.fori_loop(0, nmid, lambda c, u: (mlp_rows(mid0 + c * MID_ROWS, MID_ROWS, first, last), u)[1], 0)
            lax.fori_loop(tail0, nsub, lambda s, u: (mlp_rows(s * SUB_ROWS, SUB_ROWS, first, last), u)[1], 0)

        @pl.when(f == 0)
        def _():
            lax.fori_loop(0, nsub, lambda s, c: (in_copy(s).start(), c)[1], 0)
            sweep(True, False)

        @pl.when((f > 0) & (f < nf - 1))
        def _():
            sweep(False, False)

        @pl.when(f == nf - 1)
        def _():
            sweep(False, True)
            lax.fori_loop(0, nbig, lambda c, u: (out_copy(c * BIG_ROWS, BIG_ROWS).wait(), u)[1], 0)
            lax.fori_loop(0, nmid, lambda c, u: (out_copy(mid0 + c * MID_ROWS, MID_ROWS).wait(), u)[1], 0)
            lax.fori_loop(tail0, nsub, lambda s, u: (out_copy(s * SUB_ROWS, SUB_ROWS).wait(), u)[1], 0)


def _moe_experts(items, xs, w1, b1, w2, b2, n_slots, rmax, tf):
    item_e, item_start, item_nsub, n_items = items
    ni = item_e.shape[0]
    nf = D_FF // tf
    half = D_MODEL // 2

    def fidx(it, f, n):
        return jnp.where(it < n[0], f, nf - 1)

    grid_spec = pltpu.PrefetchScalarGridSpec(
        num_scalar_prefetch=4,
        grid=(n_items[0], nf),
        in_specs=[
            pl.BlockSpec(memory_space=pl.ANY),
            pl.BlockSpec((1, D_MODEL, tf), lambda it, f, e, s, ns, n: (e[it], 0, fidx(it, f, n))),
            pl.BlockSpec((1, D_MODEL, tf), lambda it, f, e, s, ns, n: (e[it], 0, nf + fidx(it, f, n))),
            pl.BlockSpec((1, tf, D_MODEL), lambda it, f, e, s, ns, n: (e[it], fidx(it, f, n), 0)),
            pl.BlockSpec((1, 1, tf), lambda it, f, e, s, ns, n: (e[it], 0, fidx(it, f, n))),
            pl.BlockSpec((1, 1, tf), lambda it, f, e, s, ns, n: (e[it], 0, nf + fidx(it, f, n))),
            pl.BlockSpec((1, 1, D_MODEL), lambda it, f, e, s, ns, n: (e[it], 0, 0)),
        ],
        out_specs=pl.BlockSpec(memory_space=pl.ANY),
        scratch_shapes=[
            pltpu.VMEM((rmax, half), U32),
            pltpu.VMEM((rmax, half), BF16),
            pltpu.VMEM((rmax, half), BF16),
            pltpu.VMEM((rmax, D_MODEL), F32),
            pltpu.SemaphoreType.DMA((rmax // SUB_ROWS,)), pltpu.SemaphoreType.DMA(()),
        ],
    )
    return pl.pallas_call(
        functools.partial(_moe_kernel, nf=nf),
        grid_spec=grid_spec,
        out_shape=jax.ShapeDtypeStruct((n_slots, D_MODEL), F32),
        compiler_params=_cparams(("arbitrary", "arbitrary")),
        name="moe_experts",
    )(item_e, item_start, item_nsub, n_items, xs, w1, w1, w2,
      b1.reshape(N_EXPERTS, 1, 2 * D_FF), b1.reshape(N_EXPERTS, 1, 2 * D_FF), b2.reshape(N_EXPERTS, 1, D_MODEL))


def _combine_kernel(pos_ref, ys_hbm, h1_hbm, gate_hbm, y_ref, ysamp_ref, ybuf, hbuf, gbuf, sem, hsem,
                    *, tm, tiles_per_batch, t_pad, n_dec, samp_row0):
    i = pl.program_id(0)
    n_tiles = pl.num_programs(0)

    def gather_and_mix(row0, n):
        hcp = pltpu.make_async_copy(h1_hbm.at[pl.ds(row0, n)], hbuf.at[pl.ds(0, n)], hsem)
        gcp = pltpu.make_async_copy(gate_hbm.at[pl.ds(row0, n)], gbuf.at[pl.ds(0, n)], hsem)
        hcp.start()
        gcp.start()

        def start(t, c):
            for j in range(TOP_K):
                src = pos_ref[(row0 + t) * TOP_K + j]
                pltpu.make_async_copy(ys_hbm.at[pl.ds(src, 1)], ybuf.at[j, pl.ds(t, 1)], sem).start(priority=j % 2)
            return c

        def wait(t, c):
            for j in range(TOP_K):
                pltpu.make_async_copy(ys_hbm.at[pl.ds(0, 1)], ybuf.at[0, pl.ds(0, 1)], sem).wait()
            return c

        lax.fori_loop(0, n, start, 0, unroll=DMA_LOOP_UNROLL)
        hcp.wait()
        gcp.wait()
        lax.fori_loop(0, n, wait, 0, unroll=DMA_LOOP_UNROLL)
        g = gbuf[0:n, :]
        out = hbuf[0:n, :]
        for j in range(TOP_K):
            out = out + g[:, j:j + 1] * ybuf[j, 0:n, :]
        return out

    row0 = (i // tiles_per_batch) * t_pad + N_META + (i % tiles_per_batch) * tm
    y_ref[0] = gather_and_mix(pl.multiple_of(row0, 8), tm)

    @pl.when(i == n_tiles - 1)
    def _():
        ysamp_ref[...] = gather_and_mix(samp_row0, n_dec)


def _combine(pos_flat, ys, h1, gate, n_batch, seq, t_pad, n_dec, samp_row0, tm):
    tpb = seq // tm
    grid_spec = pltpu.PrefetchScalarGridSpec(
        num_scalar_prefetch=1,
        grid=(n_batch * tpb,),
        in_specs=[pl.BlockSpec(memory_space=pl.ANY)] * 3,
        out_specs=[pl.BlockSpec((1, tm, D_MODEL), lambda i, p: (i // tpb, i % tpb, 0)),
                   pl.BlockSpec((n_dec, D_MODEL), lambda i, p: (0, 0))],
        scratch_shapes=[
            pltpu.VMEM((TOP_K, tm, D_MODEL), F32),
            pltpu.VMEM((tm, D_MODEL), F32),
            pltpu.VMEM((tm, LANES), F32),
            pltpu.SemaphoreType.DMA(()), pltpu.SemaphoreType.DMA(()),
        ],
    )
    return pl.pallas_call(
        functools.partial(_combine_kernel, tm=tm, tiles_per_batch=tpb, t_pad=t_pad, n_dec=n_dec,
                          samp_row0=samp_row0),
        grid_spec=grid_spec,
        out_shape=[jax.ShapeDtypeStruct((n_batch, seq, D_MODEL), F32),
                   jax.ShapeDtypeStruct((n_dec, D_MODEL), F32)],
        compiler_params=_cparams(("arbitrary",)),
        name="moe_combine",
    )(pos_flat, ys, h1, gate)


def _t5_bucket(rel):
    n = jnp.maximum(rel, 0)
    max_exact = N_BUCKETS // 2
    nf = jnp.maximum(n, 1).astype(F32)
    large = max_exact + (jnp.log(nf / max_exact) / math.log(MAX_DISTANCE / max_exact)
                         * (N_BUCKETS - max_exact)).astype(I32)
    large = jnp.minimum(large, N_BUCKETS - 1)
    return jnp.where(n < max_exact, n, large)


def _bias_tables(rel_bias, tq):
    shifted = (rel_bias - rel_bias[N_BUCKETS - 1][None]) * LOG2E

    def bias_of(rel, out):
        onehot = jax.nn.one_hot(_t5_bucket(rel), N_BUCKETS, dtype=F32)
        return jnp.einsum("...b,bh->" + out, onehot, shifted, precision=lax.Precision.HIGHEST)

    rel0 = jnp.arange(tq)[:, None] - jnp.arange(tq)[None, :]
    diag = jnp.where((rel0 >= 0)[None], bias_of(rel0, "h..."), NEG_INF)
    tiles = jnp.stack([diag, bias_of(tq + rel0, "h...")])
    rel_last = PAGE_SIZE - jnp.arange(PAGE_SIZE)
    pbias = jnp.broadcast_to(bias_of(rel_last, "...h")[:, :, None], (PAGE_SIZE, N_HEADS, 2 * LANES))
    sbias = jnp.broadcast_to(bias_of(jnp.zeros((), I32), "...h")[:, None], (N_HEADS, 2 * LANES))
    return tiles, pbias, sbias


def _routing_tables(counts, topi, rank, valid, n_trash_rows, rmax, n_items_max, real_slots):
    padded = (counts + SUB_ROWS - 1) // SUB_ROWS * SUB_ROWS
    ends = jnp.cumsum(padded)
    off = ends - padded
    experts = jnp.arange(N_EXPERTS, dtype=I32)
    pos = rank + jnp.sum(jnp.where(topi[..., None] == experts, off, 0), axis=-1)
    trash_row = jnp.cumsum(jnp.logical_not(valid).astype(I32)) - 1
    trash = real_slots + trash_row[:, None] * TOP_K + jnp.arange(TOP_K, dtype=I32)[None]
    pos_scatter = jnp.where(valid[:, None], pos, trash).reshape(-1)
    pos_gather = jnp.where(valid[:, None], pos, 0).reshape(-1)
    zero_trash = real_slots + -(-(n_trash_rows * TOP_K) // SUB_ROWS) * SUB_ROWS
    zpos = jnp.where(counts > 0, ends - SUB_ROWS, zero_trash).astype(I32)
    per_e = (padded + rmax - 1) // rmax
    item_end = jnp.cumsum(per_e)
    n_items = item_end[-1]
    t = jnp.arange(n_items_max, dtype=I32)
    tt = jnp.minimum(t, n_items - 1)
    e_of = jnp.minimum(jnp.searchsorted(item_end, tt, side="right"), N_EXPERTS - 1).astype(I32)
    k = tt - (item_end - per_e)[e_of]
    start = off[e_of] + k * rmax
    nrows = jnp.minimum(rmax, padded[e_of] - k * rmax)
    nsub = jnp.where(t < n_items, nrows // SUB_ROWS, 0).astype(I32)
    items = (e_of, start.astype(I32), nsub, n_items.reshape(1).astype(I32))
    return pos_scatter.astype(I32), pos_gather.astype(I32), zpos, items, zero_trash + SUB_ROWS


def kernel(x_prompt, x_sample, cache_k, cache_v, state_conv, page_table, meta_tokens, rel_bias, norm1_g, w_in,
           conv_w, q_norm_g, k_norm_g, lambda_q1, lambda_k1, lambda_q2, lambda_k2, subln_g, w_branch_a,
           w_branch_b, w_out, norm2_g, w_router, b_router, w_mlp1, b_mlp1, w_mlp2, b_mlp2):
    n_batch, seq, _ = x_prompt.shape
    n_dec, t_dec, _ = x_sample.shape
    depth = cache_k.shape[0]
    assert depth == 1 and t_dec == 1 and n_dec == 8
    t_real = seq + N_META
    tq = 384
    t_pad = -(-(t_real + n_dec) // tq) * tq
    rows = n_batch * t_pad
    samp_row0 = t_real
    assert samp_row0 % 8 == 0 and t_pad % 24 == 0
    lam_init = 0.8 - 0.6 * math.exp(-0.3 * 0)

    z2d, h2d = _in_proj(x_prompt, meta_tokens, x_sample.reshape(n_dec, D_MODEL), norm1_g[0], w_in[0],
                        t_pad, tm=t_pad // 3, tn=512)
    z3 = z2d.reshape(n_batch, t_pad, D_IN)

    qg2 = jnp.tile(q_norm_g[0], 2).reshape(1, LANES)
    kg2 = jnp.tile(k_norm_g[0], 2).reshape(1, LANES)
    sg = subln_g[0].reshape(1, V_DIM)
    lams = [v[0].reshape(1, HEAD_DIM) for v in (lambda_q1, lambda_k1, lambda_q2, lambda_k2)]
    bias_tiles, pbias, sbias = _bias_tables(rel_bias, tq)

    ya, k_p, v_p, qs, kb, vb, ust = _prep(z3, conv_w[0], qg2, kg2, n_batch, t_pad, t_real, tm=tq)
    on = _prompt_attention(qs, kb, vb, bias_tiles, lams, sg, lam_init, t_pad, tq)

    row_block = samp_row0 // n_dec
    ya_s, u_s = _sample_conv(z2d, state_conv[0, :, 0], state_conv[0, :, 1], conv_w[0], row_block, n_dec)
    zs = z2d[samp_row0:samp_row0 + n_dec].reshape(n_dec, D_IN // LANES, LANES)
    o_s, k_s, v_s = _sample_attention(zs, cache_k, cache_v, page_table, pbias, sbias, qg2, kg2, lams, sg,
                                      lam_init, pages_per_step=16)

    ya = ya.at[0, samp_row0:samp_row0 + n_dec].set(ya_s.astype(BF16))
    on = on.at[0, :, samp_row0:samp_row0 + n_dec].set(o_s.transpose(1, 0, 2).astype(BF16))

    wr = jnp.pad(w_router[0], ((0, 0), (0, LANES - N_EXPERTS)))
    br = jnp.concatenate([b_router[0], jnp.full((LANES - N_EXPERTS,), NEG_INF, F32)]).reshape(1, LANES)
    h1, xp, topi, gate, rank, cnt = _merge(
        ya.reshape(rows, D_CONV), on, z2d, h2d,
        w_branch_a[0].astype(BF16), w_branch_b[0].astype(BF16), w_out[0].astype(BF16),
        norm2_g[0].reshape(1, D_MODEL), wr.astype(BF16), br, tm=tq, t_pad=t_pad, t_real=t_real, n_dec=n_dec)

    rmax = 1536
    n_assign = (n_batch * t_real + n_dec) * TOP_K
    real_slots = -(-(n_assign + N_EXPERTS * (SUB_ROWS - 1)) // SUB_ROWS) * SUB_ROWS
    n_items_max = N_EXPERTS + real_slots // rmax
    pos_in_batch = jnp.arange(rows, dtype=I32) % t_pad
    valid = (pos_in_batch < t_real) | ((jnp.arange(rows) < t_pad) & (pos_in_batch < t_real + n_dec))
    pos_scatter, pos_gather, zpos, items, n_slots = _routing_tables(
        cnt[0, :N_EXPERTS].astype(I32), topi[:, :TOP_K], rank[:, :TOP_K], valid,
        rows - n_assign // TOP_K, rmax, n_items_max, real_slots)

    xs = _dispatch(pos_scatter, zpos, xp, n_slots, tm=tq)
    ys = _moe_experts(items, xs, w_mlp1[0], b_mlp1[0], w_mlp2[0], b_mlp2[0], n_slots, rmax, tf=256)
    y_prompt, y_samp = _combine(pos_gather, ys, h1, gate, n_batch, seq, t_pad, n_dec, samp_row0, tm=512)

    return (y_prompt,
            y_samp.reshape(n_dec, 1, D_MODEL),
            k_p.reshape(1, n_batch, t_real, N_HEADS, 2 * HEAD_DIM),
            v_p.reshape(1, n_batch, t_real, N_HEADS, V_DIM),
            ust[:, 6:8].reshape(1, n_batch, CONV_K - 1, D_CONV),
            k_s.reshape(1, n_dec, 1, N_HEADS, 2 * HEAD_DIM),
            v_s.reshape(1, n_dec, 1, N_HEADS, V_DIM),
            jnp.stack([state_conv[0, :, 1], u_s], axis=1).reshape(1, n_dec, CONV_K - 1, D_CONV))
```
